```python
import jax, jax.numpy as jnp
from jax import lax
import numpy as np

D_MODEL = 1024
BATCH = 8
SEQ = 2048
DEPTH = 1
DEC_BATCH = 128
DEC_SEQ = 4
PAST_LEN = 8192
PAGE_SIZE = 128

MLA_HEADS = 8
MLA_NOPE = 64
MLA_ROPE = 32
MLA_V = 64
MLA_KV_LORA = 256
MLA_Q_LORA = 384
MLA_SCALE = (MLA_NOPE + MLA_ROPE) ** -0.5
ROPE_BASE = 10000.0
ML_HEADS = 4
ML_DH = 128
ML_CHUNK = 64
MLA_WIDTH = MLA_HEADS * MLA_V
ML_WIDTH = ML_HEADS * ML_DH
D_MIX = MLA_WIDTH + ML_WIDTH
N_MEM = 256
MEM_HEADS = 4
MEM_HD = D_MODEL // MEM_HEADS
PEER_HEADS = 8
PEER_NKEYS = 128
PEER_N = PEER_NKEYS * PEER_NKEYS
PEER_DKEY = 128
PEER_TOPK = 16
PEER_BLOCK = 256
ATTN_BLOCK = 128
LN_EPS = 1e-5
RMS_EPS = 1e-6
ALPHA = (2 * DEPTH) ** 0.25
BETA = (8 * DEPTH) ** -0.25
IN_SIZES = (MLA_Q_LORA, MLA_KV_LORA, MLA_ROPE, ML_WIDTH, ML_WIDTH, ML_WIDTH, ML_HEADS, ML_HEADS, ML_WIDTH)
IN_TOTAL = sum(IN_SIZES)
IN_SPLITS = [int(s) for s in np.cumsum(IN_SIZES)[:-1]]

kernel_name = 'hybrid_mla_mlstm_peer_step'


def layer_norm(x, g, b):
    xf = x.astype(jnp.float32)
    mu = xf.mean(-1, keepdims=True)
    var = jnp.mean(jnp.square(xf - mu), -1, keepdims=True)
    return ((xf - mu) * lax.rsqrt(var + LN_EPS) * g + b).astype(x.dtype)


def rms_norm(x, g):
    xf = x.astype(jnp.float32)
    return (xf * lax.rsqrt(jnp.mean(xf * xf, -1, keepdims=True) + RMS_EPS) * g).astype(x.dtype)


def rope_angles(pos):
    inv = 1.0 / (ROPE_BASE ** (jnp.arange(0, MLA_ROPE, 2, dtype=jnp.float32) / MLA_ROPE))
    ang = pos.astype(jnp.float32)[:, None] * inv[None, :]
    return jnp.cos(ang), jnp.sin(ang)


def apply_rope(x, cos, sin):
    half = MLA_ROPE // 2
    x1, x2 = x[..., :half], x[..., half:]
    c = cos.astype(x.dtype)
    s = sin.astype(x.dtype)
    return jnp.concatenate([x1 * c - x2 * s, x1 * s + x2 * c], -1)


def mixer_projections(x, pos, lw):
    z = jnp.einsum('bsd,de->bse', x, lw['w_in'])
    c_q, c_kv, k_r, mq, mk, mv, i_pre, f_pre, o_pre = jnp.split(z, IN_SPLITS, axis=-1)
    cos, sin = rope_angles(pos)
    q = jnp.einsum('bsc,chd->bshd', rms_norm(c_q, lw['g_q']), lw['w_uq'])
    q_rope = apply_rope(q[..., MLA_NOPE:], cos[:, None], sin[:, None])
    q_lat = jnp.einsum('bshn,chn->bshc', q[..., :MLA_NOPE], lw['w_uk'])
    kv_lat = rms_norm(c_kv, lw['g_kv'])
    k_rope = apply_rope(k_r, cos, sin)
    B, S = x.shape[:2]
    heads = lambda t: t.reshape(B, S, ML_HEADS, ML_DH).transpose(0, 2, 1, 3).astype(jnp.float32)
    ml_q = heads(mq)
    ml_k = heads(mk) * (ML_DH ** -0.5)
    ml_v = heads(mv)
    ig = (i_pre + lw['b_i']).astype(jnp.float32).transpose(0, 2, 1)
    lf = jax.nn.log_sigmoid((f_pre + lw['b_f']).astype(jnp.float32)).transpose(0, 2, 1)
    o_gate = jax.nn.sigmoid(o_pre)
    return (q_lat, q_rope, kv_lat, k_rope), (ml_q, ml_k, ml_v, ig, lf), o_gate


def mla_attend_prompt(q_lat, q_rope, kv_lat, k_rope, w_uv):
    B, S = q_lat.shape[:2]
    nb = S // ATTN_BLOCK
    qlb = q_lat.reshape(B, nb, ATTN_BLOCK, MLA_HEADS, MLA_KV_LORA).swapaxes(0, 1)
    qrb = q_rope.reshape(B, nb, ATTN_BLOCK, MLA_HEADS, MLA_ROPE).swapaxes(0, 1)
    kpos = jnp.arange(S)

    def block(args):
        ql, qr, start = args
        s = (jnp.einsum('bqhc,bkc->bhqk', ql, kv_lat) + jnp.einsum('bqhr,bkr->bhqk', qr, k_rope)).astype(jnp.float32) * MLA_SCALE
        qpos = start + jnp.arange(ATTN_BLOCK)
        s = jnp.where(kpos[None, :] <= qpos[:, None], s, -jnp.inf)
        p = jax.nn.softmax(s, axis=-1).astype(kv_lat.dtype)
        return jnp.einsum('bhqk,bkc->bqhc', p, kv_lat)

    o = lax.map(block, (qlb, qrb, jnp.arange(nb) * ATTN_BLOCK))
    o = o.swapaxes(0, 1).reshape(B, S, MLA_HEADS, MLA_KV_LORA)
    return jnp.einsum('bshc,chv->bshv', o, w_uv).reshape(B, S, MLA_WIDTH)


def mla_attend_sample(q_lat, q_rope, kv_lat, k_rope, pool_lat, pool_rope, page_table, w_uv):
    B, T = q_lat.shape[:2]
    past_lat = pool_lat[page_table].reshape(B, -1, MLA_KV_LORA)
    past_rope = pool_rope[page_table].reshape(B, -1, MLA_ROPE)
    s_past = jnp.einsum('bqhc,bkc->bhqk', q_lat, past_lat) + jnp.einsum('bqhr,bkr->bhqk', q_rope, past_rope)
    s_new = jnp.einsum('bqhc,bkc->bhqk', q_lat, kv_lat) + jnp.einsum('bqhr,bkr->bhqk', q_rope, k_rope)
    causal = jnp.arange(T)[None, :] <= jnp.arange(T)[:, None]
    s_new = jnp.where(causal, s_new.astype(jnp.float32), -jnp.inf)
    s = jnp.concatenate([s_past.astype(jnp.float32), s_new], -1) * MLA_SCALE
    p = jax.nn.softmax(s, axis=-1).astype(kv_lat.dtype)
    P = past_lat.shape[1]
    o = jnp.einsum('bhqk,bkc->bqhc', p[..., :P], past_lat) + jnp.einsum('bhqk,bkc->bqhc', p[..., P:], kv_lat)
    return jnp.einsum('bqhc,chv->bqhv', o, w_uv).reshape(B, T, MLA_WIDTH)


def mlstm_chunk(carry, inp):
    C, n, m = carry
    q, k, v, ig, lf = inp
    L = q.shape[2]
    b = jnp.cumsum(lf, axis=-1)
    causal = jnp.arange(L)[None, :] <= jnp.arange(L)[:, None]
    D = jnp.where(causal, b[..., :, None] - b[..., None, :] + ig[..., None, :], -jnp.inf)
    inter = b + m[..., None]
    m_t = jnp.maximum(inter, D.max(-1))
    A = jnp.exp(D - m_t[..., None]) * jnp.einsum('bhtd,bhsd->bhts', q, k)
    w_inter = jnp.exp(inter - m_t)
    num = w_inter[..., None] * jnp.einsum('bhtd,bhde->bhte', q, C) + jnp.einsum('bhts,bhse->bhte', A, v)
    den = w_inter * jnp.einsum('bhtd,bhd->bht', q, n) + A.sum(-1)
    h = num / jnp.maximum(jnp.abs(den), jnp.exp(-m_t))[..., None]
    b_end = b[..., -1]
    dec = b_end[..., None] - b + ig
    m_new = jnp.maximum(b_end + m, dec.max(-1))
    a_prev = jnp.exp(b_end + m - m_new)
    w_row = jnp.exp(dec - m_new[..., None])
    C_new = a_prev[..., None, None] * C + jnp.einsum('bhs,bhsd,bhse->bhde', w_row, k, v)
    n_new = a_prev[..., None] * n + jnp.einsum('bhs,bhsd->bhd', w_row, k)
    return (C_new, n_new, m_new), h


def mlstm_prompt(q, k, v, ig, lf):
    B, H, S, d = q.shape
    nc = S // ML_CHUNK
    ch = lambda t: jnp.moveaxis(t.reshape((B, H, nc, ML_CHUNK) + t.shape[3:]), 2, 0)
    init = (jnp.zeros((B, H, d, d), jnp.float32), jnp.zeros((B, H, d), jnp.float32), jnp.zeros((B, H), jnp.float32))
    state, h = lax.scan(mlstm_chunk, init, (ch(q), ch(k), ch(v), ch(ig), ch(lf)))
    h = jnp.moveaxis(h, 0, 2).reshape(B, H, S, d)
    return h, state


def mem_kv(mem, w_mk, w_mv):
    return jnp.einsum('bmd,dhe->bmhe', mem, w_mk), jnp.einsum('bmd,dhe->bmhe', mem, w_mv)


def mem_attend(x, mem_k, mem_v, w_mq, w_mo):
    q = jnp.einsum('bsd,dhe->bshe', x, w_mq)
    s = jnp.einsum('bshe,bmhe->bhsm', q, mem_k).astype(jnp.float32) * (MEM_HD ** -0.5)
    p = jax.nn.softmax(s, axis=-1).astype(x.dtype)
    o = jnp.einsum('bhsm,bmhe->bshe', p, mem_v)
    return jnp.einsum('bshe,hed->bsd', o, w_mo)


def peer(x, w_pq, sub_k1, sub_k2, peer_u, peer_v):
    shp = x.shape
    xt = x.reshape(-1, D_MODEL)
    T = xt.shape[0]
    nb = -(-T // PEER_BLOCK)
    xt = jnp.pad(xt, ((0, nb * PEER_BLOCK - T), (0, 0)))
    half = PEER_DKEY // 2

    def block(xb):
        q = jnp.einsum('td,dhk->thk', xb, w_pq)
        s1 = jnp.einsum('thk,nk->thn', q[..., :half], sub_k1).astype(jnp.float32)
        s2 = jnp.einsum('thk,nk->thn', q[..., half:], sub_k2).astype(jnp.float32)
        v1, i1 = lax.top_k(s1, PEER_TOPK)
        v2, i2 = lax.top_k(s2, PEER_TOPK)
        cand = (v1[..., :, None] + v2[..., None, :]).reshape(xb.shape[0], PEER_HEADS, PEER_TOPK * PEER_TOPK)
        cidx = (i1[..., :, None] * PEER_NKEYS + i2[..., None, :]).reshape(xb.shape[0], PEER_HEADS, PEER_TOPK * PEER_TOPK)
        sc, j = lax.top_k(cand, PEER_TOPK)
        e = jnp.take_along_axis(cidx, j, axis=-1)
        g = jax.nn.softmax(sc, axis=-1).astype(xb.dtype)
        a = g * jax.nn.gelu(jnp.einsum('td,thkd->thk', xb, peer_u[e]), approximate=False)
        return jnp.einsum('thk,thkd->td', a, peer_v[e])

    y = lax.map(block, xt.reshape(nb, PEER_BLOCK, D_MODEL))
    return y.reshape(-1, D_MODEL)[:T].reshape(shp)


def finish_layer(x, mla_o, ml_h, o_gate, mem_k, mem_v, lw):
    B, S = x.shape[:2]
    ml_o = o_gate * ml_h.transpose(0, 2, 1, 3).reshape(B, S, ML_WIDTH).astype(x.dtype)
    mix = jnp.einsum('bse,ed->bsd', jnp.concatenate([mla_o, ml_o], -1), lw['w_out'])
    x = layer_norm(ALPHA * x + mix, lw['ln1_g'], lw['ln1_b'])
    x = layer_norm(ALPHA * x + mem_attend(x, mem_k, mem_v, lw['w_mq'], lw['w_mo']), lw['ln2_g'], lw['ln2_b'])
    y = peer(x, lw['w_pq'], lw['sub_k1'], lw['sub_k2'], lw['peer_u'], lw['peer_v'])
    return layer_norm(ALPHA * x + y, lw['ln3_g'], lw['ln3_b'])


def setup_inputs(seed: int = 0) -> dict:
    key = jax.random.key(seed)
    ks = iter(jax.random.split(key, 64))
    nrm = lambda shape, scale: jax.random.normal(next(ks), shape, jnp.float32) * scale
    gain = lambda shape: 1.0 + nrm(shape, 0.01)
    L = DEPTH
    n_pages = PAST_LEN // PAGE_SIZE
    n_used = DEC_BATCH * n_pages
    n_pool = n_used + max(1, n_used // 4)
    page_table = jax.random.permutation(next(ks), n_pool)[:n_used].reshape(DEC_BATCH, n_pages).astype(jnp.int32)
    col_scale = jnp.concatenate([
        jnp.ones((MLA_Q_LORA + MLA_KV_LORA + MLA_ROPE + 2 * ML_WIDTH,), jnp.float32),
        jnp.full((ML_WIDTH,), BETA, jnp.float32),
        jnp.ones((2 * ML_HEADS + ML_WIDTH,), jnp.float32)])
    return {
        'x_prompt': nrm((BATCH, SEQ, D_MODEL), 1.0),
        'x_sample': nrm((DEC_BATCH, DEC_SEQ, D_MODEL), 1.0),
        'cache_kv_latent': nrm((L, n_pool, PAGE_SIZE, MLA_KV_LORA), 1.0),
        'cache_k_rope': nrm((L, n_pool, PAGE_SIZE, MLA_ROPE), 1.0),
        'state_C': nrm((L, DEC_BATCH, ML_HEADS, ML_DH, ML_DH), 0.5),
        'state_n': nrm((L, DEC_BATCH, ML_HEADS, ML_DH), 0.5),
        'state_m': nrm((L, DEC_BATCH, ML_HEADS), 1.0),
        'cache_mem_k': nrm((L, DEC_BATCH, N_MEM, MEM_HEADS, MEM_HD), 1.0),
        'cache_mem_v': nrm((L, DEC_BATCH, N_MEM, MEM_HEADS, MEM_HD), 1.0),
        'page_table': page_table,
        'mem_prompt': nrm((BATCH, N_MEM, D_MODEL), 1.0),
        'ln0_g': gain((D_MODEL,)),
        'ln0_b': nrm((D_MODEL,), 0.01),
        'w_in': nrm((L, D_MODEL, IN_TOTAL), D_MODEL ** -0.5) * col_scale,
        'b_i': nrm((L, ML_HEADS), 0.1),
        'b_f': 3.0 + 3.0 * jax.random.uniform(next(ks), (L, ML_HEADS), jnp.float32),
        'g_q': gain((L, MLA_Q_LORA)),
        'w_uq': nrm((L, MLA_Q_LORA, MLA_HEADS, MLA_NOPE + MLA_ROPE), MLA_Q_LORA ** -0.5),
        'g_kv': gain((L, MLA_KV_LORA)),
        'w_uk': nrm((L, MLA_KV_LORA, MLA_HEADS, MLA_NOPE), MLA_KV_LORA ** -0.5),
        'w_uv': nrm((L, MLA_KV_LORA, MLA_HEADS, MLA_V), BETA * MLA_KV_LORA ** -0.5),
        'w_out': nrm((L, D_MIX, D_MODEL), BETA * D_MIX ** -0.5),
        'ln1_g': gain((L, D_MODEL)),
        'ln1_b': nrm((L, D_MODEL), 0.01),
        'w_mq': nrm((L, D_MODEL, MEM_HEADS, MEM_HD), D_MODEL ** -0.5),
        'w_mk': nrm((L, D_MODEL, MEM_HEADS, MEM_HD), D_MODEL ** -0.5),
        'w_mv': nrm((L, D_MODEL, MEM_HEADS, MEM_HD), BETA * D_MODEL ** -0.5),
        'w_mo': nrm((L, MEM_HEADS, MEM_HD, D_MODEL), BETA * D_MODEL ** -0.5),
        'ln2_g': gain((L, D_MODEL)),
        'ln2_b': nrm((L, D_MODEL), 0.01),
        'w_pq': nrm((L, D_MODEL, PEER_HEADS, PEER_DKEY), D_MODEL ** -0.5),
        'sub_k1': nrm((L, PEER_NKEYS, PEER_DKEY // 2), (PEER_DKEY // 2) ** -0.5),
        'sub_k2': nrm((L, PEER_NKEYS, PEER_DKEY // 2), (PEER_DKEY // 2) ** -0.5),
        'peer_u': nrm((L, PEER_N, D_MODEL), D_MODEL ** -0.5),
        'peer_v': nrm((L, PEER_N, D_MODEL), BETA * PEER_HEADS ** -0.5),
        'ln3_g': gain((L, D_MODEL)),
        'ln3_b': nrm((L, D_MODEL), 0.01),
    }


def reference(x_prompt, x_sample, cache_kv_latent, cache_k_rope, state_C, state_n, state_m,
              cache_mem_k, cache_mem_v, page_table, mem_prompt, ln0_g, ln0_b, w_in, b_i, b_f,
              g_q, w_uq, g_kv, w_uk, w_uv, w_out, ln1_g, ln1_b, w_mq, w_mk, w_mv, w_mo,
              ln2_g, ln2_b, w_pq, sub_k1, sub_k2, peer_u, peer_v, ln3_g, ln3_b):
    S = x_prompt.shape[1]
    T = x_sample.shape[1]
    past = page_table.shape[1] * PAGE_SIZE
    pos_p = jnp.arange(S)
    pos_s = past + jnp.arange(T)
    xp = layer_norm(x_prompt, ln0_g, ln0_b)
    xs = layer_norm(x_sample, ln0_g, ln0_b)
    kvl_p, kr_p, C_p, n_p, m_p, mk_p, mv_p = [], [], [], [], [], [], []
    kvl_s, kr_s, C_s, n_s, m_s = [], [], [], [], []
    for l in range(DEPTH):
        lw = {'w_in': w_in[l], 'b_i': b_i[l], 'b_f': b_f[l], 'g_q': g_q[l], 'w_uq': w_uq[l],
              'g_kv': g_kv[l], 'w_uk': w_uk[l], 'w_out': w_out[l], 'ln1_g': ln1_g[l], 'ln1_b': ln1_b[l],
              'w_mq': w_mq[l], 'w_mo': w_mo[l], 'ln2_g': ln2_g[l], 'ln2_b': ln2_b[l], 'w_pq': w_pq[l],
              'sub_k1': sub_k1[l], 'sub_k2': sub_k2[l], 'peer_u': peer_u[l], 'peer_v': peer_v[l],
              'ln3_g': ln3_g[l], 'ln3_b': ln3_b[l]}
        (ql, qr, kvl, kr), ml_in, og = mixer_projections(xp, pos_p, lw)
        mla_o = mla_attend_prompt(ql, qr, kvl, kr, w_uv[l])
        ml_h, (C, n, m) = mlstm_prompt(*ml_in)
        mk, mv = mem_kv(mem_prompt, w_mk[l], w_mv[l])
        xp = finish_layer(xp, mla_o, ml_h, og, mk, mv, lw)
        kvl_p.append(kvl); kr_p.append(kr); C_p.append(C); n_p.append(n); m_p.append(m)
        mk_p.append(mk); mv_p.append(mv)
        (ql, qr, kvl, kr), ml_in, og = mixer_projections(xs, pos_s, lw)
        mla_o = mla_attend_sample(ql, qr, kvl, kr, cache_kv_latent[l], cache_k_rope[l], page_table, w_uv[l])
        carry = (state_C[l].astype(jnp.float32), state_n[l].astype(jnp.float32), state_m[l].astype(jnp.float32))
        (C, n, m), ml_h = mlstm_chunk(carry, ml_in)
        xs = finish_layer(xs, mla_o, ml_h, og, cache_mem_k[l], cache_mem_v[l], lw)
        kvl_s.append(kvl); kr_s.append(kr); C_s.append(C); n_s.append(n); m_s.append(m)
    return (xp, xs, jnp.stack(kvl_p), jnp.stack(kr_p), jnp.stack(C_p), jnp.stack(n_p), jnp.stack(m_p),
            jnp.stack(mk_p), jnp.stack(mv_p), jnp.stack(kvl_s), jnp.stack(kr_s), jnp.stack(C_s),
            jnp.stack(n_s), jnp.stack(m_s))
```

```python
import functools

import jax, jax.numpy as jnp
from jax import lax
import numpy as np
from jax.experimental import pallas as pl
from jax.experimental.pallas import tpu as pltpu

D_MODEL = 1024
BATCH = 8
SEQ = 2048
DEPTH = 1
DEC_BATCH = 128
DEC_SEQ = 4
PAST_LEN = 8192
PAGE_SIZE = 128

MLA_HEADS = 8
MLA_NOPE = 64
MLA_ROPE = 32
MLA_V = 64
MLA_KV_LORA = 256
MLA_Q_LORA = 384
MLA_SCALE = (MLA_NOPE + MLA_ROPE) ** -0.5
ROPE_BASE = 10000.0
ML_HEADS = 4
ML_DH = 128
ML_CHUNK = 64
MLA_WIDTH = MLA_HEADS * MLA_V
ML_WIDTH = ML_HEADS * ML_DH
D_MIX = MLA_WIDTH + ML_WIDTH
N_MEM = 256
MEM_HEADS = 4
MEM_HD = D_MODEL // MEM_HEADS
PEER_HEADS = 8
PEER_NKEYS = 128
PEER_N = PEER_NKEYS * PEER_NKEYS
PEER_DKEY = 128
PEER_TOPK = 16
PEER_BLOCK = 256
ATTN_BLOCK = 128
LN_EPS = 1e-5
RMS_EPS = 1e-6
ALPHA = (2 * DEPTH) ** 0.25
BETA = (8 * DEPTH) ** -0.25
IN_SIZES = (MLA_Q_LORA, MLA_KV_LORA, MLA_ROPE, ML_WIDTH, ML_WIDTH, ML_WIDTH, ML_HEADS, ML_HEADS, ML_WIDTH)
IN_TOTAL = sum(IN_SIZES)
IN_SPLITS = [int(s) for s in np.cumsum(IN_SIZES)[:-1]]


def _ln_kernel(x_ref, g_ref, b_ref, o_ref):
    x = x_ref[...]
    mu = jnp.mean(x, axis=-1, keepdims=True)
    xc = x - mu
    var = jnp.mean(xc * xc, axis=-1, keepdims=True)
    o_ref[...] = xc * lax.rsqrt(var + LN_EPS) * g_ref[...] + b_ref[...]


def pallas_layer_norm(x, g, b, rows=512):
    shp = x.shape
    x2 = x.reshape(-1, shp[-1])
    n = x2.shape[0]
    out = pl.pallas_call(
        _ln_kernel,
        grid=(n // rows,),
        in_specs=[pl.BlockSpec((rows, shp[-1]), lambda i: (i, 0)),
                  pl.BlockSpec((1, shp[-1]), lambda i: (0, 0)),
                  pl.BlockSpec((1, shp[-1]), lambda i: (0, 0))],
        out_specs=pl.BlockSpec((rows, shp[-1]), lambda i: (i, 0)),
        out_shape=jax.ShapeDtypeStruct(x2.shape, x2.dtype),
        name="ln0",
    )(x2, g.reshape(1, -1), b.reshape(1, -1))
    return out.reshape(shp)


def layer_norm(x, g, b):
    xf = x.astype(jnp.float32)
    mu = xf.mean(-1, keepdims=True)
    var = jnp.mean(jnp.square(xf - mu), -1, keepdims=True)
    return ((xf - mu) * lax.rsqrt(var + LN_EPS) * g + b).astype(x.dtype)


def rms_norm(x, g):
    xf = x.astype(jnp.float32)
    return (xf * lax.rsqrt(jnp.mean(xf * xf, -1, keepdims=True) + RMS_EPS) * g).astype(x.dtype)


def rope_angles(pos):
    inv = 1.0 / (ROPE_BASE ** (jnp.arange(0, MLA_ROPE, 2, dtype=jnp.float32) / MLA_ROPE))
    ang = pos.astype(jnp.float32)[:, None] * inv[None, :]
    return jnp.cos(ang), jnp.sin(ang)


def apply_rope(x, cos, sin):
    half = MLA_ROPE // 2
    x1, x2 = x[..., :half], x[..., half:]
    c = cos.astype(x.dtype)
    s = sin.astype(x.dtype)
    return jnp.concatenate([x1 * c - x2 * s, x1 * s + x2 * c], -1)


def mixer_projections(x, pos, lw):
    z = jnp.einsum('bsd,de->bse', x, lw['w_in'])
    c_q, c_kv, k_r, mq, mk, mv, i_pre, f_pre, o_pre = jnp.split(z, IN_SPLITS, axis=-1)
    cos, sin = rope_angles(pos)
    q = jnp.einsum('bsc,chd->bshd', rms_norm(c_q, lw['g_q']), lw['w_uq'])
    q_rope = apply_rope(q[..., MLA_NOPE:], cos[:, None], sin[:, None])
    q_lat = jnp.einsum('bshn,chn->bshc', q[..., :MLA_NOPE], lw['w_uk'])
    kv_lat = rms_norm(c_kv, lw['g_kv'])
    k_rope = apply_rope(k_r, cos, sin)
    B, S = x.shape[:2]
    heads = lambda t: t.reshape(B, S, ML_HEADS, ML_DH).transpose(0, 2, 1, 3).astype(jnp.float32)
    ml_q = heads(mq)
    ml_k = heads(mk) * (ML_DH ** -0.5)
    ml_v = heads(mv)
    ig = (i_pre + lw['b_i']).astype(jnp.float32).transpose(0, 2, 1)
    lf = jax.nn.log_sigmoid((f_pre + lw['b_f']).astype(jnp.float32)).transpose(0, 2, 1)
    o_gate = jax.nn.sigmoid(o_pre)
    return (q_lat, q_rope, kv_lat, k_rope), (ml_q, ml_k, ml_v, ig, lf), o_gate


def mla_attend_prompt(q_lat, q_rope, kv_lat, k_rope, w_uv):
    B, S = q_lat.shape[:2]
    nb = S // ATTN_BLOCK
    qlb = q_lat.reshape(B, nb, ATTN_BLOCK, MLA_HEADS, MLA_KV_LORA).swapaxes(0, 1)
    qrb = q_rope.reshape(B, nb, ATTN_BLOCK, MLA_HEADS, MLA_ROPE).swapaxes(0, 1)
    kpos = jnp.arange(S)

    def block(args):
        ql, qr, start = args
        s = (jnp.einsum('bqhc,bkc->bhqk', ql, kv_lat) + jnp.einsum('bqhr,bkr->bhqk', qr, k_rope)).astype(jnp.float32) * MLA_SCALE
        qpos = start + jnp.arange(ATTN_BLOCK)
        s = jnp.where(kpos[None, :] <= qpos[:, None], s, -jnp.inf)
        p = jax.nn.softmax(s, axis=-1).astype(kv_lat.dtype)
        return jnp.einsum('bhqk,bkc->bqhc', p, kv_lat)

    o = lax.map(block, (qlb, qrb, jnp.arange(nb) * ATTN_BLOCK))
    o = o.swapaxes(0, 1).reshape(B, S, MLA_HEADS, MLA_KV_LORA)
    return jnp.einsum('bshc,chv->bshv', o, w_uv).reshape(B, S, MLA_WIDTH)


def mla_attend_sample(q_lat, q_rope, kv_lat, k_rope, pool_lat, pool_rope, page_table, w_uv):
    B, T = q_lat.shape[:2]
    past_lat = pool_lat[page_table].reshape(B, -1, MLA_KV_LORA)
    past_rope = pool_rope[page_table].reshape(B, -1, MLA_ROPE)
    s_past = jnp.einsum('bqhc,bkc->bhqk', q_lat, past_lat) + jnp.einsum('bqhr,bkr->bhqk', q_rope, past_rope)
    s_new = jnp.einsum('bqhc,bkc->bhqk', q_lat, kv_lat) + jnp.einsum('bqhr,bkr->bhqk', q_rope, k_rope)
    causal = jnp.arange(T)[None, :] <= jnp.arange(T)[:, None]
    s_new = jnp.where(causal, s_new.astype(jnp.float32), -jnp.inf)
    s = jnp.concatenate([s_past.astype(jnp.float32), s_new], -1) * MLA_SCALE
    p = jax.nn.softmax(s, axis=-1).astype(kv_lat.dtype)
    P = past_lat.shape[1]
    o = jnp.einsum('bhqk,bkc->bqhc', p[..., :P], past_lat) + jnp.einsum('bhqk,bkc->bqhc', p[..., P:], kv_lat)
    return jnp.einsum('bqhc,chv->bqhv', o, w_uv).reshape(B, T, MLA_WIDTH)


def mlstm_chunk(carry, inp):
    C, n, m = carry
    q, k, v, ig, lf = inp
    L = q.shape[2]
    b = jnp.cumsum(lf, axis=-1)
    causal = jnp.arange(L)[None, :] <= jnp.arange(L)[:, None]
    D = jnp.where(causal, b[..., :, None] - b[..., None, :] + ig[..., None, :], -jnp.inf)
    inter = b + m[..., None]
    m_t = jnp.maximum(inter, D.max(-1))
    A = jnp.exp(D - m_t[..., None]) * jnp.einsum('bhtd,bhsd->bhts', q, k)
    w_inter = jnp.exp(inter - m_t)
    num = w_inter[..., None] * jnp.einsum('bhtd,bhde->bhte', q, C) + jnp.einsum('bhts,bhse->bhte', A, v)
    den = w_inter * jnp.einsum('bhtd,bhd->bht', q, n) + A.sum(-1)
    h = num / jnp.maximum(jnp.abs(den), jnp.exp(-m_t))[..., None]
    b_end = b[..., -1]
    dec = b_end[..., None] - b + ig
    m_new = jnp.maximum(b_end + m, dec.max(-1))
    a_prev = jnp.exp(b_end + m - m_new)
    w_row = jnp.exp(dec - m_new[..., None])
    C_new = a_prev[..., None, None] * C + jnp.einsum('bhs,bhsd,bhse->bhde', w_row, k, v)
    n_new = a_prev[..., None] * n + jnp.einsum('bhs,bhsd->bhd', w_row, k)
    return (C_new, n_new, m_new), h


def mlstm_prompt(q, k, v, ig, lf):
    B, H, S, d = q.shape
    nc = S // ML_CHUNK
    ch = lambda t: jnp.moveaxis(t.reshape((B, H, nc, ML_CHUNK) + t.shape[3:]), 2, 0)
    init = (jnp.zeros((B, H, d, d), jnp.float32), jnp.zeros((B, H, d), jnp.float32), jnp.zeros((B, H), jnp.float32))
    state, h = lax.scan(mlstm_chunk, init, (ch(q), ch(k), ch(v), ch(ig), ch(lf)))
    h = jnp.moveaxis(h, 0, 2).reshape(B, H, S, d)
    return h, state


def mem_kv(mem, w_mk, w_mv):
    return jnp.einsum('bmd,dhe->bmhe', mem, w_mk), jnp.einsum('bmd,dhe->bmhe', mem, w_mv)


def mem_attend(x, mem_k, mem_v, w_mq, w_mo):
    q = jnp.einsum('bsd,dhe->bshe', x, w_mq)
    s = jnp.einsum('bshe,bmhe->bhsm', q, mem_k).astype(jnp.float32) * (MEM_HD ** -0.5)
    p = jax.nn.softmax(s, axis=-1).astype(x.dtype)
    o = jnp.einsum('bhsm,bmhe->bshe', p, mem_v)
    return jnp.einsum('bshe,hed->bsd', o, w_mo)


def peer(x, w_pq, sub_k1, sub_k2, peer_u, peer_v):
    shp = x.shape
    xt = x.reshape(-1, D_MODEL)
    T = xt.shape[0]
    nb = -(-T // PEER_BLOCK)
    xt = jnp.pad(xt, ((0, nb * PEER_BLOCK - T), (0, 0)))
    half = PEER_DKEY // 2

    def block(xb):
        q = jnp.einsum('td,dhk->thk', xb, w_pq)
        s1 = jnp.einsum('thk,nk->thn', q[..., :half], sub_k1).astype(jnp.float32)
        s2 = jnp.einsum('thk,nk->thn', q[..., half:], sub_k2).astype(jnp.float32)
        v1, i1 = lax.top_k(s1, PEER_TOPK)
        v2, i2 = lax.top_k(s2, PEER_TOPK)
        cand = (v1[..., :, None] + v2[..., None, :]).reshape(xb.shape[0], PEER_HEADS, PEER_TOPK * PEER_TOPK)
        cidx = (i1[..., :, None] * PEER_NKEYS + i2[..., None, :]).reshape(xb.shape[0], PEER_HEADS, PEER_TOPK * PEER_TOPK)
        sc, j = lax.top_k(cand, PEER_TOPK)
        e = jnp.take_along_axis(cidx, j, axis=-1)
        g = jax.nn.softmax(sc, axis=-1).astype(xb.dtype)
        a = g * jax.nn.gelu(jnp.einsum('td,thkd->thk', xb, peer_u[e]), approximate=False)
        return jnp.einsum('thk,thkd->td', a, peer_v[e])

    y = lax.map(block, xt.reshape(nb, PEER_BLOCK, D_MODEL))
    return y.reshape(-1, D_MODEL)[:T].reshape(shp)


def finish_layer(x, mla_o, ml_h, o_gate, mem_k, mem_v, lw):
    B, S = x.shape[:2]
    ml_o = o_gate * ml_h.transpose(0, 2, 1, 3).reshape(B, S, ML_WIDTH).astype(x.dtype)
    mix = jnp.einsum('bse,ed->bsd', jnp.concatenate([mla_o, ml_o], -1), lw['w_out'])
    x = layer_norm(ALPHA * x + mix, lw['ln1_g'], lw['ln1_b'])
    x = layer_norm(ALPHA * x + mem_attend(x, mem_k, mem_v, lw['w_mq'], lw['w_mo']), lw['ln2_g'], lw['ln2_b'])
    y = peer(x, lw['w_pq'], lw['sub_k1'], lw['sub_k2'], lw['peer_u'], lw['peer_v'])
    return layer_norm(ALPHA * x + y, lw['ln3_g'], lw['ln3_b'])


def kernel(x_prompt, x_sample, cache_kv_latent, cache_k_rope, state_C, state_n, state_m,
           cache_mem_k, cache_mem_v, page_table, mem_prompt, ln0_g, ln0_b, w_in, b_i, b_f,
           g_q, w_uq, g_kv, w_uk, w_uv, w_out, ln1_g, ln1_b, w_mq, w_mk, w_mv, w_mo,
           ln2_g, ln2_b, w_pq, sub_k1, sub_k2, peer_u, peer_v, ln3_g, ln3_b):
    S = x_prompt.shape[1]
    T = x_sample.shape[1]
    past = page_table.shape[1] * PAGE_SIZE
    pos_p = jnp.arange(S)
    pos_s = past + jnp.arange(T)
    xp = pallas_layer_norm(x_prompt, ln0_g, ln0_b)
    xs = pallas_layer_norm(x_sample, ln0_g, ln0_b)
    l = 0
    lw = {'w_in': w_in[l], 'b_i': b_i[l], 'b_f': b_f[l], 'g_q': g_q[l], 'w_uq': w_uq[l],
          'g_kv': g_kv[l], 'w_uk': w_uk[l], 'w_out': w_out[l], 'ln1_g': ln1_g[l], 'ln1_b': ln1_b[l],
          'w_mq': w_mq[l], 'w_mo': w_mo[l], 'ln2_g': ln2_g[l], 'ln2_b': ln2_b[l], 'w_pq': w_pq[l],
          'sub_k1': sub_k1[l], 'sub_k2': sub_k2[l], 'peer_u': peer_u[l], 'peer_v': peer_v[l],
          'ln3_g': ln3_g[l], 'ln3_b': ln3_b[l]}
    (ql, qr, kvl, kr), ml_in, og = mixer_projections(xp, pos_p, lw)
    mla_o = mla_attend_prompt(ql, qr, kvl, kr, w_uv[l])
    ml_h, (C, n, m) = mlstm_prompt(*ml_in)
    mk, mv = mem_kv(mem_prompt, w_mk[l], w_mv[l])
    xp = finish_layer(xp, mla_o, ml_h, og, mk, mv, lw)
    kvl_p, kr_p, C_p, n_p, m_p = kvl, kr, C, n, m
    (ql, qr, kvl, kr), ml_in, og = mixer_projections(xs, pos_s, lw)
    mla_o = mla_attend_sample(ql, qr, kvl, kr, cache_kv_latent[l], cache_k_rope[l], page_table, w_uv[l])
    carry = (state_C[l], state_n[l], state_m[l])
    (C, n, m), ml_h = mlstm_chunk(carry, ml_in)
    xs = finish_layer(xs, mla_o, ml_h, og, cache_mem_k[l], cache_mem_v[l], lw)
    st = lambda t: t[None]
    return (xp, xs, st(kvl_p), st(kr_p), st(C_p), st(n_p), st(m_p), st(mk), st(mv),
            st(kvl), st(kr), st(C), st(n), st(m))
```

```python
import functools

import jax, jax.numpy as jnp
from jax import lax
import numpy as np
from jax.experimental import pallas as pl
from jax.experimental.pallas import tpu as pltpu

D_MODEL = 1024
BATCH = 8
SEQ = 2048
DEPTH = 1
DEC_BATCH = 128
DEC_SEQ = 4
PAST_LEN = 8192
PAGE_SIZE = 128

MLA_HEADS = 8
MLA_NOPE = 64
MLA_ROPE = 32
MLA_V = 64
MLA_KV_LORA = 256
MLA_Q_LORA = 384
MLA_SCALE = (MLA_NOPE + MLA_ROPE) ** -0.5
ROPE_BASE = 10000.0
ML_HEADS = 4
ML_DH = 128
ML_CHUNK = 64
MLA_WIDTH = MLA_HEADS * MLA_V
ML_WIDTH = ML_HEADS * ML_DH
D_MIX = MLA_WIDTH + ML_WIDTH
N_MEM = 256
MEM_HEADS = 4
MEM_HD = D_MODEL // MEM_HEADS
PEER_HEADS = 8
PEER_NKEYS = 128
PEER_N = PEER_NKEYS * PEER_NKEYS
PEER_DKEY = 128
PEER_TOPK = 16
PEER_BLOCK = 256
ATTN_BLOCK = 128
LN_EPS = 1e-5
RMS_EPS = 1e-6
ALPHA = (2 * DEPTH) ** 0.25
BETA = (8 * DEPTH) ** -0.25
IN_SIZES = (MLA_Q_LORA, MLA_KV_LORA, MLA_ROPE, ML_WIDTH, ML_WIDTH, ML_WIDTH, ML_HEADS, ML_HEADS, ML_WIDTH)
IN_TOTAL = sum(IN_SIZES)
IN_SPLITS = [int(s) for s in np.cumsum(IN_SIZES)[:-1]]


def _ln_kernel(x_ref, g_ref, b_ref, o_ref):
    x = x_ref[...]
    mu = jnp.mean(x, axis=-1, keepdims=True)
    xc = x - mu
    var = jnp.mean(xc * xc, axis=-1, keepdims=True)
    o_ref[...] = xc * lax.rsqrt(var + LN_EPS) * g_ref[...] + b_ref[...]


def pallas_layer_norm(x, g, b, rows=512):
    shp = x.shape
    x2 = x.reshape(-1, shp[-1])
    n = x2.shape[0]
    out = pl.pallas_call(
        _ln_kernel,
        grid=(n // rows,),
        in_specs=[pl.BlockSpec((rows, shp[-1]), lambda i: (i, 0)),
                  pl.BlockSpec((1, shp[-1]), lambda i: (0, 0)),
                  pl.BlockSpec((1, shp[-1]), lambda i: (0, 0))],
        out_specs=pl.BlockSpec((rows, shp[-1]), lambda i: (i, 0)),
        out_shape=jax.ShapeDtypeStruct(x2.shape, x2.dtype),
        name="ln0",
    )(x2, g.reshape(1, -1), b.reshape(1, -1))
    return out.reshape(shp)


def layer_norm(x, g, b):
    xf = x.astype(jnp.float32)
    mu = xf.mean(-1, keepdims=True)
    var = jnp.mean(jnp.square(xf - mu), -1, keepdims=True)
    return ((xf - mu) * lax.rsqrt(var + LN_EPS) * g + b).astype(x.dtype)


def rms_norm(x, g):
    xf = x.astype(jnp.float32)
    return (xf * lax.rsqrt(jnp.mean(xf * xf, -1, keepdims=True) + RMS_EPS) * g).astype(x.dtype)


def rope_angles(pos):
    inv = 1.0 / (ROPE_BASE ** (jnp.arange(0, MLA_ROPE, 2, dtype=jnp.float32) / MLA_ROPE))
    ang = pos.astype(jnp.float32)[:, None] * inv[None, :]
    return jnp.cos(ang), jnp.sin(ang)


def apply_rope(x, cos, sin):
    half = MLA_ROPE // 2
    x1, x2 = x[..., :half], x[..., half:]
    c = cos.astype(x.dtype)
    s = sin.astype(x.dtype)
    return jnp.concatenate([x1 * c - x2 * s, x1 * s + x2 * c], -1)


def mixer_projections(x, pos, lw):
    z = jnp.einsum('bsd,de->bse', x, lw['w_in'])
    c_q, c_kv, k_r, mq, mk, mv, i_pre, f_pre, o_pre = jnp.split(z, IN_SPLITS, axis=-1)
    cos, sin = rope_angles(pos)
    q = jnp.einsum('bsc,chd->bshd', rms_norm(c_q, lw['g_q']), lw['w_uq'])
    q_rope = apply_rope(q[..., MLA_NOPE:], cos[:, None], sin[:, None])
    q_lat = jnp.einsum('bshn,chn->bshc', q[..., :MLA_NOPE], lw['w_uk'])
    kv_lat = rms_norm(c_kv, lw['g_kv'])
    k_rope = apply_rope(k_r, cos, sin)
    B, S = x.shape[:2]
    heads = lambda t: t.reshape(B, S, ML_HEADS, ML_DH).transpose(0, 2, 1, 3).astype(jnp.float32)
    ml_q = heads(mq)
    ml_k = heads(mk) * (ML_DH ** -0.5)
    ml_v = heads(mv)
    ig = (i_pre + lw['b_i']).astype(jnp.float32).transpose(0, 2, 1)
    lf = jax.nn.log_sigmoid((f_pre + lw['b_f']).astype(jnp.float32)).transpose(0, 2, 1)
    o_gate = jax.nn.sigmoid(o_pre)
    return (q_lat, q_rope, kv_lat, k_rope), (ml_q, ml_k, ml_v, ig, lf), o_gate


def mla_attend_prompt(q_lat, q_rope, kv_lat, k_rope, w_uv):
    B, S = q_lat.shape[:2]
    nb = S // ATTN_BLOCK
    qlb = q_lat.reshape(B, nb, ATTN_BLOCK, MLA_HEADS, MLA_KV_LORA).swapaxes(0, 1)
    qrb = q_rope.reshape(B, nb, ATTN_BLOCK, MLA_HEADS, MLA_ROPE).swapaxes(0, 1)
    kpos = jnp.arange(S)

    def block(args):
        ql, qr, start = args
        s = (jnp.einsum('bqhc,bkc->bhqk', ql, kv_lat) + jnp.einsum('bqhr,bkr->bhqk', qr, k_rope)).astype(jnp.float32) * MLA_SCALE
        qpos = start + jnp.arange(ATTN_BLOCK)
        s = jnp.where(kpos[None, :] <= qpos[:, None], s, -jnp.inf)
        p = jax.nn.softmax(s, axis=-1).astype(kv_lat.dtype)
        return jnp.einsum('bhqk,bkc->bqhc', p, kv_lat)

    o = lax.map(block, (qlb, qrb, jnp.arange(nb) * ATTN_BLOCK))
    o = o.swapaxes(0, 1).reshape(B, S, MLA_HEADS, MLA_KV_LORA)
    return jnp.einsum('bshc,chv->bshv', o, w_uv).reshape(B, S, MLA_WIDTH)


def mla_attend_sample(q_lat, q_rope, kv_lat, k_rope, pool_lat, pool_rope, page_table, w_uv):
    B, T = q_lat.shape[:2]
    past_lat = pool_lat[page_table].reshape(B, -1, MLA_KV_LORA)
    past_rope = pool_rope[page_table].reshape(B, -1, MLA_ROPE)
    s_past = jnp.einsum('bqhc,bkc->bhqk', q_lat, past_lat) + jnp.einsum('bqhr,bkr->bhqk', q_rope, past_rope)
    s_new = jnp.einsum('bqhc,bkc->bhqk', q_lat, kv_lat) + jnp.einsum('bqhr,bkr->bhqk', q_rope, k_rope)
    causal = jnp.arange(T)[None, :] <= jnp.arange(T)[:, None]
    s_new = jnp.where(causal, s_new.astype(jnp.float32), -jnp.inf)
    s = jnp.concatenate([s_past.astype(jnp.float32), s_new], -1) * MLA_SCALE
    p = jax.nn.softmax(s, axis=-1).astype(kv_lat.dtype)
    P = past_lat.shape[1]
    o = jnp.einsum('bhqk,bkc->bqhc', p[..., :P], past_lat) + jnp.einsum('bhqk,bkc->bqhc', p[..., P:], kv_lat)
    return jnp.einsum('bqhc,chv->bqhv', o, w_uv).reshape(B, T, MLA_WIDTH)


def mlstm_chunk(carry, inp):
    C, n, m = carry
    q, k, v, ig, lf = inp
    L = q.shape[2]
    b = jnp.cumsum(lf, axis=-1)
    causal = jnp.arange(L)[None, :] <= jnp.arange(L)[:, None]
    D = jnp.where(causal, b[..., :, None] - b[..., None, :] + ig[..., None, :], -jnp.inf)
    inter = b + m[..., None]
    m_t = jnp.maximum(inter, D.max(-1))
    A = jnp.exp(D - m_t[..., None]) * jnp.einsum('bhtd,bhsd->bhts', q, k)
    w_inter = jnp.exp(inter - m_t)
    num = w_inter[..., None] * jnp.einsum('bhtd,bhde->bhte', q, C) + jnp.einsum('bhts,bhse->bhte', A, v)
    den = w_inter * jnp.einsum('bhtd,bhd->bht', q, n) + A.sum(-1)
    h = num / jnp.maximum(jnp.abs(den), jnp.exp(-m_t))[..., None]
    b_end = b[..., -1]
    dec = b_end[..., None] - b + ig
    m_new = jnp.maximum(b_end + m, dec.max(-1))
    a_prev = jnp.exp(b_end + m - m_new)
    w_row = jnp.exp(dec - m_new[..., None])
    C_new = a_prev[..., None, None] * C + jnp.einsum('bhs,bhsd,bhse->bhde', w_row, k, v)
    n_new = a_prev[..., None] * n + jnp.einsum('bhs,bhsd->bhd', w_row, k)
    return (C_new, n_new, m_new), h


def mlstm_prompt(q, k, v, ig, lf):
    B, H, S, d = q.shape
    nc = S // ML_CHUNK
    ch = lambda t: jnp.moveaxis(t.reshape((B, H, nc, ML_CHUNK) + t.shape[3:]), 2, 0)
    init = (jnp.zeros((B, H, d, d), jnp.float32), jnp.zeros((B, H, d), jnp.float32), jnp.zeros((B, H), jnp.float32))
    state, h = lax.scan(mlstm_chunk, init, (ch(q), ch(k), ch(v), ch(ig), ch(lf)))
    h = jnp.moveaxis(h, 0, 2).reshape(B, H, S, d)
    return h, state


def mem_kv(mem, w_mk, w_mv):
    return jnp.einsum('bmd,dhe->bmhe', mem, w_mk), jnp.einsum('bmd,dhe->bmhe', mem, w_mv)


def mem_attend(x, mem_k, mem_v, w_mq, w_mo):
    q = jnp.einsum('bsd,dhe->bshe', x, w_mq)
    s = jnp.einsum('bshe,bmhe->bhsm', q, mem_k).astype(jnp.float32) * (MEM_HD ** -0.5)
    p = jax.nn.softmax(s, axis=-1).astype(x.dtype)
    o = jnp.einsum('bhsm,bmhe->bshe', p, mem_v)
    return jnp.einsum('bshe,hed->bsd', o, w_mo)


PEER_RT = 256
PEER_TB = 512
PEER_EB = 1024
NEG_INF = float('-inf')


def _top16_rows(s, row_id):
    n_rows = float(s.shape[0])
    out_id = lax.broadcasted_iota(jnp.int32, (PEER_TOPK, s.shape[1]), 0)
    stacked = jnp.zeros((PEER_TOPK, s.shape[1]), jnp.float32)
    rows = []
    for k in range(PEER_TOPK):
        m = jnp.max(s, axis=0, keepdims=True)
        first = jnp.min(jnp.where(s == m, row_id, n_rows), axis=0, keepdims=True)
        s = jnp.where(row_id == first, NEG_INF, s)
        rows.append(m)
        stacked = jnp.where(out_id == k, m, stacked)
    return rows, stacked


def _peer_route_kernel(x_ref, wq_ref, k1_ref, k2_ref, s1_ref, s2_ref, p1_ref, p2_ref, thr_ref):
    half = PEER_DKEY // 2
    tb = x_ref.shape[0]
    xb = x_ref[...].astype(jnp.bfloat16)
    qt = lax.dot_general(wq_ref[...], xb, (((1,), (1,)), ((), ())), preferred_element_type=jnp.float32)
    row128 = lax.broadcasted_iota(jnp.int32, (PEER_NKEYS, tb), 0).astype(jnp.float32)
    sub8 = lax.broadcasted_iota(jnp.int32, (8, tb), 0)
    for h in range(PEER_HEADS):
        q1 = qt[h * PEER_DKEY:h * PEER_DKEY + half].astype(jnp.bfloat16)
        q2 = qt[h * PEER_DKEY + half:(h + 1) * PEER_DKEY].astype(jnp.bfloat16)
        s1 = jnp.dot(k1_ref[...], q1, preferred_element_type=jnp.float32)
        s2 = jnp.dot(k2_ref[...], q2, preferred_element_type=jnp.float32)
        r1, v1 = _top16_rows(s1, row128)
        r2, v2 = _top16_rows(s2, row128)
        groups = []
        for b in range(8):
            lim = PEER_TOPK // (b + 1)
            for a0 in range(0, lim, 8):
                g = v1[a0:a0 + 8] + r2[b]
                if lim - a0 < 8:
                    g = jnp.where(sub8 < lim - a0, g, NEG_INF)
                groups.append(g)
        groups.append(r1[0] + v2[8:16])
        cand = jnp.concatenate(groups, axis=0)
        cand_id = lax.broadcasted_iota(jnp.int32, cand.shape, 0).astype(jnp.float32)
        vals, _ = _top16_rows(cand, cand_id)
        z = jnp.ones_like(vals[0])
        for k in range(1, PEER_TOPK):
            z = z + jnp.exp(vals[k] - vals[0])
        s1_ref[h] = s1
        s2_ref[h] = s2
        p1_ref[h] = jnp.exp(s1 - r1[0]) / z
        p2_ref[h] = jnp.exp(s2 - r2[0])
        thr_ref[h:h + 1, :] = vals[PEER_TOPK - 1]


def _peer_dense_kernel(x_ref, s1_ref, s2_ref, p1_ref, p2_ref, thr_ref, u_ref, vt_ref, g_ref, b_ref,
                       o_ref, xb_scr, ht_scr, at_scr, yt_scr):
    k = pl.program_id(1)
    n_i1 = PEER_EB // PEER_NKEYS

    @pl.when(k == 0)
    def _():
        xb_scr[...] = x_ref[...].astype(jnp.bfloat16)
        yt_scr[...] = jnp.zeros_like(yt_scr)

    ht = lax.dot_general(u_ref[...], xb_scr[...], (((1,), (1,)), ((), ())),
                         preferred_element_type=jnp.float32)

    pieces = []
    for c in range(n_i1):
        i1 = k * n_i1 + c
        w = jnp.zeros((PEER_NKEYS, x_ref.shape[0]), jnp.float32)
        for h in range(PEER_HEADS):
            s = s1_ref[h, pl.ds(i1, 1), :] + s2_ref[h]
            pw = p1_ref[h, pl.ds(i1, 1), :] * p2_ref[h]
            w = w + jnp.where(s >= thr_ref[h:h + 1, :], pw, 0.0)
        hc = ht[c * PEER_NKEYS:(c + 1) * PEER_NKEYS]
        gelu = 0.5 * hc * (1.0 + lax.erf(hc * (2.0 ** -0.5)))
        pieces.append((w * gelu).astype(jnp.bfloat16))
    at = jnp.concatenate(pieces, axis=0)
    yt_scr[...] += jnp.dot(vt_ref[...], at, preferred_element_type=jnp.float32)

    @pl.when(k == pl.num_programs(1) - 1)
    def _():
        z = ALPHA * x_ref[...] + yt_scr[...].T
        mu = jnp.mean(z, axis=-1, keepdims=True)
        zc = z - mu
        var = jnp.mean(zc * zc, axis=-1, keepdims=True)
        o_ref[...] = zc * lax.rsqrt(var + LN_EPS) * g_ref[...] + b_ref[...]


def peer_ln(x, w_pq, sub_k1, sub_k2, peer_u, peer_v, ln_g, ln_b):
    T = x.shape[0]
    assert T % PEER_TB == 0 and T % PEER_RT == 0
    nt = T // PEER_TB
    half = PEER_DKEY // 2
    wq_t = w_pq.reshape(D_MODEL, PEER_HEADS * PEER_DKEY).T.astype(jnp.bfloat16)
    sshape = jax.ShapeDtypeStruct((PEER_HEADS, PEER_NKEYS, T), jnp.float32)
    sspec = pl.BlockSpec((PEER_HEADS, PEER_NKEYS, PEER_RT), lambda j: (0, 0, j))
    s1, s2, p1, p2, thr = pl.pallas_call(
        _peer_route_kernel,
        grid=(T // PEER_RT,),
        in_specs=[pl.BlockSpec((PEER_RT, D_MODEL), lambda j: (j, 0)),
                  pl.BlockSpec((PEER_HEADS * PEER_DKEY, D_MODEL), lambda j: (0, 0)),
                  pl.BlockSpec((PEER_NKEYS, half), lambda j: (0, 0)),
                  pl.BlockSpec((PEER_NKEYS, half), lambda j: (0, 0))],
        out_specs=[sspec, sspec, sspec, sspec, pl.BlockSpec((PEER_HEADS, PEER_RT), lambda j: (0, j))],
        out_shape=[sshape, sshape, sshape, sshape, jax.ShapeDtypeStruct((PEER_HEADS, T), jnp.float32)],
        compiler_params=pltpu.CompilerParams(dimension_semantics=("arbitrary",),
                                             vmem_limit_bytes=48 * 1024 * 1024),
        name="peer_route",
    )(x, wq_t, sub_k1.astype(jnp.bfloat16), sub_k2.astype(jnp.bfloat16))

    u_b = peer_u.astype(jnp.bfloat16)
    vt_b = peer_v.T.astype(jnp.bfloat16)
    ne = PEER_N // PEER_EB
    sspec2 = pl.BlockSpec((PEER_HEADS, PEER_NKEYS, PEER_TB), lambda j, k: (0, 0, j))
    return pl.pallas_call(
        _peer_dense_kernel,
        grid=(nt, ne),
        in_specs=[pl.BlockSpec((PEER_TB, D_MODEL), lambda j, k: (j, 0)),
                  sspec2, sspec2, sspec2, sspec2,
                  pl.BlockSpec((PEER_HEADS, PEER_TB), lambda j, k: (0, j)),
                  pl.BlockSpec((PEER_EB, D_MODEL), lambda j, k: (k, 0)),
                  pl.BlockSpec((D_MODEL, PEER_EB), lambda j, k: (0, k)),
                  pl.BlockSpec((1, D_MODEL), lambda j, k: (0, 0)),
                  pl.BlockSpec((1, D_MODEL), lambda j, k: (0, 0))],
        out_specs=pl.BlockSpec((PEER_TB, D_MODEL), lambda j, k: (j, 0)),
        out_shape=jax.ShapeDtypeStruct((T, D_MODEL), jnp.float32),
        scratch_shapes=[pltpu.VMEM((PEER_TB, D_MODEL), jnp.bfloat16),
                        pltpu.VMEM((PEER_EB, PEER_TB), jnp.float32),
                        pltpu.VMEM((PEER_EB, PEER_TB), jnp.bfloat16),
                        pltpu.VMEM((D_MODEL, PEER_TB), jnp.float32)],
        compiler_params=pltpu.CompilerParams(dimension_semantics=("arbitrary", "arbitrary"),
                                             vmem_limit_bytes=48 * 1024 * 1024),
        name="peer_dense",
    )(x, s1, s2, p1, p2, thr, u_b, vt_b, ln_g.reshape(1, -1), ln_b.reshape(1, -1))


def pre_peer(x, mla_o, ml_h, o_gate, mem_k, mem_v, lw):
    B, S = x.shape[:2]
    ml_o = o_gate * ml_h.transpose(0, 2, 1, 3).reshape(B, S, ML_WIDTH).astype(x.dtype)
    mix = jnp.einsum('bse,ed->bsd', jnp.concatenate([mla_o, ml_o], -1), lw['w_out'])
    x = layer_norm(ALPHA * x + mix, lw['ln1_g'], lw['ln1_b'])
    return layer_norm(ALPHA * x + mem_attend(x, mem_k, mem_v, lw['w_mq'], lw['w_mo']), lw['ln2_g'], lw['ln2_b'])


def kernel(x_prompt, x_sample, cache_kv_latent, cache_k_rope, state_C, state_n, state_m,
           cache_mem_k, cache_mem_v, page_table, mem_prompt, ln0_g, ln0_b, w_in, b_i, b_f,
           g_q, w_uq, g_kv, w_uk, w_uv, w_out, ln1_g, ln1_b, w_mq, w_mk, w_mv, w_mo,
           ln2_g, ln2_b, w_pq, sub_k1, sub_k2, peer_u, peer_v, ln3_g, ln3_b):
    S = x_prompt.shape[1]
    T = x_sample.shape[1]
    past = page_table.shape[1] * PAGE_SIZE
    pos_p = jnp.arange(S)
    pos_s = past + jnp.arange(T)
    xp = pallas_layer_norm(x_prompt, ln0_g, ln0_b)
    xs = pallas_layer_norm(x_sample, ln0_g, ln0_b)
    l = 0
    lw = {'w_in': w_in[l], 'b_i': b_i[l], 'b_f': b_f[l], 'g_q': g_q[l], 'w_uq': w_uq[l],
          'g_kv': g_kv[l], 'w_uk': w_uk[l], 'w_out': w_out[l], 'ln1_g': ln1_g[l], 'ln1_b': ln1_b[l],
          'w_mq': w_mq[l], 'w_mo': w_mo[l], 'ln2_g': ln2_g[l], 'ln2_b': ln2_b[l], 'w_pq': w_pq[l],
          'sub_k1': sub_k1[l], 'sub_k2': sub_k2[l], 'peer_u': peer_u[l], 'peer_v': peer_v[l],
          'ln3_g': ln3_g[l], 'ln3_b': ln3_b[l]}
    (ql, qr, kvl, kr), ml_in, og = mixer_projections(xp, pos_p, lw)
    mla_o = mla_attend_prompt(ql, qr, kvl, kr, w_uv[l])
    ml_h, (C, n, m) = mlstm_prompt(*ml_in)
    mk, mv = mem_kv(mem_prompt, w_mk[l], w_mv[l])
    xp2 = pre_peer(xp, mla_o, ml_h, og, mk, mv, lw)
    kvl_p, kr_p, C_p, n_p, m_p = kvl, kr, C, n, m
    (ql, qr, kvl, kr), ml_in, og = mixer_projections(xs, pos_s, lw)
    mla_o = mla_attend_sample(ql, qr, kvl, kr, cache_kv_latent[l], cache_k_rope[l], page_table, w_uv[l])
    carry = (state_C[l], state_n[l], state_m[l])
    (C, n, m), ml_h = mlstm_chunk(carry, ml_in)
    xs2 = pre_peer(xs, mla_o, ml_h, og, cache_mem_k[l], cache_mem_v[l], lw)
    n_p_tok = xp2.shape[0] * xp2.shape[1]
    x2 = jnp.concatenate([xp2.reshape(-1, D_MODEL), xs2.reshape(-1, D_MODEL)], axis=0)
    x3 = peer_ln(x2, lw['w_pq'], lw['sub_k1'], lw['sub_k2'], lw['peer_u'], lw['peer_v'], lw['ln3_g'], lw['ln3_b'])
    xp = x3[:n_p_tok].reshape(xp2.shape)
    xs = x3[n_p_tok:].reshape(xs2.shape)
    st = lambda t: t[None]
    return (xp, xs, st(kvl_p), st(kr_p), st(C_p), st(n_p), st(m_p), st(mk), st(mv),
            st(kvl), st(kr), st(C), st(n), st(m))
```

```python
import functools

import jax, jax.numpy as jnp
from jax import lax
import numpy as np
from jax.experimental import pallas as pl
from jax.experimental.pallas import tpu as pltpu

D_MODEL = 1024
BATCH = 8
SEQ = 2048
DEPTH = 1
DEC_BATCH = 128
DEC_SEQ = 4
PAST_LEN = 8192
PAGE_SIZE = 128

MLA_HEADS = 8
MLA_NOPE = 64
MLA_ROPE = 32
MLA_V = 64
MLA_KV_LORA = 256
MLA_Q_LORA = 384
MLA_SCALE = (MLA_NOPE + MLA_ROPE) ** -0.5
ROPE_BASE = 10000.0
ML_HEADS = 4
ML_DH = 128
ML_CHUNK = 64
MLA_WIDTH = MLA_HEADS * MLA_V
ML_WIDTH = ML_HEADS * ML_DH
D_MIX = MLA_WIDTH + ML_WIDTH
N_MEM = 256
MEM_HEADS = 4
MEM_HD = D_MODEL // MEM_HEADS
PEER_HEADS = 8
PEER_NKEYS = 128
PEER_N = PEER_NKEYS * PEER_NKEYS
PEER_DKEY = 128
PEER_TOPK = 16
PEER_BLOCK = 256
ATTN_BLOCK = 128
LN_EPS = 1e-5
RMS_EPS = 1e-6
ALPHA = (2 * DEPTH) ** 0.25
BETA = (8 * DEPTH) ** -0.25
IN_SIZES = (MLA_Q_LORA, MLA_KV_LORA, MLA_ROPE, ML_WIDTH, ML_WIDTH, ML_WIDTH, ML_HEADS, ML_HEADS, ML_WIDTH)
IN_TOTAL = sum(IN_SIZES)
IN_SPLITS = [int(s) for s in np.cumsum(IN_SIZES)[:-1]]


def _ln_kernel(x_ref, g_ref, b_ref, o_ref):
    x = x_ref[...]
    mu = jnp.mean(x, axis=-1, keepdims=True)
    xc = x - mu
    var = jnp.mean(xc * xc, axis=-1, keepdims=True)
    o_ref[...] = xc * lax.rsqrt(var + LN_EPS) * g_ref[...] + b_ref[...]


def pallas_layer_norm(x, g, b, rows=512):
    shp = x.shape
    x2 = x.reshape(-1, shp[-1])
    n = x2.shape[0]
    out = pl.pallas_call(
        _ln_kernel,
        grid=(n // rows,),
        in_specs=[pl.BlockSpec((rows, shp[-1]), lambda i: (i, 0)),
                  pl.BlockSpec((1, shp[-1]), lambda i: (0, 0)),
                  pl.BlockSpec((1, shp[-1]), lambda i: (0, 0))],
        out_specs=pl.BlockSpec((rows, shp[-1]), lambda i: (i, 0)),
        out_shape=jax.ShapeDtypeStruct(x2.shape, x2.dtype),
        name="ln0",
    )(x2, g.reshape(1, -1), b.reshape(1, -1))
    return out.reshape(shp)


def layer_norm(x, g, b):
    xf = x.astype(jnp.float32)
    mu = xf.mean(-1, keepdims=True)
    var = jnp.mean(jnp.square(xf - mu), -1, keepdims=True)
    return ((xf - mu) * lax.rsqrt(var + LN_EPS) * g + b).astype(x.dtype)


def rms_norm(x, g):
    xf = x.astype(jnp.float32)
    return (xf * lax.rsqrt(jnp.mean(xf * xf, -1, keepdims=True) + RMS_EPS) * g).astype(x.dtype)


def rope_angles(pos):
    inv = 1.0 / (ROPE_BASE ** (jnp.arange(0, MLA_ROPE, 2, dtype=jnp.float32) / MLA_ROPE))
    ang = pos.astype(jnp.float32)[:, None] * inv[None, :]
    return jnp.cos(ang), jnp.sin(ang)


def apply_rope(x, cos, sin):
    half = MLA_ROPE // 2
    x1, x2 = x[..., :half], x[..., half:]
    c = cos.astype(x.dtype)
    s = sin.astype(x.dtype)
    return jnp.concatenate([x1 * c - x2 * s, x1 * s + x2 * c], -1)


def mixer_projections(x, pos, lw):
    z = jnp.einsum('bsd,de->bse', x, lw['w_in'])
    c_q, c_kv, k_r, mq, mk, mv, i_pre, f_pre, o_pre = jnp.split(z, IN_SPLITS, axis=-1)
    cos, sin = rope_angles(pos)
    q = jnp.einsum('bsc,chd->bshd', rms_norm(c_q, lw['g_q']), lw['w_uq'])
    q_rope = apply_rope(q[..., MLA_NOPE:], cos[:, None], sin[:, None])
    q_lat = jnp.einsum('bshn,chn->bshc', q[..., :MLA_NOPE], lw['w_uk'])
    kv_lat = rms_norm(c_kv, lw['g_kv'])
    k_rope = apply_rope(k_r, cos, sin)
    B, S = x.shape[:2]
    heads = lambda t: t.reshape(B, S, ML_HEADS, ML_DH).transpose(0, 2, 1, 3).astype(jnp.float32)
    ml_q = heads(mq)
    ml_k = heads(mk) * (ML_DH ** -0.5)
    ml_v = heads(mv)
    ig = (i_pre + lw['b_i']).astype(jnp.float32).transpose(0, 2, 1)
    lf = jax.nn.log_sigmoid((f_pre + lw['b_f']).astype(jnp.float32)).transpose(0, 2, 1)
    o_gate = jax.nn.sigmoid(o_pre)
    return (q_lat, q_rope, kv_lat, k_rope), (ml_q, ml_k, ml_v, ig, lf), o_gate


ATT_BQ = 128
ATT_BK = 512


def _mla_prompt_kernel(ql_ref, qr_ref, kv_ref, kr_ref, wuv_ref, o_ref, acc_scr, m_scr, l_scr):
    qi = pl.program_id(1)
    rows = MLA_HEADS * ATT_BQ
    ql = ql_ref[0].reshape(rows, MLA_KV_LORA)
    qr = qr_ref[0].reshape(rows, MLA_ROPE)
    acc_scr[...] = jnp.zeros_like(acc_scr)
    m_scr[...] = jnp.full_like(m_scr, NEG_INF)
    l_scr[...] = jnp.zeros_like(l_scr)
    nt = (((1,), (1,)), ((), ()))

    def step(kj, masked):
        k0 = pl.multiple_of(kj * ATT_BK, ATT_BK)
        kvb = kv_ref[0, pl.ds(k0, ATT_BK), :]
        krb = kr_ref[0, pl.ds(k0, ATT_BK), :]
        s = (lax.dot_general(ql, kvb, nt, preferred_element_type=jnp.float32)
             + lax.dot_general(qr, krb, nt, preferred_element_type=jnp.float32)) * MLA_SCALE
        if masked:
            col = k0 + lax.broadcasted_iota(jnp.int32, s.shape, 1)
            tok = qi * ATT_BQ + (lax.broadcasted_iota(jnp.int32, s.shape, 0) & (ATT_BQ - 1))
            s = jnp.where(col <= tok, s, NEG_INF)
        m_old = m_scr[...]
        m_new = jnp.maximum(m_old, jnp.max(s, axis=1, keepdims=True))
        alpha = jnp.exp(m_old - m_new)
        p = jnp.exp(s - m_new)
        l_scr[...] = alpha * l_scr[...] + jnp.sum(p, axis=1, keepdims=True)
        acc_scr[...] = alpha * acc_scr[...] + jnp.dot(p.astype(jnp.bfloat16), kvb,
                                                      preferred_element_type=jnp.float32)
        m_scr[...] = m_new

    n_full = (qi * ATT_BQ) // ATT_BK

    def body(kj, c):
        step(kj, False)
        return c

    lax.fori_loop(0, n_full, body, 0)
    step(n_full, True)
    o = (acc_scr[...] / l_scr[...]).astype(jnp.bfloat16)
    out = jnp.zeros((ATT_BQ, MLA_WIDTH), jnp.float32)
    for h in range(MLA_HEADS):
        out = out + jnp.dot(o[h * ATT_BQ:(h + 1) * ATT_BQ], wuv_ref[h], preferred_element_type=jnp.float32)
    o_ref[0] = out


def mla_attend_prompt(q_lat, q_rope, kv_lat, k_rope, w_uv):
    B, S, H, C = q_lat.shape
    assert S % ATT_BK == 0 and ATT_BK % ATT_BQ == 0
    bf = jnp.bfloat16
    wpad = jnp.zeros((H, C, H, MLA_V), jnp.float32)
    wpad = wpad.at[jnp.arange(H), :, jnp.arange(H), :].set(jnp.transpose(w_uv, (1, 0, 2)))
    wpad = wpad.reshape(H, C, H * MLA_V).astype(bf)
    rows = H * ATT_BQ
    return pl.pallas_call(
        _mla_prompt_kernel,
        grid=(B, S // ATT_BQ),
        in_specs=[pl.BlockSpec((1, H, ATT_BQ, C), lambda b, i: (b, 0, i, 0)),
                  pl.BlockSpec((1, H, ATT_BQ, MLA_ROPE), lambda b, i: (b, 0, i, 0)),
                  pl.BlockSpec((1, S, C), lambda b, i: (b, 0, 0)),
                  pl.BlockSpec((1, S, MLA_ROPE), lambda b, i: (b, 0, 0)),
                  pl.BlockSpec((H, C, H * MLA_V), lambda b, i: (0, 0, 0))],
        out_specs=pl.BlockSpec((1, ATT_BQ, H * MLA_V), lambda b, i: (b, i, 0)),
        out_shape=jax.ShapeDtypeStruct((B, S, H * MLA_V), jnp.float32),
        scratch_shapes=[pltpu.VMEM((rows, C), jnp.float32),
                        pltpu.VMEM((rows, 1), jnp.float32),
                        pltpu.VMEM((rows, 1), jnp.float32)],
        compiler_params=pltpu.CompilerParams(dimension_semantics=("arbitrary", "arbitrary"),
                                             vmem_limit_bytes=48 * 1024 * 1024),
        name="mla_prompt",
    )(q_lat.transpose(0, 2, 1, 3).astype(bf), q_rope.transpose(0, 2, 1, 3).astype(bf),
      kv_lat.astype(bf), k_rope.astype(bf), wpad)


def mla_attend_sample(q_lat, q_rope, kv_lat, k_rope, pool_lat, pool_rope, page_table, w_uv):
    B, T = q_lat.shape[:2]
    past_lat = pool_lat[page_table].reshape(B, -1, MLA_KV_LORA)
    past_rope = pool_rope[page_table].reshape(B, -1, MLA_ROPE)
    s_past = jnp.einsum('bqhc,bkc->bhqk', q_lat, past_lat) + jnp.einsum('bqhr,bkr->bhqk', q_rope, past_rope)
    s_new = jnp.einsum('bqhc,bkc->bhqk', q_lat, kv_lat) + jnp.einsum('bqhr,bkr->bhqk', q_rope, k_rope)
    causal = jnp.arange(T)[None, :] <= jnp.arange(T)[:, None]
    s_new = jnp.where(causal, s_new.astype(jnp.float32), -jnp.inf)
    s = jnp.concatenate([s_past.astype(jnp.float32), s_new], -1) * MLA_SCALE
    p = jax.nn.softmax(s, axis=-1).astype(kv_lat.dtype)
    P = past_lat.shape[1]
    o = jnp.einsum('bhqk,bkc->bqhc', p[..., :P], past_lat) + jnp.einsum('bhqk,bkc->bqhc', p[..., P:], kv_lat)
    return jnp.einsum('bqhc,chv->bqhv', o, w_uv).reshape(B, T, MLA_WIDTH)


def mlstm_chunk(carry, inp):
    C, n, m = carry
    q, k, v, ig, lf = inp
    L = q.shape[2]
    b = jnp.cumsum(lf, axis=-1)
    causal = jnp.arange(L)[None, :] <= jnp.arange(L)[:, None]
    D = jnp.where(causal, b[..., :, None] - b[..., None, :] + ig[..., None, :], -jnp.inf)
    inter = b + m[..., None]
    m_t = jnp.maximum(inter, D.max(-1))
    A = jnp.exp(D - m_t[..., None]) * jnp.einsum('bhtd,bhsd->bhts', q, k)
    w_inter = jnp.exp(inter - m_t)
    num = w_inter[..., None] * jnp.einsum('bhtd,bhde->bhte', q, C) + jnp.einsum('bhts,bhse->bhte', A, v)
    den = w_inter * jnp.einsum('bhtd,bhd->bht', q, n) + A.sum(-1)
    h = num / jnp.maximum(jnp.abs(den), jnp.exp(-m_t))[..., None]
    b_end = b[..., -1]
    dec = b_end[..., None] - b + ig
    m_new = jnp.maximum(b_end + m, dec.max(-1))
    a_prev = jnp.exp(b_end + m - m_new)
    w_row = jnp.exp(dec - m_new[..., None])
    C_new = a_prev[..., None, None] * C + jnp.einsum('bhs,bhsd,bhse->bhde', w_row, k, v)
    n_new = a_prev[..., None] * n + jnp.einsum('bhs,bhsd->bhd', w_row, k)
    return (C_new, n_new, m_new), h


def mlstm_prompt(q, k, v, ig, lf):
    B, H, S, d = q.shape
    nc = S // ML_CHUNK
    ch = lambda t: jnp.moveaxis(t.reshape((B, H, nc, ML_CHUNK) + t.shape[3:]), 2, 0)
    init = (jnp.zeros((B, H, d, d), jnp.float32), jnp.zeros((B, H, d), jnp.float32), jnp.zeros((B, H), jnp.float32))
    state, h = lax.scan(mlstm_chunk, init, (ch(q), ch(k), ch(v), ch(ig), ch(lf)))
    h = jnp.moveaxis(h, 0, 2).reshape(B, H, S, d)
    return h, state


def mem_kv(mem, w_mk, w_mv):
    return jnp.einsum('bmd,dhe->bmhe', mem, w_mk), jnp.einsum('bmd,dhe->bmhe', mem, w_mv)


def mem_attend(x, mem_k, mem_v, w_mq, w_mo):
    q = jnp.einsum('bsd,dhe->bshe', x, w_mq)
    s = jnp.einsum('bshe,bmhe->bhsm', q, mem_k).astype(jnp.float32) * (MEM_HD ** -0.5)
    p = jax.nn.softmax(s, axis=-1).astype(x.dtype)
    o = jnp.einsum('bhsm,bmhe->bshe', p, mem_v)
    return jnp.einsum('bshe,hed->bsd', o, w_mo)


PEER_RT = 256
PEER_TB = 512
PEER_EB = 1024
NEG_INF = float('-inf')


def _top16_rows(s, row_id):
    big = float(2 ** 20)
    out_id = lax.broadcasted_iota(jnp.int32, (PEER_TOPK, s.shape[1]), 0)
    stacked = jnp.zeros((PEER_TOPK, s.shape[1]), jnp.float32)
    rank = jnp.full(s.shape, float(PEER_TOPK), jnp.float32)
    rows, firsts = [], []
    for k in range(PEER_TOPK):
        m = jnp.max(s, axis=0, keepdims=True)
        first = jnp.min(jnp.where(s == m, row_id, big), axis=0, keepdims=True)
        hit = row_id == first
        s = jnp.where(hit, NEG_INF, s)
        rank = jnp.where(hit, float(k), rank)
        rows.append(m)
        firsts.append(first)
        stacked = jnp.where(out_id == k, m, stacked)
    return rows, stacked, firsts, rank, s


def _peer_route_kernel(x_ref, wq_ref, k1_ref, k2_ref, rk2_ref, lim_ref, p1_ref, p2_ref):
    half = PEER_DKEY // 2
    tb = x_ref.shape[0]
    xb = x_ref[...].astype(jnp.bfloat16)
    qt = lax.dot_general(wq_ref[...], xb, (((1,), (1,)), ((), ())), preferred_element_type=jnp.float32)
    row128 = lax.broadcasted_iota(jnp.int32, (PEER_NKEYS, tb), 0).astype(jnp.float32)
    sub8 = lax.broadcasted_iota(jnp.int32, (8, tb), 0)
    sub8f = sub8.astype(jnp.float32)
    for h in range(PEER_HEADS):
        q1 = qt[h * PEER_DKEY:h * PEER_DKEY + half].astype(jnp.bfloat16)
        q2 = qt[h * PEER_DKEY + half:(h + 1) * PEER_DKEY].astype(jnp.bfloat16)
        s1 = jnp.dot(k1_ref[...], q1, preferred_element_type=jnp.float32)
        s2 = jnp.dot(k2_ref[...], q2, preferred_element_type=jnp.float32)
        r1, v1, first1, _, _ = _top16_rows(s1, row128)
        r2, v2, _, rank2, _ = _top16_rows(s2, row128)
        groups, ids = [], []
        for b in range(8):
            lim = PEER_TOPK // (b + 1)
            for a0 in range(0, lim, 8):
                g = v1[a0:a0 + 8] + r2[b]
                if lim - a0 < 8:
                    g = jnp.where(sub8 < lim - a0, g, NEG_INF)
                groups.append(g)
                ids.append((sub8f + float(a0)) * float(PEER_TOPK) + float(b))
        groups.append(r1[0] + v2[8:16])
        ids.append(sub8f + 8.0)
        cand = jnp.concatenate(groups, axis=0)
        vals, _, _, _, left = _top16_rows(cand, jnp.concatenate(ids, axis=0))
        z = jnp.ones_like(vals[0])
        for k in range(1, PEER_TOPK):
            z = z + jnp.exp(vals[k] - vals[0])
        taken = jnp.where((left == NEG_INF) & (cand > NEG_INF), 1.0, 0.0)
        cnt_lo = jnp.zeros((8, tb), jnp.float32)
        gi = 0
        for b in range(8):
            for a0 in range(0, PEER_TOPK // (b + 1), 8):
                if a0 == 0:
                    cnt_lo = cnt_lo + taken[gi * 8:(gi + 1) * 8]
                else:
                    cnt_hi = taken[gi * 8:(gi + 1) * 8]
                gi += 1
        tail = jnp.sum(taken[gi * 8:(gi + 1) * 8], axis=0, keepdims=True)
        cnt_lo = cnt_lo + jnp.where(sub8 == 0, tail, 0.0)
        lim_full = jnp.full((PEER_NKEYS, tb), -1.0, jnp.float32)
        for a in range(PEER_TOPK):
            cnt = cnt_lo if a < 8 else cnt_hi
            lim_full = jnp.where(row128 == first1[a], cnt[a % 8:a % 8 + 1] - 1.0, lim_full)
        rk2_ref[h] = rank2.astype(jnp.bfloat16)
        lim_ref[h] = lim_full
        p1_ref[h] = jnp.exp(s1 - r1[0]) / z
        p2_ref[h] = jnp.exp(s2 - r2[0]).astype(jnp.bfloat16)


def _row_bf16(row):
    r16 = jnp.broadcast_to(row, (16, row.shape[1])).astype(jnp.bfloat16)
    return pltpu.repeat(r16, PEER_NKEYS // 16, axis=0)


def _peer_dense_kernel(x_ref, rk2_ref, lim_ref, p1_ref, p2_ref, u_ref, vt_ref, g_ref, b_ref,
                       o_ref, xb_scr, yt_scr):
    k = pl.program_id(1)
    n_i1 = PEER_EB // PEER_NKEYS

    @pl.when(k == 0)
    def _():
        xb_scr[...] = x_ref[...].astype(jnp.bfloat16)
        yt_scr[...] = jnp.zeros_like(yt_scr)

    n_sub = 2
    sub = PEER_EB // n_sub
    xb = xb_scr[...]
    hts = [lax.dot_general(u_ref[j * sub:(j + 1) * sub, :], xb, (((1,), (1,)), ((), ())),
                           preferred_element_type=jnp.float32) for j in range(n_sub)]
    acc = yt_scr[...]
    for j in range(n_sub):
        pieces = []
        for cc in range(sub // PEER_NKEYS):
            i1 = k * n_i1 + j * (sub // PEER_NKEYS) + cc
            w = jnp.zeros((PEER_NKEYS, x_ref.shape[0]), jnp.bfloat16)
            for h in range(PEER_HEADS):
                p2h = p2_ref[h]
                sel = rk2_ref[h] <= _row_bf16(lim_ref[h, pl.ds(i1, 1), :])
                w = w + _row_bf16(p1_ref[h, pl.ds(i1, 1), :]) * jnp.where(sel, p2h, jnp.zeros_like(p2h))
            hc = hts[j][cc * PEER_NKEYS:(cc + 1) * PEER_NKEYS]
            gelu = 0.5 * hc * (1.0 + lax.erf(hc * (2.0 ** -0.5)))
            pieces.append(w * gelu.astype(jnp.bfloat16))
        at = jnp.concatenate(pieces, axis=0)
        acc = acc + jnp.dot(vt_ref[:, j * sub:(j + 1) * sub], at, preferred_element_type=jnp.float32)
    yt_scr[...] = acc

    @pl.when(k == pl.num_programs(1) - 1)
    def _():
        z = ALPHA * x_ref[...] + yt_scr[...].T
        mu = jnp.mean(z, axis=-1, keepdims=True)
        zc = z - mu
        var = jnp.mean(zc * zc, axis=-1, keepdims=True)
        o_ref[...] = zc * lax.rsqrt(var + LN_EPS) * g_ref[...] + b_ref[...]


def peer_ln(x, w_pq, sub_k1, sub_k2, peer_u, peer_v, ln_g, ln_b):
    T = x.shape[0]
    assert T % PEER_TB == 0 and T % PEER_RT == 0
    nt = T // PEER_TB
    half = PEER_DKEY // 2
    wq_t = w_pq.reshape(D_MODEL, PEER_HEADS * PEER_DKEY).T.astype(jnp.bfloat16)
    sshape = jax.ShapeDtypeStruct((PEER_HEADS, PEER_NKEYS, T), jnp.float32)
    sspec = pl.BlockSpec((PEER_HEADS, PEER_NKEYS, PEER_RT), lambda j: (0, 0, j))
    rk2, lim, p1, p2 = pl.pallas_call(
        _peer_route_kernel,
        grid=(T // PEER_RT,),
        in_specs=[pl.BlockSpec((PEER_RT, D_MODEL), lambda j: (j, 0)),
                  pl.BlockSpec((PEER_HEADS * PEER_DKEY, D_MODEL), lambda j: (0, 0)),
                  pl.BlockSpec((PEER_NKEYS, half), lambda j: (0, 0)),
                  pl.BlockSpec((PEER_NKEYS, half), lambda j: (0, 0))],
        out_specs=[sspec, sspec, sspec, sspec],
        out_shape=[jax.ShapeDtypeStruct(sshape.shape, jnp.bfloat16), sshape, sshape,
                   jax.ShapeDtypeStruct(sshape.shape, jnp.bfloat16)],
        compiler_params=pltpu.CompilerParams(dimension_semantics=("arbitrary",),
                                             vmem_limit_bytes=48 * 1024 * 1024),
        name="peer_route",
    )(x, wq_t, sub_k1.astype(jnp.bfloat16), sub_k2.astype(jnp.bfloat16))

    u_b = peer_u.astype(jnp.bfloat16)
    vt_b = peer_v.T.astype(jnp.bfloat16)
    ne = PEER_N // PEER_EB
    sspec2 = pl.BlockSpec((PEER_HEADS, PEER_NKEYS, PEER_TB), lambda j, k: (0, 0, j))
    return pl.pallas_call(
        _peer_dense_kernel,
        grid=(nt, ne),
        in_specs=[pl.BlockSpec((PEER_TB, D_MODEL), lambda j, k: (j, 0)),
                  sspec2, sspec2, sspec2, sspec2,
                  pl.BlockSpec((PEER_EB, D_MODEL), lambda j, k: (k, 0)),
                  pl.BlockSpec((D_MODEL, PEER_EB), lambda j, k: (0, k)),
                  pl.BlockSpec((1, D_MODEL), lambda j, k: (0, 0)),
                  pl.BlockSpec((1, D_MODEL), lambda j, k: (0, 0))],
        out_specs=pl.BlockSpec((PEER_TB, D_MODEL), lambda j, k: (j, 0)),
        out_shape=jax.ShapeDtypeStruct((T, D_MODEL), jnp.float32),
        scratch_shapes=[pltpu.VMEM((PEER_TB, D_MODEL), jnp.bfloat16),
                        pltpu.VMEM((D_MODEL, PEER_TB), jnp.float32)],
        compiler_params=pltpu.CompilerParams(dimension_semantics=("arbitrary", "arbitrary"),
                                             vmem_limit_bytes=48 * 1024 * 1024),
        name="peer_dense",
    )(x, rk2, lim, p1, p2, u_b, vt_b, ln_g.reshape(1, -1), ln_b.reshape(1, -1))


def pre_peer(x, mla_o, ml_h, o_gate, mem_k, mem_v, lw):
    B, S = x.shape[:2]
    ml_o = o_gate * ml_h.transpose(0, 2, 1, 3).reshape(B, S, ML_WIDTH).astype(x.dtype)
    mix = jnp.einsum('bse,ed->bsd', jnp.concatenate([mla_o, ml_o], -1), lw['w_out'])
    x = layer_norm(ALPHA * x + mix, lw['ln1_g'], lw['ln1_b'])
    return layer_norm(ALPHA * x + mem_attend(x, mem_k, mem_v, lw['w_mq'], lw['w_mo']), lw['ln2_g'], lw['ln2_b'])


def kernel(x_prompt, x_sample, cache_kv_latent, cache_k_rope, state_C, state_n, state_m,
           cache_mem_k, cache_mem_v, page_table, mem_prompt, ln0_g, ln0_b, w_in, b_i, b_f,
           g_q, w_uq, g_kv, w_uk, w_uv, w_out, ln1_g, ln1_b, w_mq, w_mk, w_mv, w_mo,
           ln2_g, ln2_b, w_pq, sub_k1, sub_k2, peer_u, peer_v, ln3_g, ln3_b):
    S = x_prompt.shape[1]
    T = x_sample.shape[1]
    past = page_table.shape[1] * PAGE_SIZE
    pos_p = jnp.arange(S)
    pos_s = past + jnp.arange(T)
    xp = pallas_layer_norm(x_prompt, ln0_g, ln0_b)
    xs = pallas_layer_norm(x_sample, ln0_g, ln0_b)
    l = 0
    lw = {'w_in': w_in[l], 'b_i': b_i[l], 'b_f': b_f[l], 'g_q': g_q[l], 'w_uq': w_uq[l],
          'g_kv': g_kv[l], 'w_uk': w_uk[l], 'w_out': w_out[l], 'ln1_g': ln1_g[l], 'ln1_b': ln1_b[l],
          'w_mq': w_mq[l], 'w_mo': w_mo[l], 'ln2_g': ln2_g[l], 'ln2_b': ln2_b[l], 'w_pq': w_pq[l],
          'sub_k1': sub_k1[l], 'sub_k2': sub_k2[l], 'peer_u': peer_u[l], 'peer_v': peer_v[l],
          'ln3_g': ln3_g[l], 'ln3_b': ln3_b[l]}
    (ql, qr, kvl, kr), ml_in, og = mixer_projections(xp, pos_p, lw)
    mla_o = mla_attend_prompt(ql, qr, kvl, kr, w_uv[l])
    ml_h, (C, n, m) = mlstm_prompt(*ml_in)
    mk, mv = mem_kv(mem_prompt, w_mk[l], w_mv[l])
    xp2 = pre_peer(xp, mla_o, ml_h, og, mk, mv, lw)
    kvl_p, kr_p, C_p, n_p, m_p = kvl, kr, C, n, m
    (ql, qr, kvl, kr), ml_in, og = mixer_projections(xs, pos_s, lw)
    mla_o = mla_attend_sample(ql, qr, kvl, kr, cache_kv_latent[l], cache_k_rope[l], page_table, w_uv[l])
    carry = (state_C[l], state_n[l], state_m[l])
    (C, n, m), ml_h = mlstm_chunk(carry, ml_in)
    xs2 = pre_peer(xs, mla_o, ml_h, og, cache_mem_k[l], cache_mem_v[l], lw)
    n_p_tok = xp2.shape[0] * xp2.shape[1]
    x2 = jnp.concatenate([xp2.reshape(-1, D_MODEL), xs2.reshape(-1, D_MODEL)], axis=0)
    x3 = peer_ln(x2, lw['w_pq'], lw['sub_k1'], lw['sub_k2'], lw['peer_u'], lw['peer_v'], lw['ln3_g'], lw['ln3_b'])
    xp = x3[:n_p_tok].reshape(xp2.shape)
    xs = x3[n_p_tok:].reshape(xs2.shape)
    st = lambda t: t[None]
    return (xp, xs, st(kvl_p), st(kr_p), st(C_p), st(n_p), st(m_p), st(mk), st(mv),
            st(kvl), st(kr), st(C), st(n), st(m))
```

```python
import jax, jax.numpy as jnp
from jax import lax
import numpy as np
from jax.experimental import pallas as pl
from jax.experimental.pallas import tpu as pltpu

D_MODEL = 1024
PAGE_SIZE = 128

MLA_HEADS = 8
MLA_NOPE = 64
MLA_ROPE = 32
MLA_V = 64
MLA_KV_LORA = 256
MLA_Q_LORA = 384
MLA_SCALE = (MLA_NOPE + MLA_ROPE) ** -0.5
ROPE_BASE = 10000.0
ML_HEADS = 4
ML_DH = 128
ML_CHUNK = 64
MLA_WIDTH = MLA_HEADS * MLA_V
ML_WIDTH = ML_HEADS * ML_DH
D_MIX = MLA_WIDTH + ML_WIDTH
N_MEM = 256
MEM_HEADS = 4
MEM_HD = D_MODEL // MEM_HEADS
PEER_HEADS = 8
PEER_NKEYS = 128
PEER_N = PEER_NKEYS * PEER_NKEYS
PEER_DKEY = 128
PEER_TOPK = 16
LN_EPS = 1e-5
RMS_EPS = 1e-6
DEPTH = 1
ALPHA = (2 * DEPTH) ** 0.25
NEG_INF = float('-inf')

VMEM_LIMIT = 48 * 1024 * 1024

IN_PAD = 2944
OFF_CQ, OFF_CKV, OFF_KR, OFF_MQ, OFF_MK, OFF_MV, OFF_G, OFF_O = 0, 384, 640, 768, 1280, 1792, 2304, 2432
MIX_TM = 256


def _layer_norm_rows(z, g, b):
    mu = jnp.mean(z, axis=-1, keepdims=True)
    zc = z - mu
    var = jnp.mean(zc * zc, axis=-1, keepdims=True)
    return zc * lax.rsqrt(var + LN_EPS) * g + b


def _rope_lanes(x, cos, sin_signed):
    n = x.shape[1]
    lane = lax.broadcasted_iota(jnp.int32, x.shape, 1)
    partner = jnp.where((lane & 31) < 16, pltpu.roll(x, n - 16, axis=1), pltpu.roll(x, 16, axis=1))
    return x * cos + partner * sin_signed


def _mix_in_kernel(x_ref, g0_ref, b0_ref, win_ref, gq_ref, wuq_ref, wuk_ref, gkv_ref, cos_ref, sin_ref, gb_ref,
                   xn_ref, ql_ref, qr_ref, kv_ref, kvb_ref, kr_ref, krt_ref, mq_ref, mk_ref, mv_ref, gate_ref, og_ref):
    xn = _layer_norm_rows(x_ref[...], g0_ref[...], b0_ref[...])
    xn_ref[...] = xn
    z = jnp.dot(xn.astype(jnp.bfloat16), win_ref[...], preferred_element_type=jnp.float32)
    cos = cos_ref[...]
    sin = sin_ref[...]
    cq = z[:, OFF_CQ:OFF_CQ + MLA_Q_LORA]
    cq = cq * lax.rsqrt(jnp.mean(cq * cq, axis=-1, keepdims=True) + RMS_EPS) * gq_ref[...]
    q = jnp.dot(cq.astype(jnp.bfloat16), wuq_ref[...], preferred_element_type=jnp.float32)
    n_nope = MLA_HEADS * MLA_NOPE
    qrope = _rope_lanes(q[:, n_nope:], jnp.concatenate([cos, cos], axis=1), jnp.concatenate([sin, sin], axis=1))
    qr_ref[...] = qrope.astype(jnp.bfloat16)
    ql_ref[...] = jnp.dot(q[:, :n_nope].astype(jnp.bfloat16), wuk_ref[...],
                          preferred_element_type=jnp.float32).astype(jnp.bfloat16)
    ckv = z[:, OFF_CKV:OFF_CKV + MLA_KV_LORA]
    kv = ckv * lax.rsqrt(jnp.mean(ckv * ckv, axis=-1, keepdims=True) + RMS_EPS) * gkv_ref[...]
    kv_ref[...] = kv
    kvb_ref[...] = kv.astype(jnp.bfloat16)
    kr = _rope_lanes(z[:, OFF_KR:OFF_KR + 128], cos, sin)
    kr_ref[...] = kr
    krt = kr + pltpu.roll(kr, 32, axis=1) + pltpu.roll(kr, 64, axis=1) + pltpu.roll(kr, 96, axis=1)
    krt_ref[...] = jnp.concatenate([krt, krt], axis=1).astype(jnp.bfloat16)
    mq_ref[...] = z[:, OFF_MQ:OFF_MQ + ML_WIDTH]
    mk_ref[...] = z[:, OFF_MK:OFF_MK + ML_WIDTH] * (ML_DH ** -0.5)
    mv_ref[...] = z[:, OFF_MV:OFF_MV + ML_WIDTH]
    g = z[:, OFF_G:OFF_G + 128] + gb_ref[...]
    lane = lax.broadcasted_iota(jnp.int32, g.shape, 1)
    gate_ref[...] = jnp.where(lane < ML_HEADS, g, jax.nn.log_sigmoid(g))
    og_ref[...] = jax.nn.sigmoid(z[:, OFF_O:OFF_O + ML_WIDTH])


def mix_in(x, pos, ln0_g, ln0_b, w_in, b_i, b_f, g_q, w_uq, g_kv, w_uk):
    T = x.shape[0]
    f32, bf = jnp.float32, jnp.bfloat16
    zc = lambda n: jnp.zeros((D_MODEL, n), f32)
    win_p = jnp.concatenate([w_in[:, :672], zc(96), w_in[:, 672:2208], w_in[:, 2208:2216], zc(120), w_in[:, 2216:]],
                            axis=1).astype(bf)
    assert win_p.shape[1] == IN_PAD
    wuq_p = jnp.concatenate([w_uq[:, :, :MLA_NOPE].reshape(MLA_Q_LORA, -1),
                             w_uq[:, :, MLA_NOPE:].reshape(MLA_Q_LORA, -1)], axis=1).astype(bf)
    hh = jnp.arange(MLA_HEADS)
    wuk_blk = jnp.zeros((MLA_HEADS, MLA_NOPE, MLA_HEADS, MLA_KV_LORA), f32)
    wuk_blk = wuk_blk.at[hh, :, hh, :].set(jnp.transpose(w_uk, (1, 2, 0)))
    wuk_blk = wuk_blk.reshape(MLA_HEADS * MLA_NOPE, MLA_HEADS * MLA_KV_LORA).astype(bf)
    inv = 1.0 / (ROPE_BASE ** (jnp.arange(0, MLA_ROPE, 2, dtype=f32) / MLA_ROPE))
    ang = pos.astype(f32)[:, None] * inv[None, :]
    c, s = jnp.cos(ang), jnp.sin(ang)
    cos128 = jnp.tile(jnp.concatenate([c, c], axis=1), (1, 4))
    sin128 = jnp.tile(jnp.concatenate([-s, s], axis=1), (1, 4))
    gbias = jnp.concatenate([b_i, b_f, jnp.zeros((120,), f32)]).reshape(1, 128)
    row = lambda n: pl.BlockSpec((MIX_TM, n), lambda i: (i, 0))
    full = lambda a: pl.BlockSpec(a.shape, lambda i: (0,) * a.ndim)
    ins = [x, ln0_g.reshape(1, -1), ln0_b.reshape(1, -1), win_p, g_q.reshape(1, -1), wuq_p, wuk_blk,
           g_kv.reshape(1, -1), cos128, sin128, gbias]
    in_specs = [row(D_MODEL)] + [full(a) for a in ins[1:8]] + [row(128), row(128), full(gbias)]
    outs = [(D_MODEL, f32), (MLA_HEADS * MLA_KV_LORA, bf), (MLA_HEADS * MLA_ROPE, bf), (MLA_KV_LORA, f32),
            (MLA_KV_LORA, bf), (128, f32), (MLA_HEADS * MLA_ROPE, bf), (ML_WIDTH, f32), (ML_WIDTH, f32),
            (ML_WIDTH, f32), (128, f32), (ML_WIDTH, f32)]
    return pl.pallas_call(
        _mix_in_kernel,
        grid=(T // MIX_TM,),
        in_specs=in_specs,
        out_specs=[row(n) for n, _ in outs],
        out_shape=[jax.ShapeDtypeStruct((T, n), dt) for n, dt in outs],
        compiler_params=pltpu.CompilerParams(dimension_semantics=("arbitrary",), vmem_limit_bytes=VMEM_LIMIT),
        name="mix_in",
    )(*ins)


ATT_BQ = 128
ATT_BK = 512


def _mla_prompt_kernel(ql_ref, qr_ref, kv_ref, kr_ref, wuv_ref, o_ref, acc_scr, m_scr, l_scr):
    qi = pl.program_id(1)
    ql = jnp.concatenate([ql_ref[:, h * MLA_KV_LORA:(h + 1) * MLA_KV_LORA] for h in range(MLA_HEADS)], axis=0)
    qr_all = qr_ref[...]
    lane_head = lax.broadcasted_iota(jnp.int32, qr_all.shape, 1) // MLA_ROPE
    qr = jnp.concatenate([jnp.where(lane_head == h, qr_all, jnp.zeros_like(qr_all)) for h in range(MLA_HEADS)],
                         axis=0)
    acc_scr[...] = jnp.zeros_like(acc_scr)
    m_scr[...] = jnp.full_like(m_scr, NEG_INF)
    l_scr[...] = jnp.zeros_like(l_scr)
    nt = (((1,), (1,)), ((), ()))

    def step(kj, masked):
        k0 = pl.multiple_of(kj * ATT_BK, ATT_BK)
        kvb = kv_ref[pl.ds(k0, ATT_BK), :]
        krb = kr_ref[pl.ds(k0, ATT_BK), :]
        s = (lax.dot_general(ql, kvb, nt, preferred_element_type=jnp.float32)
             + lax.dot_general(qr, krb, nt, preferred_element_type=jnp.float32)) * MLA_SCALE
        if masked:
            col = k0 + lax.broadcasted_iota(jnp.int32, s.shape, 1)
            tok = qi * ATT_BQ + (lax.broadcasted_iota(jnp.int32, s.shape, 0) & (ATT_BQ - 1))
            s = jnp.where(col <= tok, s, NEG_INF)
        m_old = m_scr[...]
        m_new = jnp.maximum(m_old, jnp.max(s, axis=1, keepdims=True))
        alpha = jnp.exp(m_old - m_new)
        p = jnp.exp(s - m_new)
        l_scr[...] = alpha * l_scr[...] + jnp.sum(p, axis=1, keepdims=True)
        acc_scr[...] = alpha * acc_scr[...] + jnp.dot(p.astype(jnp.bfloat16), kvb,
                                                      preferred_element_type=jnp.float32)
        m_scr[...] = m_new

    n_full = (qi * ATT_BQ) // ATT_BK

    def body(kj, c):
        step(kj, False)
        return c

    lax.fori_loop(0, n_full, body, 0)
    step(n_full, True)
    o = (acc_scr[...] / l_scr[...]).astype(jnp.bfloat16)
    out = jnp.zeros((ATT_BQ, MLA_WIDTH), jnp.float32)
    for h in range(MLA_HEADS):
        out = out + jnp.dot(o[h * ATT_BQ:(h + 1) * ATT_BQ], wuv_ref[h], preferred_element_type=jnp.float32)
    o_ref[...] = out


def mla_attend_prompt(q_lat, q_rope, kv_b, kr_tiled, w_uv, n_seq, seq):
    assert seq % ATT_BK == 0 and ATT_BK % ATT_BQ == 0
    H, C = MLA_HEADS, MLA_KV_LORA
    nq = seq // ATT_BQ
    hh = jnp.arange(H)
    wpad = jnp.zeros((H, C, H, MLA_V), jnp.float32).at[hh, :, hh, :].set(jnp.transpose(w_uv, (1, 0, 2)))
    wpad = wpad.reshape(H, C, H * MLA_V).astype(jnp.bfloat16)
    rows = H * ATT_BQ
    return pl.pallas_call(
        _mla_prompt_kernel,
        grid=(n_seq, nq),
        in_specs=[pl.BlockSpec((ATT_BQ, H * C), lambda b, i: (b * nq + i, 0)),
                  pl.BlockSpec((ATT_BQ, H * MLA_ROPE), lambda b, i: (b * nq + i, 0)),
                  pl.BlockSpec((seq, C), lambda b, i: (b, 0)),
                  pl.BlockSpec((seq, H * MLA_ROPE), lambda b, i: (b, 0)),
                  pl.BlockSpec((H, C, H * MLA_V), lambda b, i: (0, 0, 0))],
        out_specs=pl.BlockSpec((ATT_BQ, H * MLA_V), lambda b, i: (b * nq + i, 0)),
        out_shape=jax.ShapeDtypeStruct((n_seq * seq, H * MLA_V), jnp.float32),
        scratch_shapes=[pltpu.VMEM((rows, C), jnp.float32),
                        pltpu.VMEM((rows, 1), jnp.float32),
                        pltpu.VMEM((rows, 1), jnp.float32)],
        compiler_params=pltpu.CompilerParams(dimension_semantics=("arbitrary", "arbitrary"),
                                             vmem_limit_bytes=VMEM_LIMIT),
        name="mla_prompt",
    )(q_lat, q_rope, kv_b, kr_tiled, wpad)


def mla_attend_sample(q_lat, q_rope, kv_lat, k_rope, pool_lat, pool_rope, page_table, w_uv):
    B, T = q_lat.shape[:2]
    past_lat = pool_lat[page_table].reshape(B, -1, MLA_KV_LORA)
    past_rope = pool_rope[page_table].reshape(B, -1, MLA_ROPE)
    s_past = jnp.einsum('bqhc,bkc->bhqk', q_lat, past_lat) + jnp.einsum('bqhr,bkr->bhqk', q_rope, past_rope)
    s_new = jnp.einsum('bqhc,bkc->bhqk', q_lat, kv_lat) + jnp.einsum('bqhr,bkr->bhqk', q_rope, k_rope)
    causal = jnp.arange(T)[None, :] <= jnp.arange(T)[:, None]
    s_new = jnp.where(causal, s_new.astype(jnp.float32), -jnp.inf)
    s = jnp.concatenate([s_past.astype(jnp.float32), s_new], -1) * MLA_SCALE
    p = jax.nn.softmax(s, axis=-1).astype(kv_lat.dtype)
    P = past_lat.shape[1]
    o = jnp.einsum('bhqk,bkc->bqhc', p[..., :P], past_lat) + jnp.einsum('bhqk,bkc->bqhc', p[..., P:], kv_lat)
    return jnp.einsum('bqhc,chv->bqhv', o, w_uv).reshape(B, T, MLA_WIDTH)


def mlstm_chunk(carry, inp):
    C, n, m = carry
    q, k, v, ig, lf = inp
    L = q.shape[2]
    b = jnp.cumsum(lf, axis=-1)
    causal = jnp.arange(L)[None, :] <= jnp.arange(L)[:, None]
    D = jnp.where(causal, b[..., :, None] - b[..., None, :] + ig[..., None, :], -jnp.inf)
    inter = b + m[..., None]
    m_t = jnp.maximum(inter, D.max(-1))
    A = jnp.exp(D - m_t[..., None]) * jnp.einsum('bhtd,bhsd->bhts', q, k)
    w_inter = jnp.exp(inter - m_t)
    num = w_inter[..., None] * jnp.einsum('bhtd,bhde->bhte', q, C) + jnp.einsum('bhts,bhse->bhte', A, v)
    den = w_inter * jnp.einsum('bhtd,bhd->bht', q, n) + A.sum(-1)
    h = num / jnp.maximum(jnp.abs(den), jnp.exp(-m_t))[..., None]
    b_end = b[..., -1]
    dec = b_end[..., None] - b + ig
    m_new = jnp.maximum(b_end + m, dec.max(-1))
    a_prev = jnp.exp(b_end + m - m_new)
    w_row = jnp.exp(dec - m_new[..., None])
    C_new = a_prev[..., None, None] * C + jnp.einsum('bhs,bhsd,bhse->bhde', w_row, k, v)
    n_new = a_prev[..., None] * n + jnp.einsum('bhs,bhsd->bhd', w_row, k)
    return (C_new, n_new, m_new), h


def mlstm_prompt(q, k, v, ig, lf):
    B, H, S, d = q.shape
    nc = S // ML_CHUNK
    ch = lambda t: jnp.moveaxis(t.reshape((B, H, nc, ML_CHUNK) + t.shape[3:]), 2, 0)
    init = (jnp.zeros((B, H, d, d), jnp.float32), jnp.zeros((B, H, d), jnp.float32), jnp.zeros((B, H), jnp.float32))
    state, h = lax.scan(mlstm_chunk, init, (ch(q), ch(k), ch(v), ch(ig), ch(lf)))
    h = jnp.moveaxis(h, 0, 2).reshape(B, H, S, d)
    return h, state


POST_TM = 256


def _mix_out_kernel(xn_ref, mla_ref, mlh_ref, og_ref, wout_ref, g_ref, b_ref, wmq_ref, x1_ref, qm_ref):
    mixed = jnp.concatenate([mla_ref[...], og_ref[...] * mlh_ref[...]], axis=1).astype(jnp.bfloat16)
    mix = jnp.dot(mixed, wout_ref[...], preferred_element_type=jnp.float32)
    x1 = _layer_norm_rows(ALPHA * xn_ref[...] + mix, g_ref[...], b_ref[...])
    x1_ref[...] = x1
    qm_ref[...] = jnp.dot(x1.astype(jnp.bfloat16), wmq_ref[...],
                          preferred_element_type=jnp.float32).astype(jnp.bfloat16)


def mix_out(xn, mla_o, ml_h, o_gate, w_out, ln_g, ln_b, w_mq):
    T = xn.shape[0]
    bf = jnp.bfloat16
    row = lambda n: pl.BlockSpec((POST_TM, n), lambda i: (i, 0))
    full = lambda shape: pl.BlockSpec(shape, lambda i: (0,) * len(shape))
    return pl.pallas_call(
        _mix_out_kernel,
        grid=(T // POST_TM,),
        in_specs=[row(D_MODEL), row(MLA_WIDTH), row(ML_WIDTH), row(ML_WIDTH), full((D_MIX, D_MODEL)),
                  full((1, D_MODEL)), full((1, D_MODEL)), full((D_MODEL, D_MODEL))],
        out_specs=[row(D_MODEL), row(D_MODEL)],
        out_shape=[jax.ShapeDtypeStruct((T, D_MODEL), jnp.float32), jax.ShapeDtypeStruct((T, D_MODEL), bf)],
        compiler_params=pltpu.CompilerParams(dimension_semantics=("arbitrary",), vmem_limit_bytes=VMEM_LIMIT),
        name="mix_out",
    )(xn, mla_o, ml_h, o_gate, w_out.astype(bf), ln_g.reshape(1, -1), ln_b.reshape(1, -1),
      w_mq.reshape(D_MODEL, D_MODEL).astype(bf))


def _mem_attn_kernel(x1_ref, qm_ref, mk_ref, mv_ref, wmo_ref, g_ref, b_ref, x2_ref):
    q = qm_ref[...]
    mk = mk_ref[...].astype(jnp.bfloat16)
    mv = mv_ref[...].astype(jnp.bfloat16)
    nt = (((1,), (1,)), ((), ()))
    outs = []
    for h in range(MEM_HEADS):
        c = slice(h * MEM_HD, (h + 1) * MEM_HD)
        s = lax.dot_general(q[:, c], mk[:, c], nt, preferred_element_type=jnp.float32) * (MEM_HD ** -0.5)
        m = jnp.max(s, axis=-1, keepdims=True)
        p = jnp.exp(s - m)
        p = p / jnp.sum(p, axis=-1, keepdims=True)
        outs.append(jnp.dot(p.astype(jnp.bfloat16), mv[:, c], preferred_element_type=jnp.float32))
    o = jnp.concatenate(outs, axis=1).astype(jnp.bfloat16)
    att = jnp.dot(o, wmo_ref[...], preferred_element_type=jnp.float32)
    x2_ref[...] = _layer_norm_rows(ALPHA * x1_ref[...] + att, g_ref[...], b_ref[...])


def mem_attend_ln(x1, qm, mem_k, mem_v, w_mo, ln_g, ln_b, n_seq, rows_per_seq):
    bf = jnp.bfloat16
    wmo = w_mo.reshape(D_MODEL, D_MODEL).astype(bf)
    g, b = ln_g.reshape(1, -1), ln_b.reshape(1, -1)
    cp = pltpu.CompilerParams(dimension_semantics=("arbitrary",) * 2, vmem_limit_bytes=VMEM_LIMIT)
    mem = pl.BlockSpec((None, N_MEM, D_MODEL), lambda s, i: (s, 0, 0))
    full = lambda shape: pl.BlockSpec(shape, lambda s, i: (0,) * len(shape))
    if rows_per_seq % POST_TM == 0:
        nb = rows_per_seq // POST_TM
        tok = pl.BlockSpec((POST_TM, D_MODEL), lambda s, i: (s * nb + i, 0))
        return pl.pallas_call(
            _mem_attn_kernel, grid=(n_seq, nb),
            in_specs=[tok, tok, mem, mem, full((D_MODEL, D_MODEL)), full((1, D_MODEL)), full((1, D_MODEL))],
            out_specs=tok, out_shape=jax.ShapeDtypeStruct((n_seq * rows_per_seq, D_MODEL), jnp.float32),
            compiler_params=cp, name="mem_attn_prompt",
        )(x1, qm, mem_k, mem_v, wmo, g, b)
    x3 = x1.reshape(n_seq, rows_per_seq, D_MODEL)
    q3 = qm.reshape(n_seq, rows_per_seq, D_MODEL)
    tok = pl.BlockSpec((None, rows_per_seq, D_MODEL), lambda s, i: (s, 0, 0))
    out = pl.pallas_call(
        _mem_attn_kernel, grid=(n_seq, 1),
        in_specs=[tok, tok, mem, mem, full((D_MODEL, D_MODEL)), full((1, D_MODEL)), full((1, D_MODEL))],
        out_specs=tok, out_shape=jax.ShapeDtypeStruct(x3.shape, jnp.float32),
        compiler_params=cp, name="mem_attn_sample",
    )(x3, q3, mem_k, mem_v, wmo, g, b)
    return out.reshape(x1.shape)


def _mem_kv_kernel(m_ref, w_ref, o_ref):
    o_ref[...] = jnp.dot(m_ref[...].astype(jnp.bfloat16), w_ref[...], preferred_element_type=jnp.float32)


def mem_kv(mem, w_mk, w_mv):
    B = mem.shape[0]
    w = jnp.concatenate([w_mk.reshape(D_MODEL, D_MODEL), w_mv.reshape(D_MODEL, D_MODEL)], axis=1).astype(jnp.bfloat16)
    out = pl.pallas_call(
        _mem_kv_kernel, grid=(B,),
        in_specs=[pl.BlockSpec((N_MEM, D_MODEL), lambda i: (i, 0)), pl.BlockSpec((D_MODEL, 2 * D_MODEL), lambda i: (0, 0))],
        out_specs=pl.BlockSpec((N_MEM, 2 * D_MODEL), lambda i: (i, 0)),
        out_shape=jax.ShapeDtypeStruct((B * N_MEM, 2 * D_MODEL), jnp.float32),
        compiler_params=pltpu.CompilerParams(dimension_semantics=("arbitrary",), vmem_limit_bytes=VMEM_LIMIT),
        name="mem_kv",
    )(mem.reshape(B * N_MEM, D_MODEL), w)
    mk = out[:, :D_MODEL].reshape(B, N_MEM, MEM_HEADS, MEM_HD)
    mv = out[:, D_MODEL:].reshape(B, N_MEM, MEM_HEADS, MEM_HD)
    return mk, mv


PEER_RT = 256
PEER_TB = 512
PEER_EB = 1024


def _top16_rows(s, row_id):
    big = float(2 ** 20)
    out_id = lax.broadcasted_iota(jnp.int32, (PEER_TOPK, s.shape[1]), 0)
    stacked = jnp.zeros((PEER_TOPK, s.shape[1]), jnp.float32)
    rank = jnp.full(s.shape, float(PEER_TOPK), jnp.float32)
    rows, firsts = [], []
    for k in range(PEER_TOPK):
        m = jnp.max(s, axis=0, keepdims=True)
        first = jnp.min(jnp.where(s == m, row_id, big), axis=0, keepdims=True)
        hit = row_id == first
        s = jnp.where(hit, NEG_INF, s)
        rank = jnp.where(hit, float(k), rank)
        rows.append(m)
        firsts.append(first)
        stacked = jnp.where(out_id == k, m, stacked)
    return rows, stacked, firsts, rank, s


def _peer_route_kernel(x_ref, wq_ref, k1_ref, k2_ref, rk2_ref, lim_ref, p1_ref, p2_ref):
    half = PEER_DKEY // 2
    tb = x_ref.shape[0]
    xb = x_ref[...].astype(jnp.bfloat16)
    qt = lax.dot_general(wq_ref[...], xb, (((1,), (1,)), ((), ())), preferred_element_type=jnp.float32)
    row128 = lax.broadcasted_iota(jnp.int32, (PEER_NKEYS, tb), 0).astype(jnp.float32)
    sub8 = lax.broadcasted_iota(jnp.int32, (8, tb), 0)
    sub8f = sub8.astype(jnp.float32)
    for h in range(PEER_HEADS):
        q1 = qt[h * PEER_DKEY:h * PEER_DKEY + half].astype(jnp.bfloat16)
        q2 = qt[h * PEER_DKEY + half:(h + 1) * PEER_DKEY].astype(jnp.bfloat16)
        s1 = jnp.dot(k1_ref[...], q1, preferred_element_type=jnp.float32)
        s2 = jnp.dot(k2_ref[...], q2, preferred_element_type=jnp.float32)
        r1, v1, first1, _, _ = _top16_rows(s1, row128)
        r2, v2, _, rank2, _ = _top16_rows(s2, row128)
        groups, ids = [], []
        for b in range(8):
            lim = PEER_TOPK // (b + 1)
            for a0 in range(0, lim, 8):
                g = v1[a0:a0 + 8] + r2[b]
                if lim - a0 < 8:
                    g = jnp.where(sub8 < lim - a0, g, NEG_INF)
                groups.append(g)
                ids.append((sub8f + float(a0)) * float(PEER_TOPK) + float(b))
        groups.append(r1[0] + v2[8:16])
        ids.append(sub8f + 8.0)
        cand = jnp.concatenate(groups, axis=0)
        vals, _, _, _, left = _top16_rows(cand, jnp.concatenate(ids, axis=0))
        z = jnp.ones_like(vals[0])
        for k in range(1, PEER_TOPK):
            z = z + jnp.exp(vals[k] - vals[0])
        taken = jnp.where((left == NEG_INF) & (cand > NEG_INF), 1.0, 0.0)
        cnt_lo = jnp.zeros((8, tb), jnp.float32)
        gi = 0
        for b in range(8):
            for a0 in range(0, PEER_TOPK // (b + 1), 8):
                if a0 == 0:
                    cnt_lo = cnt_lo + taken[gi * 8:(gi + 1) * 8]
                else:
                    cnt_hi = taken[gi * 8:(gi + 1) * 8]
                gi += 1
        tail = jnp.sum(taken[gi * 8:(gi + 1) * 8], axis=0, keepdims=True)
        cnt_lo = cnt_lo + jnp.where(sub8 == 0, tail, 0.0)
        lim_full = jnp.full((PEER_NKEYS, tb), -1.0, jnp.float32)
        for a in range(PEER_TOPK):
            cnt = cnt_lo if a < 8 else cnt_hi
            lim_full = jnp.where(row128 == first1[a], cnt[a % 8:a % 8 + 1] - 1.0, lim_full)
        rk2_ref[h] = rank2.astype(jnp.bfloat16)
        lim_ref[h] = lim_full
        p1_ref[h] = jnp.exp(s1 - r1[0]) / z
        p2_ref[h] = jnp.exp(s2 - r2[0]).astype(jnp.bfloat16)


def _row_bf16(row):
    r16 = jnp.broadcast_to(row, (16, row.shape[1])).astype(jnp.bfloat16)
    return pltpu.repeat(r16, PEER_NKEYS // 16, axis=0)


def _peer_dense_kernel(x_ref, rk2_ref, lim_ref, p1_ref, p2_ref, u_ref, vt_ref, g_ref, b_ref,
                       o_ref, xb_scr, yt_scr):
    k = pl.program_id(1)
    n_i1 = PEER_EB // PEER_NKEYS

    @pl.when(k == 0)
    def _():
        xb_scr[...] = x_ref[...].astype(jnp.bfloat16)
        yt_scr[...] = jnp.zeros_like(yt_scr)

    n_sub = 2
    sub = PEER_EB // n_sub
    xb = xb_scr[...]
    hts = [lax.dot_general(u_ref[j * sub:(j + 1) * sub, :], xb, (((1,), (1,)), ((), ())),
                           preferred_element_type=jnp.float32) for j in range(n_sub)]
    acc = yt_scr[...]
    for j in range(n_sub):
        pieces = []
        for cc in range(sub // PEER_NKEYS):
            i1 = k * n_i1 + j * (sub // PEER_NKEYS) + cc
            w = jnp.zeros((PEER_NKEYS, x_ref.shape[0]), jnp.bfloat16)
            for h in range(PEER_HEADS):
                p2h = p2_ref[h]
                sel = rk2_ref[h] <= _row_bf16(lim_ref[h, pl.ds(i1, 1), :])
                w = w + _row_bf16(p1_ref[h, pl.ds(i1, 1), :]) * jnp.where(sel, p2h, jnp.zeros_like(p2h))
            hc = hts[j][cc * PEER_NKEYS:(cc + 1) * PEER_NKEYS]
            gelu = 0.5 * hc * (1.0 + lax.erf(hc * (2.0 ** -0.5)))
            pieces.append(w * gelu.astype(jnp.bfloat16))
        at = jnp.concatenate(pieces, axis=0)
        acc = acc + jnp.dot(vt_ref[:, j * sub:(j + 1) * sub], at, preferred_element_type=jnp.float32)
    yt_scr[...] = acc

    @pl.when(k == pl.num_programs(1) - 1)
    def _():
        z = ALPHA * x_ref[...] + yt_scr[...].T
        o_ref[...] = _layer_norm_rows(z, g_ref[...], b_ref[...])


def peer_ln(x, w_pq, sub_k1, sub_k2, peer_u, peer_v, ln_g, ln_b):
    T = x.shape[0]
    assert T % PEER_TB == 0 and T % PEER_RT == 0
    nt = T // PEER_TB
    half = PEER_DKEY // 2
    wq_t = w_pq.reshape(D_MODEL, PEER_HEADS * PEER_DKEY).T.astype(jnp.bfloat16)
    sshape = jax.ShapeDtypeStruct((PEER_HEADS, PEER_NKEYS, T), jnp.float32)
    sspec = pl.BlockSpec((PEER_HEADS, PEER_NKEYS, PEER_RT), lambda j: (0, 0, j))
    rk2, lim, p1, p2 = pl.pallas_call(
        _peer_route_kernel,
        grid=(T // PEER_RT,),
        in_specs=[pl.BlockSpec((PEER_RT, D_MODEL), lambda j: (j, 0)),
                  pl.BlockSpec((PEER_HEADS * PEER_DKEY, D_MODEL), lambda j: (0, 0)),
                  pl.BlockSpec((PEER_NKEYS, half), lambda j: (0, 0)),
                  pl.BlockSpec((PEER_NKEYS, half), lambda j: (0, 0))],
        out_specs=[sspec, sspec, sspec, sspec],
        out_shape=[jax.ShapeDtypeStruct(sshape.shape, jnp.bfloat16), sshape, sshape,
                   jax.ShapeDtypeStruct(sshape.shape, jnp.bfloat16)],
        compiler_params=pltpu.CompilerParams(dimension_semantics=("arbitrary",), vmem_limit_bytes=VMEM_LIMIT),
        name="peer_route",
    )(x, wq_t, sub_k1.astype(jnp.bfloat16), sub_k2.astype(jnp.bfloat16))

    u_b = peer_u.astype(jnp.bfloat16)
    vt_b = peer_v.T.astype(jnp.bfloat16)
    ne = PEER_N // PEER_EB
    sspec2 = pl.BlockSpec((PEER_HEADS, PEER_NKEYS, PEER_TB), lambda j, k: (0, 0, j))
    return pl.pallas_call(
        _peer_dense_kernel,
        grid=(nt, ne),
        in_specs=[pl.BlockSpec((PEER_TB, D_MODEL), lambda j, k: (j, 0)),
                  sspec2, sspec2, sspec2, sspec2,
                  pl.BlockSpec((PEER_EB, D_MODEL), lambda j, k: (k, 0)),
                  pl.BlockSpec((D_MODEL, PEER_EB), lambda j, k: (0, k)),
                  pl.BlockSpec((1, D_MODEL), lambda j, k: (0, 0)),
                  pl.BlockSpec((1, D_MODEL), lambda j, k: (0, 0))],
        out_specs=pl.BlockSpec((PEER_TB, D_MODEL), lambda j, k: (j, 0)),
        out_shape=jax.ShapeDtypeStruct((T, D_MODEL), jnp.float32),
        scratch_shapes=[pltpu.VMEM((PEER_TB, D_MODEL), jnp.bfloat16),
                        pltpu.VMEM((D_MODEL, PEER_TB), jnp.float32)],
        compiler_params=pltpu.CompilerParams(dimension_semantics=("arbitrary", "arbitrary"),
                                             vmem_limit_bytes=VMEM_LIMIT),
        name="peer_dense",
    )(x, rk2, lim, p1, p2, u_b, vt_b, ln_g.reshape(1, -1), ln_b.reshape(1, -1))


def kernel(x_prompt, x_sample, cache_kv_latent, cache_k_rope, state_C, state_n, state_m,
           cache_mem_k, cache_mem_v, page_table, mem_prompt, ln0_g, ln0_b, w_in, b_i, b_f,
           g_q, w_uq, g_kv, w_uk, w_uv, w_out, ln1_g, ln1_b, w_mq, w_mk, w_mv, w_mo,
           ln2_g, ln2_b, w_pq, sub_k1, sub_k2, peer_u, peer_v, ln3_g, ln3_b):
    B, S = x_prompt.shape[:2]
    NB, TQ = x_sample.shape[:2]
    past = page_table.shape[1] * PAGE_SIZE
    n_p = B * S
    l = 0
    x_all = jnp.concatenate([x_prompt.reshape(n_p, D_MODEL), x_sample.reshape(NB * TQ, D_MODEL)], axis=0)
    pos = jnp.concatenate([jnp.tile(jnp.arange(S), B), jnp.tile(past + jnp.arange(TQ), NB)])
    (xn, q_lat, q_rope, kv, kv_b, kr, kr_t, mq, mk, mv, gates, o_gate) = mix_in(
        x_all, pos, ln0_g, ln0_b, w_in[l], b_i[l], b_f[l], g_q[l], w_uq[l], g_kv[l], w_uk[l])
    ig, lf = gates[:, :ML_HEADS], gates[:, ML_HEADS:2 * ML_HEADS]

    mla_p = mla_attend_prompt(q_lat, q_rope, kv_b, kr_t, w_uv[l], B, S)
    heads_p = lambda t: t[:n_p].reshape(B, S, ML_HEADS, -1).transpose(0, 2, 1, 3)
    gate_p = lambda t: t[:n_p].reshape(B, S, ML_HEADS).transpose(0, 2, 1)
    mlh_p, (C_p, n_pst, m_p) = mlstm_prompt(heads_p(mq), heads_p(mk), heads_p(mv), gate_p(ig), gate_p(lf))

    f32 = jnp.float32
    kv_s = kv[n_p:].reshape(NB, TQ, MLA_KV_LORA)
    kr_s = kr[n_p:, :MLA_ROPE].reshape(NB, TQ, MLA_ROPE)
    mla_s = mla_attend_sample(q_lat[n_p:].astype(f32).reshape(NB, TQ, MLA_HEADS, MLA_KV_LORA),
                              q_rope[n_p:].astype(f32).reshape(NB, TQ, MLA_HEADS, MLA_ROPE),
                              kv_s, kr_s, cache_kv_latent[l], cache_k_rope[l], page_table, w_uv[l])
    heads_s = lambda t: t[n_p:].reshape(NB, TQ, ML_HEADS, -1).transpose(0, 2, 1, 3)
    gate_s = lambda t: t[n_p:].reshape(NB, TQ, ML_HEADS).transpose(0, 2, 1)
    (C_s, n_s, m_s), mlh_s = mlstm_chunk((state_C[l], state_n[l], state_m[l]),
                                         (heads_s(mq), heads_s(mk), heads_s(mv), gate_s(ig), gate_s(lf)))

    tok = lambda t, n: t.transpose(0, 2, 1, 3).reshape(n, ML_WIDTH)
    mla_all = jnp.concatenate([mla_p, mla_s.reshape(NB * TQ, MLA_WIDTH)], axis=0)
    mlh_all = jnp.concatenate([tok(mlh_p, n_p), tok(mlh_s, NB * TQ)], axis=0)
    x1, qm = mix_out(xn, mla_all, mlh_all, o_gate, w_out[l], ln1_g[l], ln1_b[l], w_mq[l])
    mk_p, mv_p = mem_kv(mem_prompt, w_mk[l], w_mv[l])
    x2_p = mem_attend_ln(x1, qm, mk_p.reshape(B, N_MEM, D_MODEL), mv_p.reshape(B, N_MEM, D_MODEL),
                         w_mo[l], ln2_g[l], ln2_b[l], B, S)
    x2_s = mem_attend_ln(x1[n_p:], qm[n_p:], cache_mem_k[l].reshape(NB, N_MEM, D_MODEL),
                         cache_mem_v[l].reshape(NB, N_MEM, D_MODEL), w_mo[l], ln2_g[l], ln2_b[l], NB, TQ)
    x2 = jnp.concatenate([x2_p, x2_s], axis=0)
    x3 = peer_ln(x2, w_pq[l], sub_k1[l], sub_k2[l], peer_u[l], peer_v[l], ln3_g[l], ln3_b[l])
    st = lambda t: t[None]
    return (x3[:n_p].reshape(B, S, D_MODEL), x3[n_p:].reshape(NB, TQ, D_MODEL),
            st(kv[:n_p].reshape(B, S, MLA_KV_LORA)), st(kr[:n_p, :MLA_ROPE].reshape(B, S, MLA_ROPE)),
            st(C_p), st(n_pst), st(m_p), st(mk_p), st(mv_p),
            st(kv_s), st(kr_s), st(C_s), st(n_s), st(m_s))
```

```python
import jax, jax.numpy as jnp
from jax import lax
import numpy as np
from jax.experimental import pallas as pl
from jax.experimental.pallas import tpu as pltpu

D_MODEL = 1024
PAGE_SIZE = 128

MLA_HEADS = 8
MLA_NOPE = 64
MLA_ROPE = 32
MLA_V = 64
MLA_KV_LORA = 256
MLA_Q_LORA = 384
MLA_SCALE = (MLA_NOPE + MLA_ROPE) ** -0.5
ROPE_BASE = 10000.0
ML_HEADS = 4
ML_DH = 128
ML_CHUNK = 64
MLA_WIDTH = MLA_HEADS * MLA_V
ML_WIDTH = ML_HEADS * ML_DH
D_MIX = MLA_WIDTH + ML_WIDTH
N_MEM = 256
MEM_HEADS = 4
MEM_HD = D_MODEL // MEM_HEADS
PEER_HEADS = 8
PEER_NKEYS = 128
PEER_N = PEER_NKEYS * PEER_NKEYS
PEER_DKEY = 128
PEER_TOPK = 16
LN_EPS = 1e-5
RMS_EPS = 1e-6
DEPTH = 1
ALPHA = (2 * DEPTH) ** 0.25
NEG_INF = float('-inf')

VMEM_LIMIT = 48 * 1024 * 1024

IN_PAD = 2944
OFF_CQ, OFF_CKV, OFF_KR, OFF_MQ, OFF_MK, OFF_MV, OFF_G, OFF_O = 0, 384, 640, 768, 1280, 1792, 2304, 2432
MIX_TM = 256


def _layer_norm_rows(z, g, b):
    mu = jnp.mean(z, axis=-1, keepdims=True)
    zc = z - mu
    var = jnp.mean(zc * zc, axis=-1, keepdims=True)
    return zc * lax.rsqrt(var + LN_EPS) * g + b


def _rope_lanes(x, cos, sin_signed):
    n = x.shape[1]
    lane = lax.broadcasted_iota(jnp.int32, x.shape, 1)
    partner = jnp.where((lane & 31) < 16, pltpu.roll(x, n - 16, axis=1), pltpu.roll(x, 16, axis=1))
    return x * cos + partner * sin_signed


def _mix_in_kernel(x_ref, g0_ref, b0_ref, win_ref, gq_ref, wuq_ref, wuk_ref, gkv_ref, cos_ref, sin_ref, gb_ref,
                   xn_ref, ql_ref, qr_ref, kv_ref, kvb_ref, kr_ref, krt_ref, mq_ref, mk_ref, mv_ref, gate_ref, og_ref):
    xn = _layer_norm_rows(x_ref[...], g0_ref[...], b0_ref[...])
    xn_ref[...] = xn
    z = jnp.dot(xn.astype(jnp.bfloat16), win_ref[...], preferred_element_type=jnp.float32)
    cos = cos_ref[...]
    sin = sin_ref[...]
    cq = z[:, OFF_CQ:OFF_CQ + MLA_Q_LORA]
    cq = cq * lax.rsqrt(jnp.mean(cq * cq, axis=-1, keepdims=True) + RMS_EPS) * gq_ref[...]
    q = jnp.dot(cq.astype(jnp.bfloat16), wuq_ref[...], preferred_element_type=jnp.float32)
    n_nope = MLA_HEADS * MLA_NOPE
    qrope = _rope_lanes(q[:, n_nope:], jnp.concatenate([cos, cos], axis=1), jnp.concatenate([sin, sin], axis=1))
    qr_ref[...] = qrope.astype(jnp.bfloat16)
    ql_ref[...] = jnp.dot(q[:, :n_nope].astype(jnp.bfloat16), wuk_ref[...],
                          preferred_element_type=jnp.float32).astype(jnp.bfloat16)
    ckv = z[:, OFF_CKV:OFF_CKV + MLA_KV_LORA]
    kv = ckv * lax.rsqrt(jnp.mean(ckv * ckv, axis=-1, keepdims=True) + RMS_EPS) * gkv_ref[...]
    kv_ref[...] = kv
    kvb_ref[...] = kv.astype(jnp.bfloat16)
    kr = _rope_lanes(z[:, OFF_KR:OFF_KR + 128], cos, sin)
    kr_ref[...] = kr
    krt = kr + pltpu.roll(kr, 32, axis=1) + pltpu.roll(kr, 64, axis=1) + pltpu.roll(kr, 96, axis=1)
    krt_ref[...] = jnp.concatenate([krt, krt], axis=1).astype(jnp.bfloat16)
    mq_ref[...] = z[:, OFF_MQ:OFF_MQ + ML_WIDTH]
    mk_ref[...] = z[:, OFF_MK:OFF_MK + ML_WIDTH] * (ML_DH ** -0.5)
    mv_ref[...] = z[:, OFF_MV:OFF_MV + ML_WIDTH]
    g = z[:, OFF_G:OFF_G + 128] + gb_ref[...]
    lane = lax.broadcasted_iota(jnp.int32, g.shape, 1)
    gate_ref[...] = jnp.where(lane < ML_HEADS, g, jax.nn.log_sigmoid(g))
    og_ref[...] = jax.nn.sigmoid(z[:, OFF_O:OFF_O + ML_WIDTH])


def mix_in(x, pos, ln0_g, ln0_b, w_in, b_i, b_f, g_q, w_uq, g_kv, w_uk):
    T = x.shape[0]
    f32, bf = jnp.float32, jnp.bfloat16
    zc = lambda n: jnp.zeros((D_MODEL, n), f32)
    win_p = jnp.concatenate([w_in[:, :672], zc(96), w_in[:, 672:2208], w_in[:, 2208:2216], zc(120), w_in[:, 2216:]],
                            axis=1).astype(bf)
    assert win_p.shape[1] == IN_PAD
    wuq_p = jnp.concatenate([w_uq[:, :, :MLA_NOPE].reshape(MLA_Q_LORA, -1),
                             w_uq[:, :, MLA_NOPE:].reshape(MLA_Q_LORA, -1)], axis=1).astype(bf)
    hh = jnp.arange(MLA_HEADS)
    wuk_blk = jnp.zeros((MLA_HEADS, MLA_NOPE, MLA_HEADS, MLA_KV_LORA), f32)
    wuk_blk = wuk_blk.at[hh, :, hh, :].set(jnp.transpose(w_uk, (1, 2, 0)))
    wuk_blk = wuk_blk.reshape(MLA_HEADS * MLA_NOPE, MLA_HEADS * MLA_KV_LORA).astype(bf)
    inv = 1.0 / (ROPE_BASE ** (jnp.arange(0, MLA_ROPE, 2, dtype=f32) / MLA_ROPE))
    ang = pos.astype(f32)[:, None] * inv[None, :]
    c, s = jnp.cos(ang), jnp.sin(ang)
    cos128 = jnp.tile(jnp.concatenate([c, c], axis=1), (1, 4))
    sin128 = jnp.tile(jnp.concatenate([-s, s], axis=1), (1, 4))
    gbias = jnp.concatenate([b_i, b_f, jnp.zeros((120,), f32)]).reshape(1, 128)
    row = lambda n: pl.BlockSpec((MIX_TM, n), lambda i: (i, 0))
    full = lambda a: pl.BlockSpec(a.shape, lambda i: (0,) * a.ndim)
    ins = [x, ln0_g.reshape(1, -1), ln0_b.reshape(1, -1), win_p, g_q.reshape(1, -1), wuq_p, wuk_blk,
           g_kv.reshape(1, -1), cos128, sin128, gbias]
    in_specs = [row(D_MODEL)] + [full(a) for a in ins[1:8]] + [row(128), row(128), full(gbias)]
    outs = [(D_MODEL, f32), (MLA_HEADS * MLA_KV_LORA, bf), (MLA_HEADS * MLA_ROPE, bf), (MLA_KV_LORA, f32),
            (MLA_KV_LORA, bf), (128, f32), (MLA_HEADS * MLA_ROPE, bf), (ML_WIDTH, f32), (ML_WIDTH, f32),
            (ML_WIDTH, f32), (128, f32), (ML_WIDTH, f32)]
    return pl.pallas_call(
        _mix_in_kernel,
        grid=(T // MIX_TM,),
        in_specs=in_specs,
        out_specs=[row(n) for n, _ in outs],
        out_shape=[jax.ShapeDtypeStruct((T, n), dt) for n, dt in outs],
        compiler_params=pltpu.CompilerParams(dimension_semantics=("arbitrary",), vmem_limit_bytes=VMEM_LIMIT),
        name="mix_in",
    )(*ins)


ATT_BQ = 128
ATT_BK = 512


def _mla_prompt_kernel(ql_ref, qr_ref, kv_ref, kr_ref, wuv_ref, o_ref, acc_scr, m_scr, l_scr):
    qi = pl.program_id(1)
    ql = jnp.concatenate([ql_ref[:, h * MLA_KV_LORA:(h + 1) * MLA_KV_LORA] for h in range(MLA_HEADS)], axis=0)
    qr_all = qr_ref[...]
    lane_head = lax.broadcasted_iota(jnp.int32, qr_all.shape, 1) // MLA_ROPE
    qr = jnp.concatenate([jnp.where(lane_head == h, qr_all, jnp.zeros_like(qr_all)) for h in range(MLA_HEADS)],
                         axis=0)
    acc_scr[...] = jnp.zeros_like(acc_scr)
    m_scr[...] = jnp.full_like(m_scr, NEG_INF)
    l_scr[...] = jnp.zeros_like(l_scr)
    nt = (((1,), (1,)), ((), ()))

    def step(kj, masked):
        k0 = pl.multiple_of(kj * ATT_BK, ATT_BK)
        kvb = kv_ref[pl.ds(k0, ATT_BK), :]
        krb = kr_ref[pl.ds(k0, ATT_BK), :]
        s = (lax.dot_general(ql, kvb, nt, preferred_element_type=jnp.float32)
             + lax.dot_general(qr, krb, nt, preferred_element_type=jnp.float32)) * MLA_SCALE
        if masked:
            col = k0 + lax.broadcasted_iota(jnp.int32, s.shape, 1)
            tok = qi * ATT_BQ + (lax.broadcasted_iota(jnp.int32, s.shape, 0) & (ATT_BQ - 1))
            s = jnp.where(col <= tok, s, NEG_INF)
        m_old = m_scr[...]
        m_new = jnp.maximum(m_old, jnp.max(s, axis=1, keepdims=True))
        alpha = jnp.exp(m_old - m_new)
        p = jnp.exp(s - m_new)
        l_scr[...] = alpha * l_scr[...] + jnp.sum(p, axis=1, keepdims=True)
        acc_scr[...] = alpha * acc_scr[...] + jnp.dot(p.astype(jnp.bfloat16), kvb,
                                                      preferred_element_type=jnp.float32)
        m_scr[...] = m_new

    n_full = (qi * ATT_BQ) // ATT_BK

    def body(kj, c):
        step(kj, False)
        return c

    lax.fori_loop(0, n_full, body, 0)
    step(n_full, True)
    o = (acc_scr[...] / l_scr[...]).astype(jnp.bfloat16)
    out = jnp.zeros((ATT_BQ, MLA_WIDTH), jnp.float32)
    for h in range(MLA_HEADS):
        out = out + jnp.dot(o[h * ATT_BQ:(h + 1) * ATT_BQ], wuv_ref[h], preferred_element_type=jnp.float32)
    o_ref[...] = out


def mla_attend_prompt(q_lat, q_rope, kv_b, kr_tiled, w_uv, n_seq, seq):
    assert seq % ATT_BK == 0 and ATT_BK % ATT_BQ == 0
    H, C = MLA_HEADS, MLA_KV_LORA
    nq = seq // ATT_BQ
    hh = jnp.arange(H)
    wpad = jnp.zeros((H, C, H, MLA_V), jnp.float32).at[hh, :, hh, :].set(jnp.transpose(w_uv, (1, 0, 2)))
    wpad = wpad.reshape(H, C, H * MLA_V).astype(jnp.bfloat16)
    rows = H * ATT_BQ
    return pl.pallas_call(
        _mla_prompt_kernel,
        grid=(n_seq, nq),
        in_specs=[pl.BlockSpec((ATT_BQ, H * C), lambda b, i: (b * nq + i, 0)),
                  pl.BlockSpec((ATT_BQ, H * MLA_ROPE), lambda b, i: (b * nq + i, 0)),
                  pl.BlockSpec((seq, C), lambda b, i: (b, 0)),
                  pl.BlockSpec((seq, H * MLA_ROPE), lambda b, i: (b, 0)),
                  pl.BlockSpec((H, C, H * MLA_V), lambda b, i: (0, 0, 0))],
        out_specs=pl.BlockSpec((ATT_BQ, H * MLA_V), lambda b, i: (b * nq + i, 0)),
        out_shape=jax.ShapeDtypeStruct((n_seq * seq, H * MLA_V), jnp.float32),
        scratch_shapes=[pltpu.VMEM((rows, C), jnp.float32),
                        pltpu.VMEM((rows, 1), jnp.float32),
                        pltpu.VMEM((rows, 1), jnp.float32)],
        compiler_params=pltpu.CompilerParams(dimension_semantics=("arbitrary", "arbitrary"),
                                             vmem_limit_bytes=VMEM_LIMIT),
        name="mla_prompt",
    )(q_lat, q_rope, kv_b, kr_tiled, wpad)


SMP_KC = 2048


def _mla_sample_kernel(pt_ref, ql_ref, qr_ref, kvn_ref, krn_ref, wuv_ref, lat_hbm, rope_hbm, o_ref,
                       lat_buf, rope_buf, lat_bf, sem_lat, sem_rope):
    b = pl.program_id(0)
    nb = pl.num_programs(0)
    n_pages = pt_ref.shape[1]
    rows = ql_ref.shape[0]
    tq = kvn_ref.shape[0]
    n_keys = n_pages * PAGE_SIZE

    def page_copies(seq, slot, p):
        page = pt_ref[seq, p]
        dst = pl.ds(p * PAGE_SIZE, PAGE_SIZE)
        return (pltpu.make_async_copy(lat_hbm.at[page], lat_buf.at[slot, dst], sem_lat.at[slot]),
                pltpu.make_async_copy(rope_hbm.at[page], rope_buf.at[slot, dst], sem_rope.at[slot]))

    def start_fetch(seq, slot):
        def body(p, c):
            for cp in page_copies(seq, slot, p):
                cp.start()
            return c
        lax.fori_loop(0, n_pages, body, 0)

    def wait_fetch(seq, slot):
        def body(p, c):
            for cp in page_copies(seq, slot, p):
                cp.wait()
            return c
        lax.fori_loop(0, n_pages, body, 0)

    slot = b % 2

    @pl.when(b == 0)
    def _():
        start_fetch(0, 0)

    @pl.when(b + 1 < nb)
    def _():
        start_fetch(b + 1, 1 - slot)

    wait_fetch(b, slot)

    bf, f32 = jnp.bfloat16, jnp.float32
    ql = ql_ref[...]
    qr = qr_ref[...]
    nt = (((1,), (1,)), ((), ()))
    parts = []
    for c in range(n_keys // SMP_KC):
        r = pl.ds(c * SMP_KC, SMP_KC)
        lb = lat_buf[slot, r, :].astype(bf)
        rb = rope_buf[slot, r, :].astype(bf)
        lat_bf[r, :] = lb
        parts.append(lax.dot_general(ql, lb, nt, preferred_element_type=f32)
                     + lax.dot_general(qr, rb, nt, preferred_element_type=f32))
    s_past = jnp.concatenate(parts, axis=1) * MLA_SCALE
    qlf, qrf = ql.astype(f32), qr.astype(f32)
    kvn = kvn_ref[...].astype(f32)
    krn = krn_ref[...].astype(f32)
    tok = lax.broadcasted_iota(jnp.int32, (rows, 1), 0) % tq
    s_new = []
    for j in range(tq):
        sj = (jnp.sum(qlf * kvn[j:j + 1, :], axis=1, keepdims=True)
              + jnp.sum(qrf * krn[j:j + 1, :], axis=1, keepdims=True)) * MLA_SCALE
        s_new.append(jnp.where(tok >= j, sj, NEG_INF))
    m = jnp.max(s_past, axis=1, keepdims=True)
    for sj in s_new:
        m = jnp.maximum(m, sj)
    p_past = jnp.exp(s_past - m)
    p_new = [jnp.exp(sj - m) for sj in s_new]
    l = jnp.sum(p_past, axis=1, keepdims=True)
    for pj in p_new:
        l = l + pj
    inv = 1.0 / l
    o = jnp.dot((p_past * inv).astype(bf), lat_bf[...], preferred_element_type=f32)
    for j in range(tq):
        o = o + (p_new[j] * inv).astype(bf).astype(f32) * kvn[j:j + 1, :]
    ob = o.astype(bf)
    out = jnp.zeros((tq, MLA_WIDTH), f32)
    for h in range(MLA_HEADS):
        out = out + jnp.dot(ob, wuv_ref[h], preferred_element_type=f32)[h * tq:(h + 1) * tq, :]
    o_ref[...] = out


def mla_attend_sample(q_lat, q_rope, kv_new, kr_new, pool_lat, pool_rope, page_table, w_uv):
    NB, T = q_lat.shape[:2]
    H, C, R = MLA_HEADS, MLA_KV_LORA, MLA_ROPE
    bf = jnp.bfloat16
    n_pages = page_table.shape[1]
    n_keys = n_pages * PAGE_SIZE
    assert n_keys % SMP_KC == 0
    ql = q_lat.reshape(NB, T, H, C).transpose(0, 2, 1, 3).reshape(NB, H * T, C)
    qr = q_rope.reshape(NB, T, H, R).transpose(0, 2, 1, 3).reshape(NB, H * T, R)
    hh = jnp.arange(H)
    wpad = jnp.zeros((H, C, H, MLA_V), jnp.float32).at[hh, :, hh, :].set(jnp.transpose(w_uv, (1, 0, 2)))
    wpad = wpad.reshape(H, C, H * MLA_V).astype(bf)
    seq = lambda n, w: pl.BlockSpec((None, n, w), lambda b, pt: (b, 0, 0))
    grid_spec = pltpu.PrefetchScalarGridSpec(
        num_scalar_prefetch=1,
        grid=(NB,),
        in_specs=[seq(H * T, C), seq(H * T, R), seq(T, C), seq(T, R),
                  pl.BlockSpec((H, C, H * MLA_V), lambda b, pt: (0, 0, 0)),
                  pl.BlockSpec(memory_space=pl.ANY), pl.BlockSpec(memory_space=pl.ANY)],
        out_specs=seq(T, H * MLA_V),
        scratch_shapes=[pltpu.VMEM((2, n_keys, C), jnp.float32), pltpu.VMEM((2, n_keys, R), jnp.float32),
                        pltpu.VMEM((n_keys, C), bf),
                        pltpu.SemaphoreType.DMA((2,)), pltpu.SemaphoreType.DMA((2,))],
    )
    out = pl.pallas_call(
        _mla_sample_kernel,
        grid_spec=grid_spec,
        out_shape=jax.ShapeDtypeStruct((NB, T, H * MLA_V), jnp.float32),
        compiler_params=pltpu.CompilerParams(dimension_semantics=("arbitrary",), vmem_limit_bytes=VMEM_LIMIT),
        name="mla_sample",
    )(page_table, ql, qr, kv_new.astype(bf), kr_new.astype(bf), wpad, pool_lat, pool_rope)
    return out.reshape(NB * T, H * MLA_V)


def _split3(x):
    hi = x.astype(jnp.bfloat16)
    r1 = x - hi.astype(jnp.float32)
    mid = r1.astype(jnp.bfloat16)
    lo = (r1 - mid.astype(jnp.float32)).astype(jnp.bfloat16)
    return hi, mid, lo


def _mlstm_chunk(q_all, k_all, v_all, gates, c_refs, n_refs, m_refs):
    L = q_all.shape[0]
    bf, f32 = jnp.bfloat16, jnp.float32
    row_t = lax.broadcasted_iota(jnp.int32, (L, L), 0)
    col_s = lax.broadcasted_iota(jnp.int32, (L, L), 1)
    causal = col_s <= row_t
    tril = jnp.where(causal, 1.0, 0.0).astype(bf)
    cum = sum(jnp.dot(tril, part, preferred_element_type=f32) for part in _split3(gates))
    gates_t = gates.T
    cum_t = cum.T
    nt = (((1,), (1,)), ((), ()))
    outs = []
    for h in range(ML_HEADS):
        c = slice(h * ML_DH, (h + 1) * ML_DH)
        q, k, v = q_all[:, c], k_all[:, c], v_all[:, c]
        C, n, m = c_refs[0](h), n_refs[0](h), m_refs[0](h)
        ig_col = gates[:, h:h + 1]
        b_col = cum[:, ML_HEADS + h:ML_HEADS + h + 1]
        ig_row = gates_t[h:h + 1, :]
        b_row = cum_t[ML_HEADS + h:ML_HEADS + h + 1, :]
        D = jnp.where(causal, b_col - b_row + ig_row, NEG_INF)
        inter = b_col + m
        m_t = jnp.maximum(inter, jnp.max(D, axis=1, keepdims=True))
        qb, kb, vb = q.astype(bf), k.astype(bf), v.astype(bf)
        A = jnp.exp(D - m_t) * lax.dot_general(qb, kb, nt, preferred_element_type=f32)
        w_inter = jnp.exp(inter - m_t)
        num = w_inter * jnp.dot(qb, C.astype(bf), preferred_element_type=f32) \
            + jnp.dot(A.astype(bf), vb, preferred_element_type=f32)
        qn = jnp.sum(qb.astype(f32) * n.astype(bf).astype(f32), axis=1, keepdims=True)
        den = w_inter * qn + jnp.sum(A, axis=1, keepdims=True)
        outs.append(num / jnp.maximum(jnp.abs(den), jnp.exp(-m_t)))
        b_end = b_col[L - 1:L, :]
        m_new = jnp.maximum(b_end + m, jnp.max(b_end - b_row + ig_row, axis=1, keepdims=True))
        a_prev = jnp.exp(b_end + m - m_new)
        kw = k * jnp.exp(b_end - b_col + ig_col - m_new)
        c_refs[1](h, a_prev * C + jnp.dot(kw.T.astype(bf), vb, preferred_element_type=f32))
        n_refs[1](h, a_prev * n + jnp.sum(kw, axis=0, keepdims=True))
        m_refs[1](h, m_new)
    return jnp.concatenate(outs, axis=1)


def _state_access(c_get, n_get, m_get, c_set, n_set, m_set):
    c_refs = (lambda h: c_get[h], lambda h, val: c_set.__setitem__(h, val))
    n_refs = (lambda h: n_get[h:h + 1, :], lambda h, val: n_set.__setitem__((slice(h, h + 1), slice(None)), val))
    m_refs = (lambda h: m_get[h:h + 1, 0:1],
              lambda h, val: m_set.__setitem__((slice(h, h + 1), slice(None)), jnp.broadcast_to(val, (1, 128))))
    return c_refs, n_refs, m_refs


def _mlstm_prompt_kernel(q_ref, k_ref, v_ref, g_ref, h_ref, c_out, n_out, m_out, c_scr, n_scr, m_scr):
    j = pl.program_id(1)

    @pl.when(j == 0)
    def _():
        c_scr[...] = jnp.zeros_like(c_scr)
        n_scr[...] = jnp.zeros_like(n_scr)
        m_scr[...] = jnp.zeros_like(m_scr)

    h_ref[...] = _mlstm_chunk(q_ref[...], k_ref[...], v_ref[...], g_ref[...],
                              *_state_access(c_scr, n_scr, m_scr, c_scr, n_scr, m_scr))

    @pl.when(j == pl.num_programs(1) - 1)
    def _():
        c_out[...] = c_scr[...]
        n_out[...] = n_scr[0:ML_HEADS, :]
        m_out[...] = m_scr[0:ML_HEADS, :]


def mlstm_prompt(mq, mk, mv, gates, n_seq, seq):
    nc = seq // ML_CHUNK
    f32 = jnp.float32
    tok = lambda w: pl.BlockSpec((ML_CHUNK, w), lambda b, j: (b * nc + j, 0))
    h, C, n, m = pl.pallas_call(
        _mlstm_prompt_kernel,
        grid=(n_seq, nc),
        in_specs=[tok(ML_WIDTH), tok(ML_WIDTH), tok(ML_WIDTH), tok(128)],
        out_specs=[tok(ML_WIDTH),
                   pl.BlockSpec((None, ML_HEADS, ML_DH, ML_DH), lambda b, j: (b, 0, 0, 0)),
                   pl.BlockSpec((None, ML_HEADS, ML_DH), lambda b, j: (b, 0, 0)),
                   pl.BlockSpec((None, ML_HEADS, 128), lambda b, j: (b, 0, 0))],
        out_shape=[jax.ShapeDtypeStruct((n_seq * seq, ML_WIDTH), f32),
                   jax.ShapeDtypeStruct((n_seq, ML_HEADS, ML_DH, ML_DH), f32),
                   jax.ShapeDtypeStruct((n_seq, ML_HEADS, ML_DH), f32),
                   jax.ShapeDtypeStruct((n_seq, ML_HEADS, 128), f32)],
        scratch_shapes=[pltpu.VMEM((ML_HEADS, ML_DH, ML_DH), f32), pltpu.VMEM((8, ML_DH), f32),
                        pltpu.VMEM((8, 128), f32)],
        compiler_params=pltpu.CompilerParams(dimension_semantics=("arbitrary", "arbitrary"),
                                             vmem_limit_bytes=VMEM_LIMIT),
        name="mlstm_prompt",
    )(mq, mk, mv, gates)
    return h, C, n, m[:, :, 0]


def _mlstm_step_kernel(q_ref, k_ref, v_ref, g_ref, c_in, n_in, m_in, h_ref, c_out, n_out, m_out):
    h_ref[...] = _mlstm_chunk(q_ref[...], k_ref[...], v_ref[...], g_ref[...],
                              *_state_access(c_in, n_in, m_in, c_out, n_out, m_out))


def mlstm_step(mq, mk, mv, gates, state_C, state_n, state_m, n_seq, rows):
    f32 = jnp.float32
    tok = lambda w: pl.BlockSpec((None, rows, w), lambda b: (b, 0, 0))
    st_c = pl.BlockSpec((None, ML_HEADS, ML_DH, ML_DH), lambda b: (b, 0, 0, 0))
    st_n = pl.BlockSpec((None, ML_HEADS, ML_DH), lambda b: (b, 0, 0))
    st_m = pl.BlockSpec((None, ML_HEADS, 128), lambda b: (b, 0, 0))
    r3 = lambda t: t.reshape(n_seq, rows, t.shape[-1])
    m_in = jnp.broadcast_to(state_m[:, :, None], (n_seq, ML_HEADS, 128))
    h, C, n, m = pl.pallas_call(
        _mlstm_step_kernel,
        grid=(n_seq,),
        in_specs=[tok(ML_WIDTH), tok(ML_WIDTH), tok(ML_WIDTH), tok(128), st_c, st_n, st_m],
        out_specs=[tok(ML_WIDTH), st_c, st_n, st_m],
        out_shape=[jax.ShapeDtypeStruct((n_seq, rows, ML_WIDTH), f32),
                   jax.ShapeDtypeStruct((n_seq, ML_HEADS, ML_DH, ML_DH), f32),
                   jax.ShapeDtypeStruct((n_seq, ML_HEADS, ML_DH), f32),
                   jax.ShapeDtypeStruct((n_seq, ML_HEADS, 128), f32)],
        compiler_params=pltpu.CompilerParams(dimension_semantics=("arbitrary",), vmem_limit_bytes=VMEM_LIMIT),
        name="mlstm_step",
    )(r3(mq), r3(mk), r3(mv), r3(gates), state_C, state_n, m_in)
    return h.reshape(n_seq * rows, ML_WIDTH), C, n, m[:, :, 0]


POST_TM = 256


def _mix_out_kernel(xn_ref, mla_ref, mlh_ref, og_ref, wout_ref, g_ref, b_ref, wmq_ref, x1_ref, qm_ref):
    mixed = jnp.concatenate([mla_ref[...], og_ref[...] * mlh_ref[...]], axis=1).astype(jnp.bfloat16)
    mix = jnp.dot(mixed, wout_ref[...], preferred_element_type=jnp.float32)
    x1 = _layer_norm_rows(ALPHA * xn_ref[...] + mix, g_ref[...], b_ref[...])
    x1_ref[...] = x1
    qm_ref[...] = jnp.dot(x1.astype(jnp.bfloat16), wmq_ref[...],
                          preferred_element_type=jnp.float32).astype(jnp.bfloat16)


def mix_out(xn, mla_o, ml_h, o_gate, w_out, ln_g, ln_b, w_mq):
    T = xn.shape[0]
    bf = jnp.bfloat16
    row = lambda n: pl.BlockSpec((POST_TM, n), lambda i: (i, 0))
    full = lambda shape: pl.BlockSpec(shape, lambda i: (0,) * len(shape))
    return pl.pallas_call(
        _mix_out_kernel,
        grid=(T // POST_TM,),
        in_specs=[row(D_MODEL), row(MLA_WIDTH), row(ML_WIDTH), row(ML_WIDTH), full((D_MIX, D_MODEL)),
                  full((1, D_MODEL)), full((1, D_MODEL)), full((D_MODEL, D_MODEL))],
        out_specs=[row(D_MODEL), row(D_MODEL)],
        out_shape=[jax.ShapeDtypeStruct((T, D_MODEL), jnp.float32), jax.ShapeDtypeStruct((T, D_MODEL), bf)],
        compiler_params=pltpu.CompilerParams(dimension_semantics=("arbitrary",), vmem_limit_bytes=VMEM_LIMIT),
        name="mix_out",
    )(xn, mla_o, ml_h, o_gate, w_out.astype(bf), ln_g.reshape(1, -1), ln_b.reshape(1, -1),
      w_mq.reshape(D_MODEL, D_MODEL).astype(bf))


def _mem_attn_kernel(x1_ref, qm_ref, mk_ref, mv_ref, wmo_ref, g_ref, b_ref, x2_ref):
    q = qm_ref[...]
    mk = mk_ref[...].astype(jnp.bfloat16)
    mv = mv_ref[...].astype(jnp.bfloat16)
    nt = (((1,), (1,)), ((), ()))
    outs = []
    for h in range(MEM_HEADS):
        c = slice(h * MEM_HD, (h + 1) * MEM_HD)
        s = lax.dot_general(q[:, c], mk[:, c], nt, preferred_element_type=jnp.float32) * (MEM_HD ** -0.5)
        m = jnp.max(s, axis=-1, keepdims=True)
        p = jnp.exp(s - m)
        p = p / jnp.sum(p, axis=-1, keepdims=True)
        outs.append(jnp.dot(p.astype(jnp.bfloat16), mv[:, c], preferred_element_type=jnp.float32))
    o = jnp.concatenate(outs, axis=1).astype(jnp.bfloat16)
    att = jnp.dot(o, wmo_ref[...], preferred_element_type=jnp.float32)
    x2_ref[...] = _layer_norm_rows(ALPHA * x1_ref[...] + att, g_ref[...], b_ref[...])


def mem_attend_ln(x1, qm, mem_k, mem_v, w_mo, ln_g, ln_b, n_seq, rows_per_seq):
    bf = jnp.bfloat16
    wmo = w_mo.reshape(D_MODEL, D_MODEL).astype(bf)
    g, b = ln_g.reshape(1, -1), ln_b.reshape(1, -1)
    cp = pltpu.CompilerParams(dimension_semantics=("arbitrary",) * 2, vmem_limit_bytes=VMEM_LIMIT)
    mem = pl.BlockSpec((None, N_MEM, D_MODEL), lambda s, i: (s, 0, 0))
    full = lambda shape: pl.BlockSpec(shape, lambda s, i: (0,) * len(shape))
    if rows_per_seq % POST_TM == 0:
        nb = rows_per_seq // POST_TM
        tok = pl.BlockSpec((POST_TM, D_MODEL), lambda s, i: (s * nb + i, 0))
        return pl.pallas_call(
            _mem_attn_kernel, grid=(n_seq, nb),
            in_specs=[tok, tok, mem, mem, full((D_MODEL, D_MODEL)), full((1, D_MODEL)), full((1, D_MODEL))],
            out_specs=tok, out_shape=jax.ShapeDtypeStruct((n_seq * rows_per_seq, D_MODEL), jnp.float32),
            compiler_params=cp, name="mem_attn_prompt",
        )(x1, qm, mem_k, mem_v, wmo, g, b)
    x3 = x1.reshape(n_seq, rows_per_seq, D_MODEL)
    q3 = qm.reshape(n_seq, rows_per_seq, D_MODEL)
    tok = pl.BlockSpec((None, rows_per_seq, D_MODEL), lambda s, i: (s, 0, 0))
    out = pl.pallas_call(
        _mem_attn_kernel, grid=(n_seq, 1),
        in_specs=[tok, tok, mem, mem, full((D_MODEL, D_MODEL)), full((1, D_MODEL)), full((1, D_MODEL))],
        out_specs=tok, out_shape=jax.ShapeDtypeStruct(x3.shape, jnp.float32),
        compiler_params=cp, name="mem_attn_sample",
    )(x3, q3, mem_k, mem_v, wmo, g, b)
    return out.reshape(x1.shape)


def _mem_kv_kernel(m_ref, w_ref, o_ref):
    o_ref[...] = jnp.dot(m_ref[...].astype(jnp.bfloat16), w_ref[...], preferred_element_type=jnp.float32)


def mem_kv(mem, w_mk, w_mv):
    B = mem.shape[0]
    w = jnp.concatenate([w_mk.reshape(D_MODEL, D_MODEL), w_mv.reshape(D_MODEL, D_MODEL)], axis=1).astype(jnp.bfloat16)
    out = pl.pallas_call(
        _mem_kv_kernel, grid=(B,),
        in_specs=[pl.BlockSpec((N_MEM, D_MODEL), lambda i: (i, 0)), pl.BlockSpec((D_MODEL, 2 * D_MODEL), lambda i: (0, 0))],
        out_specs=pl.BlockSpec((N_MEM, 2 * D_MODEL), lambda i: (i, 0)),
        out_shape=jax.ShapeDtypeStruct((B * N_MEM, 2 * D_MODEL), jnp.float32),
        compiler_params=pltpu.CompilerParams(dimension_semantics=("arbitrary",), vmem_limit_bytes=VMEM_LIMIT),
        name="mem_kv",
    )(mem.reshape(B * N_MEM, D_MODEL), w)
    mk = out[:, :D_MODEL].reshape(B, N_MEM, MEM_HEADS, MEM_HD)
    mv = out[:, D_MODEL:].reshape(B, N_MEM, MEM_HEADS, MEM_HD)
    return mk, mv


PEER_RT = 256
PEER_TB = 512
PEER_EB = 1024


def _top16_rows(s, row_id):
    big = float(2 ** 20)
    out_id = lax.broadcasted_iota(jnp.int32, (PEER_TOPK, s.shape[1]), 0)
    stacked = jnp.zeros((PEER_TOPK, s.shape[1]), jnp.float32)
    rank = jnp.full(s.shape, float(PEER_TOPK), jnp.float32)
    rows, firsts = [], []
    for k in range(PEER_TOPK):
        m = jnp.max(s, axis=0, keepdims=True)
        first = jnp.min(jnp.where(s == m, row_id, big), axis=0, keepdims=True)
        hit = row_id == first
        s = jnp.where(hit, NEG_INF, s)
        rank = jnp.where(hit, float(k), rank)
        rows.append(m)
        firsts.append(first)
        stacked = jnp.where(out_id == k, m, stacked)
    return rows, stacked, firsts, rank, s


def _peer_route_kernel(x_ref, wq_ref, k1_ref, k2_ref, rk2_ref, lim_ref, p1_ref, p2_ref):
    half = PEER_DKEY // 2
    tb = x_ref.shape[0]
    xb = x_ref[...].astype(jnp.bfloat16)
    qt = lax.dot_general(wq_ref[...], xb, (((1,), (1,)), ((), ())), preferred_element_type=jnp.float32)
    row128 = lax.broadcasted_iota(jnp.int32, (PEER_NKEYS, tb), 0).astype(jnp.float32)
    sub8 = lax.broadcasted_iota(jnp.int32, (8, tb), 0)
    sub8f = sub8.astype(jnp.float32)
    for h in range(PEER_HEADS):
        q1 = qt[h * PEER_DKEY:h * PEER_DKEY + half].astype(jnp.bfloat16)
        q2 = qt[h * PEER_DKEY + half:(h + 1) * PEER_DKEY].astype(jnp.bfloat16)
        s1 = jnp.dot(k1_ref[...], q1, preferred_element_type=jnp.float32)
        s2 = jnp.dot(k2_ref[...], q2, preferred_element_type=jnp.float32)
        r1, v1, first1, _, _ = _top16_rows(s1, row128)
        r2, v2, _, rank2, _ = _top16_rows(s2, row128)
        groups, ids = [], []
        for b in range(8):
            lim = PEER_TOPK // (b + 1)
            for a0 in range(0, lim, 8):
                g = v1[a0:a0 + 8] + r2[b]
                if lim - a0 < 8:
                    g = jnp.where(sub8 < lim - a0, g, NEG_INF)
                groups.append(g)
                ids.append((sub8f + float(a0)) * float(PEER_TOPK) + float(b))
        groups.append(r1[0] + v2[8:16])
        ids.append(sub8f + 8.0)
        cand = jnp.concatenate(groups, axis=0)
        vals, _, _, _, left = _top16_rows(cand, jnp.concatenate(ids, axis=0))
        z = jnp.ones_like(vals[0])
        for k in range(1, PEER_TOPK):
            z = z + jnp.exp(vals[k] - vals[0])
        taken = jnp.where((left == NEG_INF) & (cand > NEG_INF), 1.0, 0.0)
        cnt_lo = jnp.zeros((8, tb), jnp.float32)
        gi = 0
        for b in range(8):
            for a0 in range(0, PEER_TOPK // (b + 1), 8):
                if a0 == 0:
                    cnt_lo = cnt_lo + taken[gi * 8:(gi + 1) * 8]
                else:
                    cnt_hi = taken[gi * 8:(gi + 1) * 8]
                gi += 1
        tail = jnp.sum(taken[gi * 8:(gi + 1) * 8], axis=0, keepdims=True)
        cnt_lo = cnt_lo + jnp.where(sub8 == 0, tail, 0.0)
        lim_full = jnp.full((PEER_NKEYS, tb), -1.0, jnp.float32)
        for a in range(PEER_TOPK):
            cnt = cnt_lo if a < 8 else cnt_hi
            lim_full = jnp.where(row128 == first1[a], cnt[a % 8:a % 8 + 1] - 1.0, lim_full)
        rk2_ref[h] = rank2.astype(jnp.bfloat16)
        lim_ref[h] = lim_full
        p1_ref[h] = jnp.exp(s1 - r1[0]) / z
        p2_ref[h] = jnp.exp(s2 - r2[0]).astype(jnp.bfloat16)


def _row_bf16(row):
    r16 = jnp.broadcast_to(row, (16, row.shape[1])).astype(jnp.bfloat16)
    return jnp.concatenate([r16] * (PEER_NKEYS // 16), axis=0)


def _peer_dense_kernel(x_ref, rk2_ref, lim_ref, p1_ref, p2_ref, u_ref, vt_ref, g_ref, b_ref,
                       o_ref, xb_scr, yt_scr):
    k = pl.program_id(1)
    n_i1 = PEER_EB // PEER_NKEYS

    @pl.when(k == 0)
    def _():
        xb_scr[...] = x_ref[...].astype(jnp.bfloat16)
        yt_scr[...] = jnp.zeros_like(yt_scr)

    n_sub = 2
    sub = PEER_EB // n_sub
    xb = xb_scr[...]
    hts = [lax.dot_general(u_ref[j * sub:(j + 1) * sub, :], xb, (((1,), (1,)), ((), ())),
                           preferred_element_type=jnp.float32) for j in range(n_sub)]
    acc = yt_scr[...]
    for j in range(n_sub):
        pieces = []
        for cc in range(sub // PEER_NKEYS):
            i1 = k * n_i1 + j * (sub // PEER_NKEYS) + cc
            w = jnp.zeros((PEER_NKEYS, x_ref.shape[0]), jnp.bfloat16)
            for h in range(PEER_HEADS):
                p2h = p2_ref[h]
                sel = rk2_ref[h] <= _row_bf16(lim_ref[h, pl.ds(i1, 1), :])
                w = w + _row_bf16(p1_ref[h, pl.ds(i1, 1), :]) * jnp.where(sel, p2h, jnp.zeros_like(p2h))
            hc = hts[j][cc * PEER_NKEYS:(cc + 1) * PEER_NKEYS]
            gelu = 0.5 * hc * (1.0 + lax.erf(hc * (2.0 ** -0.5)))
            pieces.append(w * gelu.astype(jnp.bfloat16))
        at = jnp.concatenate(pieces, axis=0)
        acc = acc + jnp.dot(vt_ref[:, j * sub:(j + 1) * sub], at, preferred_element_type=jnp.float32)
    yt_scr[...] = acc

    @pl.when(k == pl.num_programs(1) - 1)
    def _():
        z = ALPHA * x_ref[...] + yt_scr[...].T
        o_ref[...] = _layer_norm_rows(z, g_ref[...], b_ref[...])


def peer_ln(x, w_pq, sub_k1, sub_k2, peer_u, peer_v, ln_g, ln_b):
    T = x.shape[0]
    assert T % PEER_TB == 0 and T % PEER_RT == 0
    nt = T // PEER_TB
    half = PEER_DKEY // 2
    wq_t = w_pq.reshape(D_MODEL, PEER_HEADS * PEER_DKEY).T.astype(jnp.bfloat16)
    sshape = jax.ShapeDtypeStruct((PEER_HEADS, PEER_NKEYS, T), jnp.float32)
    sspec = pl.BlockSpec((PEER_HEADS, PEER_NKEYS, PEER_RT), lambda j: (0, 0, j))
    rk2, lim, p1, p2 = pl.pallas_call(
        _peer_route_kernel,
        grid=(T // PEER_RT,),
        in_specs=[pl.BlockSpec((PEER_RT, D_MODEL), lambda j: (j, 0)),
                  pl.BlockSpec((PEER_HEADS * PEER_DKEY, D_MODEL), lambda j: (0, 0)),
                  pl.BlockSpec((PEER_NKEYS, half), lambda j: (0, 0)),
                  pl.BlockSpec((PEER_NKEYS, half), lambda j: (0, 0))],
        out_specs=[sspec, sspec, sspec, sspec],
        out_shape=[jax.ShapeDtypeStruct(sshape.shape, jnp.bfloat16), sshape, sshape,
                   jax.ShapeDtypeStruct(sshape.shape, jnp.bfloat16)],
        compiler_params=pltpu.CompilerParams(dimension_semantics=("arbitrary",), vmem_limit_bytes=VMEM_LIMIT),
        name="peer_route",
    )(x, wq_t, sub_k1.astype(jnp.bfloat16), sub_k2.astype(jnp.bfloat16))

    u_b = peer_u.astype(jnp.bfloat16)
    vt_b = peer_v.T.astype(jnp.bfloat16)
    ne = PEER_N // PEER_EB
    sspec2 = pl.BlockSpec((PEER_HEADS, PEER_NKEYS, PEER_TB), lambda j, k: (0, 0, j))
    return pl.pallas_call(
        _peer_dense_kernel,
        grid=(nt, ne),
        in_specs=[pl.BlockSpec((PEER_TB, D_MODEL), lambda j, k: (j, 0)),
                  sspec2, sspec2, sspec2, sspec2,
                  pl.BlockSpec((PEER_EB, D_MODEL), lambda j, k: (k, 0)),
                  pl.BlockSpec((D_MODEL, PEER_EB), lambda j, k: (0, k)),
                  pl.BlockSpec((1, D_MODEL), lambda j, k: (0, 0)),
                  pl.BlockSpec((1, D_MODEL), lambda j, k: (0, 0))],
        out_specs=pl.BlockSpec((PEER_TB, D_MODEL), lambda j, k: (j, 0)),
        out_shape=jax.ShapeDtypeStruct((T, D_MODEL), jnp.float32),
        scratch_shapes=[pltpu.VMEM((PEER_TB, D_MODEL), jnp.bfloat16),
                        pltpu.VMEM((D_MODEL, PEER_TB), jnp.float32)],
        compiler_params=pltpu.CompilerParams(dimension_semantics=("arbitrary", "arbitrary"),
                                             vmem_limit_bytes=VMEM_LIMIT),
        name="peer_dense",
    )(x, rk2, lim, p1, p2, u_b, vt_b, ln_g.reshape(1, -1), ln_b.reshape(1, -1))


def kernel(x_prompt, x_sample, cache_kv_latent, cache_k_rope, state_C, state_n, state_m,
           cache_mem_k, cache_mem_v, page_table, mem_prompt, ln0_g, ln0_b, w_in, b_i, b_f,
           g_q, w_uq, g_kv, w_uk, w_uv, w_out, ln1_g, ln1_b, w_mq, w_mk, w_mv, w_mo,
           ln2_g, ln2_b, w_pq, sub_k1, sub_k2, peer_u, peer_v, ln3_g, ln3_b):
    B, S = x_prompt.shape[:2]
    NB, TQ = x_sample.shape[:2]
    past = page_table.shape[1] * PAGE_SIZE
    n_p = B * S
    l = 0
    x_all = jnp.concatenate([x_prompt.reshape(n_p, D_MODEL), x_sample.reshape(NB * TQ, D_MODEL)], axis=0)
    pos = jnp.concatenate([jnp.tile(jnp.arange(S), B), jnp.tile(past + jnp.arange(TQ), NB)])
    (xn, q_lat, q_rope, kv, kv_b, kr, kr_t, mq, mk, mv, gates, o_gate) = mix_in(
        x_all, pos, ln0_g, ln0_b, w_in[l], b_i[l], b_f[l], g_q[l], w_uq[l], g_kv[l], w_uk[l])

    mla_p = mla_attend_prompt(q_lat, q_rope, kv_b, kr_t, w_uv[l], B, S)
    mlh_p, C_p, n_pst, m_p = mlstm_prompt(mq, mk, mv, gates, B, S)

    kv_s = kv[n_p:].reshape(NB, TQ, MLA_KV_LORA)
    kr_s = kr[n_p:, :MLA_ROPE].reshape(NB, TQ, MLA_ROPE)
    mla_s = mla_attend_sample(q_lat[n_p:].reshape(NB, TQ, -1), q_rope[n_p:].reshape(NB, TQ, -1), kv_s, kr_s,
                              cache_kv_latent.reshape(cache_kv_latent.shape[1:]),
                              cache_k_rope.reshape(cache_k_rope.shape[1:]), page_table, w_uv[l])
    mlh_s, C_s, n_s, m_s = mlstm_step(mq[n_p:], mk[n_p:], mv[n_p:], gates[n_p:], state_C.reshape(state_C.shape[1:]),
                                      state_n.reshape(state_n.shape[1:]), state_m.reshape(state_m.shape[1:]), NB, TQ)

    mla_all = jnp.concatenate([mla_p, mla_s], axis=0)
    mlh_all = jnp.concatenate([mlh_p, mlh_s], axis=0)
    x1, qm = mix_out(xn, mla_all, mlh_all, o_gate, w_out[l], ln1_g[l], ln1_b[l], w_mq[l])
    mk_p, mv_p = mem_kv(mem_prompt, w_mk[l], w_mv[l])
    x2_p = mem_attend_ln(x1, qm, mk_p.reshape(B, N_MEM, D_MODEL), mv_p.reshape(B, N_MEM, D_MODEL),
                         w_mo[l], ln2_g[l], ln2_b[l], B, S)
    x2_s = mem_attend_ln(x1[n_p:], qm[n_p:], cache_mem_k.reshape(NB, N_MEM, D_MODEL),
                         cache_mem_v.reshape(NB, N_MEM, D_MODEL), w_mo[l], ln2_g[l], ln2_b[l], NB, TQ)
    x2 = jnp.concatenate([x2_p, x2_s], axis=0)
    x3 = peer_ln(x2, w_pq[l], sub_k1[l], sub_k2[l], peer_u[l], peer_v[l], ln3_g[l], ln3_b[l])
    st = lambda t: t[None]
    return (x3[:n_p].reshape(B, S, D_MODEL), x3[n_p:].reshape(NB, TQ, D_MODEL),
            st(kv[:n_p].reshape(B, S, MLA_KV_LORA)), st(kr[:n_p, :MLA_ROPE].reshape(B, S, MLA_ROPE)),
            st(C_p), st(n_pst), st(m_p), st(mk_p), st(mv_p),
            st(kv_s), st(kr_s), st(C_s), st(n_s), st(m_s))
```

```python
import jax, jax.numpy as jnp
from jax import lax
import numpy as np
from jax.experimental import pallas as pl
from jax.experimental.pallas import tpu as pltpu

D_MODEL = 1024
PAGE_SIZE = 128

MLA_HEADS = 8
MLA_NOPE = 64
MLA_ROPE = 32
MLA_V = 64
MLA_KV_LORA = 256
MLA_Q_LORA = 384
MLA_SCALE = (MLA_NOPE + MLA_ROPE) ** -0.5
ROPE_BASE = 10000.0
ML_HEADS = 4
ML_DH = 128
ML_CHUNK = 64
MLA_WIDTH = MLA_HEADS * MLA_V
ML_WIDTH = ML_HEADS * ML_DH
D_MIX = MLA_WIDTH + ML_WIDTH
N_MEM = 256
MEM_HEADS = 4
MEM_HD = D_MODEL // MEM_HEADS
PEER_HEADS = 8
PEER_NKEYS = 128
PEER_N = PEER_NKEYS * PEER_NKEYS
PEER_DKEY = 128
PEER_TOPK = 16
LN_EPS = 1e-5
RMS_EPS = 1e-6
DEPTH = 1
ALPHA = (2 * DEPTH) ** 0.25
NEG_INF = float('-inf')

VMEM_LIMIT = 48 * 1024 * 1024

IN_PAD = 2944
OFF_CQ, OFF_CKV, OFF_KR, OFF_MQ, OFF_MK, OFF_MV, OFF_G, OFF_O = 0, 384, 640, 768, 1280, 1792, 2304, 2432
MIX_TM = 256


def _layer_norm_rows(z, g, b):
    mu = jnp.mean(z, axis=-1, keepdims=True)
    zc = z - mu
    var = jnp.mean(zc * zc, axis=-1, keepdims=True)
    return zc * lax.rsqrt(var + LN_EPS) * g + b


def _rope_lanes(x, cos, sin_signed):
    n = x.shape[1]
    lane = lax.broadcasted_iota(jnp.int32, x.shape, 1)
    partner = jnp.where((lane & 31) < 16, pltpu.roll(x, n - 16, axis=1), pltpu.roll(x, 16, axis=1))
    return x * cos + partner * sin_signed


def _mix_in_kernel(x_ref, g0_ref, b0_ref, win_ref, gq_ref, wuq_ref, wuk_ref, gkv_ref, cos_ref, sin_ref, gb_ref,
                   xn_ref, ql_ref, qr_ref, kv_ref, kvk_ref, kr_ref, mq_ref, mk_ref, mv_ref, gate_ref, og_ref):
    xn = _layer_norm_rows(x_ref[...], g0_ref[...], b0_ref[...])
    xn_ref[...] = xn
    z = jnp.dot(xn.astype(jnp.bfloat16), win_ref[...], preferred_element_type=jnp.float32)
    cos = cos_ref[...]
    sin = sin_ref[...]
    cq = z[:, OFF_CQ:OFF_CQ + MLA_Q_LORA]
    cq = cq * lax.rsqrt(jnp.mean(cq * cq, axis=-1, keepdims=True) + RMS_EPS) * gq_ref[...]
    q = jnp.dot(cq.astype(jnp.bfloat16), wuq_ref[...], preferred_element_type=jnp.float32)
    n_nope = MLA_HEADS * MLA_NOPE
    qrope = _rope_lanes(q[:, n_nope:], jnp.concatenate([cos, cos], axis=1), jnp.concatenate([sin, sin], axis=1))
    qr_ref[...] = qrope.astype(jnp.bfloat16)
    ql_ref[...] = jnp.dot(q[:, :n_nope].astype(jnp.bfloat16), wuk_ref[...],
                          preferred_element_type=jnp.float32).astype(jnp.bfloat16)
    ckv = z[:, OFF_CKV:OFF_CKV + MLA_KV_LORA]
    kv = ckv * lax.rsqrt(jnp.mean(ckv * ckv, axis=-1, keepdims=True) + RMS_EPS) * gkv_ref[...]
    kv_ref[...] = kv
    kr = _rope_lanes(z[:, OFF_KR:OFF_KR + 128], cos, sin)
    kr_ref[...] = kr
    krt = kr + pltpu.roll(kr, 32, axis=1) + pltpu.roll(kr, 64, axis=1) + pltpu.roll(kr, 96, axis=1)
    kvk_ref[...] = jnp.concatenate([kv, krt, krt], axis=1).astype(jnp.bfloat16)
    mq_ref[...] = z[:, OFF_MQ:OFF_MQ + ML_WIDTH]
    mk_ref[...] = z[:, OFF_MK:OFF_MK + ML_WIDTH] * (ML_DH ** -0.5)
    mv_ref[...] = z[:, OFF_MV:OFF_MV + ML_WIDTH]
    g = z[:, OFF_G:OFF_G + 128] + gb_ref[...]
    lane = lax.broadcasted_iota(jnp.int32, g.shape, 1)
    gate_ref[...] = jnp.where(lane < ML_HEADS, g, jax.nn.log_sigmoid(g))
    og_ref[...] = jax.nn.sigmoid(z[:, OFF_O:OFF_O + ML_WIDTH])


def mix_in(x, pos, ln0_g, ln0_b, w_in, b_i, b_f, g_q, w_uq, g_kv, w_uk):
    T = x.shape[0]
    f32, bf = jnp.float32, jnp.bfloat16
    zc = lambda n: jnp.zeros((D_MODEL, n), f32)
    win_p = jnp.concatenate([w_in[:, :672], zc(96), w_in[:, 672:2208], w_in[:, 2208:2216], zc(120), w_in[:, 2216:]],
                            axis=1).astype(bf)
    assert win_p.shape[1] == IN_PAD
    wuq_p = jnp.concatenate([w_uq[:, :, :MLA_NOPE].reshape(MLA_Q_LORA, -1),
                             w_uq[:, :, MLA_NOPE:].reshape(MLA_Q_LORA, -1)], axis=1).astype(bf)
    hh = jnp.arange(MLA_HEADS)
    wuk_blk = jnp.zeros((MLA_HEADS, MLA_NOPE, MLA_HEADS, MLA_KV_LORA), f32)
    wuk_blk = wuk_blk.at[hh, :, hh, :].set(jnp.transpose(w_uk, (1, 2, 0)))
    wuk_blk = wuk_blk.reshape(MLA_HEADS * MLA_NOPE, MLA_HEADS * MLA_KV_LORA).astype(bf)
    inv = 1.0 / (ROPE_BASE ** (jnp.arange(0, MLA_ROPE, 2, dtype=f32) / MLA_ROPE))
    ang = pos.astype(f32)[:, None] * inv[None, :]
    c, s = jnp.cos(ang), jnp.sin(ang)
    cos128 = jnp.tile(jnp.concatenate([c, c], axis=1), (1, 4))
    sin128 = jnp.tile(jnp.concatenate([-s, s], axis=1), (1, 4))
    gbias = jnp.concatenate([b_i, b_f, jnp.zeros((120,), f32)]).reshape(1, 128)
    row = lambda n: pl.BlockSpec((MIX_TM, n), lambda i: (i, 0))
    full = lambda a: pl.BlockSpec(a.shape, lambda i: (0,) * a.ndim)
    ins = [x, ln0_g.reshape(1, -1), ln0_b.reshape(1, -1), win_p, g_q.reshape(1, -1), wuq_p, wuk_blk,
           g_kv.reshape(1, -1), cos128, sin128, gbias]
    in_specs = [row(D_MODEL)] + [full(a) for a in ins[1:8]] + [row(128), row(128), full(gbias)]
    outs = [(D_MODEL, f32), (MLA_HEADS * MLA_KV_LORA, bf), (MLA_HEADS * MLA_ROPE, bf), (MLA_KV_LORA, f32),
            (MLA_KV_LORA + MLA_HEADS * MLA_ROPE, bf), (128, f32), (ML_WIDTH, f32), (ML_WIDTH, f32),
            (ML_WIDTH, f32), (128, f32), (ML_WIDTH, f32)]
    return pl.pallas_call(
        _mix_in_kernel,
        grid=(T // MIX_TM,),
        in_specs=in_specs,
        out_specs=[row(n) for n, _ in outs],
        out_shape=[jax.ShapeDtypeStruct((T, n), dt) for n, dt in outs],
        compiler_params=pltpu.CompilerParams(dimension_semantics=("arbitrary",), vmem_limit_bytes=VMEM_LIMIT),
        name="mix_in",
    )(*ins)


ATT_BQ = 128
ATT_BK = 512


def _mla_prompt_kernel(ql_ref, qr_ref, kvk_ref, wuv_ref, o_ref, acc_scr, m_scr, l_scr):
    qi = pl.program_id(1)
    ql = jnp.concatenate([ql_ref[:, h * MLA_KV_LORA:(h + 1) * MLA_KV_LORA] for h in range(MLA_HEADS)], axis=0)
    qr_all = qr_ref[...]
    lane_head = lax.broadcasted_iota(jnp.int32, qr_all.shape, 1) // MLA_ROPE
    qr = jnp.concatenate([jnp.where(lane_head == h, qr_all, jnp.zeros_like(qr_all)) for h in range(MLA_HEADS)],
                         axis=0)
    q = jnp.concatenate([ql, qr], axis=1)
    acc_scr[...] = jnp.zeros_like(acc_scr)
    m_scr[...] = jnp.full_like(m_scr, NEG_INF)
    l_scr[...] = jnp.zeros_like(l_scr)
    nt = (((1,), (1,)), ((), ()))

    def step(kj, masked):
        k0 = pl.multiple_of(kj * ATT_BK, ATT_BK)
        kk = kvk_ref[pl.ds(k0, ATT_BK), :]
        kvb = kk[:, :MLA_KV_LORA]
        s = lax.dot_general(q, kk, nt, preferred_element_type=jnp.float32) * MLA_SCALE
        if masked:
            col = k0 + lax.broadcasted_iota(jnp.int32, s.shape, 1)
            tok = qi * ATT_BQ + (lax.broadcasted_iota(jnp.int32, s.shape, 0) & (ATT_BQ - 1))
            s = jnp.where(col <= tok, s, NEG_INF)
        m_old = m_scr[...]
        m_new = jnp.maximum(m_old, jnp.max(s, axis=1, keepdims=True))
        alpha = jnp.exp(m_old - m_new)
        p = jnp.exp(s - m_new)
        l_scr[...] = alpha * l_scr[...] + jnp.sum(p, axis=1, keepdims=True)
        acc_scr[...] = alpha * acc_scr[...] + jnp.dot(p.astype(jnp.bfloat16), kvb,
                                                      preferred_element_type=jnp.float32)
        m_scr[...] = m_new

    n_full = (qi * ATT_BQ) // ATT_BK

    def body(kj, c):
        step(kj, False)
        return c

    lax.fori_loop(0, n_full, body, 0)
    step(n_full, True)
    o = (acc_scr[...] / l_scr[...]).astype(jnp.bfloat16)
    out = jnp.zeros((ATT_BQ, MLA_WIDTH), jnp.float32)
    for h in range(MLA_HEADS):
        out = out + jnp.dot(o[h * ATT_BQ:(h + 1) * ATT_BQ], wuv_ref[h], preferred_element_type=jnp.float32)
    o_ref[...] = out


def mla_attend_prompt(q_lat, q_rope, kvk, w_uv, n_seq, seq):
    assert seq % ATT_BK == 0 and ATT_BK % ATT_BQ == 0
    H, C = MLA_HEADS, MLA_KV_LORA
    nq = seq // ATT_BQ
    hh = jnp.arange(H)
    wpad = jnp.zeros((H, C, H, MLA_V), jnp.float32).at[hh, :, hh, :].set(jnp.transpose(w_uv, (1, 0, 2)))
    wpad = wpad.reshape(H, C, H * MLA_V).astype(jnp.bfloat16)
    rows = H * ATT_BQ
    return pl.pallas_call(
        _mla_prompt_kernel,
        grid=(n_seq, nq),
        in_specs=[pl.BlockSpec((ATT_BQ, H * C), lambda b, i: (b * nq + i, 0)),
                  pl.BlockSpec((ATT_BQ, H * MLA_ROPE), lambda b, i: (b * nq + i, 0)),
                  pl.BlockSpec((seq, C + H * MLA_ROPE), lambda b, i: (b, 0)),
                  pl.BlockSpec((H, C, H * MLA_V), lambda b, i: (0, 0, 0))],
        out_specs=pl.BlockSpec((ATT_BQ, H * MLA_V), lambda b, i: (b * nq + i, 0)),
        out_shape=jax.ShapeDtypeStruct((n_seq * seq, H * MLA_V), jnp.float32),
        scratch_shapes=[pltpu.VMEM((rows, C), jnp.float32),
                        pltpu.VMEM((rows, 1), jnp.float32),
                        pltpu.VMEM((rows, 1), jnp.float32)],
        compiler_params=pltpu.CompilerParams(dimension_semantics=("arbitrary", "arbitrary"),
                                             vmem_limit_bytes=VMEM_LIMIT),
        name="mla_prompt",
    )(q_lat, q_rope, kvk, wpad)


SMP_KC = 2048


def _mla_sample_kernel(pt_ref, ql_ref, qr_ref, kvn_ref, krn_ref, wuv_ref, lat_hbm, rope_hbm, o_ref,
                       lat_buf, rope_buf, lat_bf, sem_lat, sem_rope):
    b = pl.program_id(0)
    nb = pl.num_programs(0)
    n_pages = pt_ref.shape[1]
    rows = ql_ref.shape[0]
    tq = kvn_ref.shape[0]
    n_keys = n_pages * PAGE_SIZE

    def page_copies(seq, slot, p):
        page = pt_ref[seq, p]
        dst = pl.ds(p * PAGE_SIZE, PAGE_SIZE)
        return (pltpu.make_async_copy(lat_hbm.at[page], lat_buf.at[slot, dst], sem_lat.at[slot]),
                pltpu.make_async_copy(rope_hbm.at[page], rope_buf.at[slot, :, dst], sem_rope.at[slot]))

    def start_fetch(seq, slot):
        def body(p, c):
            for cp in page_copies(seq, slot, p):
                cp.start()
            return c
        lax.fori_loop(0, n_pages, body, 0)

    def wait_fetch(seq, slot):
        def body(p, c):
            for cp in page_copies(seq, slot, p):
                cp.wait()
            return c
        lax.fori_loop(0, n_pages, body, 0)

    slot = b % 2

    @pl.when(b == 0)
    def _():
        start_fetch(0, 0)

    @pl.when(b + 1 < nb)
    def _():
        start_fetch(b + 1, 1 - slot)

    wait_fetch(b, slot)

    bf, f32 = jnp.bfloat16, jnp.float32
    ql = ql_ref[...]
    qr = qr_ref[...]
    nt = (((1,), (1,)), ((), ()))
    parts = []
    for c in range(n_keys // SMP_KC):
        r = pl.ds(c * SMP_KC, SMP_KC)
        lb = lat_buf[slot, r, :].astype(bf)
        rb = rope_buf[slot, :, r].astype(bf)
        lat_bf[r, :] = lb
        parts.append(lax.dot_general(ql, lb, nt, preferred_element_type=f32)
                     + jnp.dot(qr, rb, preferred_element_type=f32))
    s_past = jnp.concatenate(parts, axis=1) * MLA_SCALE
    qlf, qrf = ql.astype(f32), qr.astype(f32)
    kvn = kvn_ref[...].astype(f32)
    krn = krn_ref[...].astype(f32)
    tok = lax.broadcasted_iota(jnp.int32, (rows, 1), 0) % tq
    s_new = []
    for j in range(tq):
        sj = (jnp.sum(qlf * kvn[j:j + 1, :], axis=1, keepdims=True)
              + jnp.sum(qrf * krn[j:j + 1, :], axis=1, keepdims=True)) * MLA_SCALE
        s_new.append(jnp.where(tok >= j, sj, NEG_INF))
    m = jnp.max(s_past, axis=1, keepdims=True)
    for sj in s_new:
        m = jnp.maximum(m, sj)
    p_past = jnp.exp(s_past - m)
    p_new = [jnp.exp(sj - m) for sj in s_new]
    l = jnp.sum(p_past, axis=1, keepdims=True)
    for pj in p_new:
        l = l + pj
    inv = 1.0 / l
    o = jnp.dot((p_past * inv).astype(bf), lat_bf[...], preferred_element_type=f32)
    for j in range(tq):
        o = o + (p_new[j] * inv).astype(bf).astype(f32) * kvn[j:j + 1, :]
    ob = o.astype(bf)
    out = jnp.zeros((tq, MLA_WIDTH), f32)
    for h in range(MLA_HEADS):
        out = out + jnp.dot(ob, wuv_ref[h], preferred_element_type=f32)[h * tq:(h + 1) * tq, :]
    o_ref[...] = out


def mla_attend_sample(q_lat, q_rope, kv_new, kr_new, pool_lat, pool_rope_t, page_table, w_uv):
    NB, T = q_lat.shape[:2]
    H, C, R = MLA_HEADS, MLA_KV_LORA, MLA_ROPE
    bf = jnp.bfloat16
    n_pages = page_table.shape[1]
    n_keys = n_pages * PAGE_SIZE
    assert n_keys % SMP_KC == 0
    ql = q_lat.reshape(NB, T, H, C).transpose(0, 2, 1, 3).reshape(NB, H * T, C)
    qr = q_rope.reshape(NB, T, H, R).transpose(0, 2, 1, 3).reshape(NB, H * T, R)
    hh = jnp.arange(H)
    wpad = jnp.zeros((H, C, H, MLA_V), jnp.float32).at[hh, :, hh, :].set(jnp.transpose(w_uv, (1, 0, 2)))
    wpad = wpad.reshape(H, C, H * MLA_V).astype(bf)
    seq = lambda n, w: pl.BlockSpec((None, n, w), lambda b, pt: (b, 0, 0))
    grid_spec = pltpu.PrefetchScalarGridSpec(
        num_scalar_prefetch=1,
        grid=(NB,),
        in_specs=[seq(H * T, C), seq(H * T, R), seq(T, C), seq(T, R),
                  pl.BlockSpec((H, C, H * MLA_V), lambda b, pt: (0, 0, 0)),
                  pl.BlockSpec(memory_space=pl.ANY), pl.BlockSpec(memory_space=pl.ANY)],
        out_specs=seq(T, H * MLA_V),
        scratch_shapes=[pltpu.VMEM((2, n_keys, C), jnp.float32), pltpu.VMEM((2, R, n_keys), jnp.float32),
                        pltpu.VMEM((n_keys, C), bf),
                        pltpu.SemaphoreType.DMA((2,)), pltpu.SemaphoreType.DMA((2,))],
    )
    out = pl.pallas_call(
        _mla_sample_kernel,
        grid_spec=grid_spec,
        out_shape=jax.ShapeDtypeStruct((NB, T, H * MLA_V), jnp.float32),
        compiler_params=pltpu.CompilerParams(dimension_semantics=("arbitrary",), vmem_limit_bytes=VMEM_LIMIT),
        name="mla_sample",
    )(page_table, ql, qr, kv_new.astype(bf), kr_new.astype(bf), wpad, pool_lat, pool_rope_t)
    return out.reshape(NB * T, H * MLA_V)


def _split3(x):
    hi = x.astype(jnp.bfloat16)
    r1 = x - hi.astype(jnp.float32)
    mid = r1.astype(jnp.bfloat16)
    lo = (r1 - mid.astype(jnp.float32)).astype(jnp.bfloat16)
    return hi, mid, lo


def _mlstm_chunk(q_all, k_all, v_all, gates, c_refs, n_refs, m_refs):
    L = q_all.shape[0]
    bf, f32 = jnp.bfloat16, jnp.float32
    row_t = lax.broadcasted_iota(jnp.int32, (L, L), 0)
    col_s = lax.broadcasted_iota(jnp.int32, (L, L), 1)
    causal = col_s <= row_t
    tril = jnp.where(causal, 1.0, 0.0).astype(bf)
    cum = sum(jnp.dot(tril, part, preferred_element_type=f32) for part in _split3(gates))
    gates_t = gates.T
    cum_t = cum.T
    nt = (((1,), (1,)), ((), ()))
    outs = []
    for h in range(ML_HEADS):
        c = slice(h * ML_DH, (h + 1) * ML_DH)
        q, k, v = q_all[:, c], k_all[:, c], v_all[:, c]
        C, n, m = c_refs[0](h), n_refs[0](h), m_refs[0](h)
        ig_col = gates[:, h:h + 1]
        b_col = cum[:, ML_HEADS + h:ML_HEADS + h + 1]
        ig_row = gates_t[h:h + 1, :]
        b_row = cum_t[ML_HEADS + h:ML_HEADS + h + 1, :]
        D = jnp.where(causal, b_col - b_row + ig_row, NEG_INF)
        inter = b_col + m
        m_t = jnp.maximum(inter, jnp.max(D, axis=1, keepdims=True))
        qb, kb, vb = q.astype(bf), k.astype(bf), v.astype(bf)
        A = jnp.exp(D - m_t) * lax.dot_general(qb, kb, nt, preferred_element_type=f32)
        w_inter = jnp.exp(inter - m_t)
        num = w_inter * jnp.dot(qb, C.astype(bf), preferred_element_type=f32) \
            + jnp.dot(A.astype(bf), vb, preferred_element_type=f32)
        qn = jnp.sum(qb.astype(f32) * n.astype(bf).astype(f32), axis=1, keepdims=True)
        den = w_inter * qn + jnp.sum(A, axis=1, keepdims=True)
        outs.append(num / jnp.maximum(jnp.abs(den), jnp.exp(-m_t)))
        b_end = b_col[L - 1:L, :]
        m_new = jnp.maximum(b_end + m, jnp.max(b_end - b_row + ig_row, axis=1, keepdims=True))
        a_prev = jnp.exp(b_end + m - m_new)
        kw = k * jnp.exp(b_end - b_col + ig_col - m_new)
        c_refs[1](h, a_prev * C + jnp.dot(kw.T.astype(bf), vb, preferred_element_type=f32))
        n_refs[1](h, a_prev * n + jnp.sum(kw, axis=0, keepdims=True))
        m_refs[1](h, m_new)
    return jnp.concatenate(outs, axis=1)


def _state_access(c_get, n_get, m_get, c_set, n_set, m_set):
    c_refs = (lambda h: c_get[h], lambda h, val: c_set.__setitem__(h, val))
    n_refs = (lambda h: n_get[h:h + 1, :], lambda h, val: n_set.__setitem__((slice(h, h + 1), slice(None)), val))
    m_refs = (lambda h: m_get[h:h + 1, 0:1],
              lambda h, val: m_set.__setitem__((slice(h, h + 1), slice(None)), jnp.broadcast_to(val, (1, 128))))
    return c_refs, n_refs, m_refs


def _mlstm_prompt_kernel(q_ref, k_ref, v_ref, g_ref, h_ref, c_out, n_out, m_out, c_scr, n_scr, m_scr):
    j = pl.program_id(1)

    @pl.when(j == 0)
    def _():
        c_scr[...] = jnp.zeros_like(c_scr)
        n_scr[...] = jnp.zeros_like(n_scr)
        m_scr[...] = jnp.zeros_like(m_scr)

    h_ref[...] = _mlstm_chunk(q_ref[...], k_ref[...], v_ref[...], g_ref[...],
                              *_state_access(c_scr, n_scr, m_scr, c_scr, n_scr, m_scr))

    @pl.when(j == pl.num_programs(1) - 1)
    def _():
        c_out[...] = c_scr[...]
        n_out[...] = n_scr[0:ML_HEADS, :]
        m_out[...] = m_scr[0:ML_HEADS, :]


def mlstm_prompt(mq, mk, mv, gates, n_seq, seq):
    nc = seq // ML_CHUNK
    f32 = jnp.float32
    tok = lambda w: pl.BlockSpec((ML_CHUNK, w), lambda b, j: (b * nc + j, 0))
    h, C, n, m = pl.pallas_call(
        _mlstm_prompt_kernel,
        grid=(n_seq, nc),
        in_specs=[tok(ML_WIDTH), tok(ML_WIDTH), tok(ML_WIDTH), tok(128)],
        out_specs=[tok(ML_WIDTH),
                   pl.BlockSpec((None, ML_HEADS, ML_DH, ML_DH), lambda b, j: (b, 0, 0, 0)),
                   pl.BlockSpec((None, ML_HEADS, ML_DH), lambda b, j: (b, 0, 0)),
                   pl.BlockSpec((None, ML_HEADS, 128), lambda b, j: (b, 0, 0))],
        out_shape=[jax.ShapeDtypeStruct((n_seq * seq, ML_WIDTH), f32),
                   jax.ShapeDtypeStruct((n_seq, ML_HEADS, ML_DH, ML_DH), f32),
                   jax.ShapeDtypeStruct((n_seq, ML_HEADS, ML_DH), f32),
                   jax.ShapeDtypeStruct((n_seq, ML_HEADS, 128), f32)],
        scratch_shapes=[pltpu.VMEM((ML_HEADS, ML_DH, ML_DH), f32), pltpu.VMEM((8, ML_DH), f32),
                        pltpu.VMEM((8, 128), f32)],
        compiler_params=pltpu.CompilerParams(dimension_semantics=("arbitrary", "arbitrary"),
                                             vmem_limit_bytes=VMEM_LIMIT),
        name="mlstm_prompt",
    )(mq, mk, mv, gates)
    return h, C, n, m[:, :, 0]


def _mlstm_step_kernel(q_ref, k_ref, v_ref, g_ref, c_in, n_in, m_in, h_ref, c_out, n_out, m_out):
    h_ref[...] = _mlstm_chunk(q_ref[...], k_ref[...], v_ref[...], g_ref[...],
                              *_state_access(c_in, n_in, m_in, c_out, n_out, m_out))


def mlstm_step(mq, mk, mv, gates, state_C, state_n, state_m, n_seq, rows):
    f32 = jnp.float32
    tok = lambda w: pl.BlockSpec((None, rows, w), lambda b: (b, 0, 0))
    st_c = pl.BlockSpec((None, ML_HEADS, ML_DH, ML_DH), lambda b: (b, 0, 0, 0))
    st_n = pl.BlockSpec((None, ML_HEADS, ML_DH), lambda b: (b, 0, 0))
    st_m = pl.BlockSpec((None, ML_HEADS, 128), lambda b: (b, 0, 0))
    r3 = lambda t: t.reshape(n_seq, rows, t.shape[-1])
    m_in = jnp.broadcast_to(state_m[:, :, None], (n_seq, ML_HEADS, 128))
    h, C, n, m = pl.pallas_call(
        _mlstm_step_kernel,
        grid=(n_seq,),
        in_specs=[tok(ML_WIDTH), tok(ML_WIDTH), tok(ML_WIDTH), tok(128), st_c, st_n, st_m],
        out_specs=[tok(ML_WIDTH), st_c, st_n, st_m],
        out_shape=[jax.ShapeDtypeStruct((n_seq, rows, ML_WIDTH), f32),
                   jax.ShapeDtypeStruct((n_seq, ML_HEADS, ML_DH, ML_DH), f32),
                   jax.ShapeDtypeStruct((n_seq, ML_HEADS, ML_DH), f32),
                   jax.ShapeDtypeStruct((n_seq, ML_HEADS, 128), f32)],
        compiler_params=pltpu.CompilerParams(dimension_semantics=("arbitrary",), vmem_limit_bytes=VMEM_LIMIT),
        name="mlstm_step",
    )(r3(mq), r3(mk), r3(mv), r3(gates), state_C, state_n, m_in)
    return h.reshape(n_seq * rows, ML_WIDTH), C, n, m[:, :, 0]


POST_TM = 256


def _mix_out_kernel(xn_ref, mla_ref, mlh_ref, og_ref, wout_ref, g_ref, b_ref, wmq_ref, x1_ref, qm_ref):
    mixed = jnp.concatenate([mla_ref[...], og_ref[...] * mlh_ref[...]], axis=1).astype(jnp.bfloat16)
    mix = jnp.dot(mixed, wout_ref[...], preferred_element_type=jnp.float32)
    x1 = _layer_norm_rows(ALPHA * xn_ref[...] + mix, g_ref[...], b_ref[...])
    x1_ref[...] = x1
    qm_ref[...] = jnp.dot(x1.astype(jnp.bfloat16), wmq_ref[...],
                          preferred_element_type=jnp.float32).astype(jnp.bfloat16)


def mix_out(xn, mla_o, ml_h, o_gate, w_out, ln_g, ln_b, w_mq):
    T = xn.shape[0]
    bf = jnp.bfloat16
    row = lambda n: pl.BlockSpec((POST_TM, n), lambda i: (i, 0))
    full = lambda shape: pl.BlockSpec(shape, lambda i: (0,) * len(shape))
    return pl.pallas_call(
        _mix_out_kernel,
        grid=(T // POST_TM,),
        in_specs=[row(D_MODEL), row(MLA_WIDTH), row(ML_WIDTH), row(ML_WIDTH), full((D_MIX, D_MODEL)),
                  full((1, D_MODEL)), full((1, D_MODEL)), full((D_MODEL, D_MODEL))],
        out_specs=[row(D_MODEL), row(D_MODEL)],
        out_shape=[jax.ShapeDtypeStruct((T, D_MODEL), jnp.float32), jax.ShapeDtypeStruct((T, D_MODEL), bf)],
        compiler_params=pltpu.CompilerParams(dimension_semantics=("arbitrary",), vmem_limit_bytes=VMEM_LIMIT),
        name="mix_out",
    )(xn, mla_o, ml_h, o_gate, w_out.astype(bf), ln_g.reshape(1, -1), ln_b.reshape(1, -1),
      w_mq.reshape(D_MODEL, D_MODEL).astype(bf))


def _mem_attn_kernel(x1_ref, qm_ref, mk_ref, mv_ref, wmo_ref, g_ref, b_ref, x2_ref):
    q = qm_ref[...]
    nt = (((1,), (1,)), ((), ()))
    outs = []
    for h in range(MEM_HEADS):
        c = slice(h * MEM_HD, (h + 1) * MEM_HD)
        mk, mv = mk_ref[:, c].astype(jnp.bfloat16), mv_ref[:, c].astype(jnp.bfloat16)
        s = lax.dot_general(q[:, c], mk, nt, preferred_element_type=jnp.float32) * (MEM_HD ** -0.5)
        m = jnp.max(s, axis=-1, keepdims=True)
        p = jnp.exp(s - m)
        p = p / jnp.sum(p, axis=-1, keepdims=True)
        outs.append(jnp.dot(p.astype(jnp.bfloat16), mv, preferred_element_type=jnp.float32))
    o = jnp.concatenate(outs, axis=1).astype(jnp.bfloat16)
    att = jnp.dot(o, wmo_ref[...], preferred_element_type=jnp.float32)
    x2_ref[...] = _layer_norm_rows(ALPHA * x1_ref[...] + att, g_ref[...], b_ref[...])


def _mem_attn_sample_kernel(x1_ref, qm_ref, mk_ref, mv_ref, wmo_ref, g_ref, b_ref, x2_ref):
    tq = x1_ref.shape[0]
    bf, f32 = jnp.bfloat16, jnp.float32
    q = qm_ref[...]
    qs = jnp.concatenate([q[:, h * MEM_HD:(h + 1) * MEM_HD] for h in range(MEM_HEADS)], axis=0)
    kall = mk_ref[...].reshape(N_MEM * MEM_HEADS, MEM_HD).astype(bf)
    vall = mv_ref[...].reshape(N_MEM * MEM_HEADS, MEM_HD).astype(bf)
    s = lax.dot_general(qs, kall, (((1,), (1,)), ((), ())), preferred_element_type=f32) * (MEM_HD ** -0.5)
    row_h = lax.broadcasted_iota(jnp.int32, s.shape, 0) // tq
    col_h = lax.broadcasted_iota(jnp.int32, s.shape, 1) % MEM_HEADS
    s = jnp.where(row_h == col_h, s, NEG_INF)
    p = jnp.exp(s - jnp.max(s, axis=-1, keepdims=True))
    p = p / jnp.sum(p, axis=-1, keepdims=True)
    o = jnp.dot(p.astype(bf), vall, preferred_element_type=f32)
    o = jnp.concatenate([o[h * tq:(h + 1) * tq] for h in range(MEM_HEADS)], axis=1).astype(bf)
    att = jnp.dot(o, wmo_ref[...], preferred_element_type=f32)
    x2_ref[...] = _layer_norm_rows(ALPHA * x1_ref[...] + att, g_ref[...], b_ref[...])


def mem_attend_ln(x1, qm, mem_k, mem_v, w_mo, ln_g, ln_b, n_seq, rows_per_seq):
    bf = jnp.bfloat16
    wmo = w_mo.reshape(D_MODEL, D_MODEL).astype(bf)
    g, b = ln_g.reshape(1, -1), ln_b.reshape(1, -1)
    cp = pltpu.CompilerParams(dimension_semantics=("arbitrary",) * 2, vmem_limit_bytes=VMEM_LIMIT)
    mem = pl.BlockSpec((None, N_MEM, D_MODEL), lambda s, i: (s, 0, 0))
    full = lambda shape: pl.BlockSpec(shape, lambda s, i: (0,) * len(shape))
    if rows_per_seq % POST_TM == 0:
        nb = rows_per_seq // POST_TM
        tok = pl.BlockSpec((POST_TM, D_MODEL), lambda s, i: (s * nb + i, 0))
        return pl.pallas_call(
            _mem_attn_kernel, grid=(n_seq, nb),
            in_specs=[tok, tok, mem, mem, full((D_MODEL, D_MODEL)), full((1, D_MODEL)), full((1, D_MODEL))],
            out_specs=tok, out_shape=jax.ShapeDtypeStruct((n_seq * rows_per_seq, D_MODEL), jnp.float32),
            compiler_params=cp, name="mem_attn_prompt",
        )(x1, qm, mem_k, mem_v, wmo, g, b)
    mem = pl.BlockSpec((None, N_MEM, MEM_HEADS, MEM_HD), lambda s, i: (s, 0, 0, 0))
    x3 = x1.reshape(n_seq, rows_per_seq, D_MODEL)
    q3 = qm.reshape(n_seq, rows_per_seq, D_MODEL)
    tok = pl.BlockSpec((None, rows_per_seq, D_MODEL), lambda s, i: (s, 0, 0))
    out = pl.pallas_call(
        _mem_attn_sample_kernel, grid=(n_seq, 1),
        in_specs=[tok, tok, mem, mem, full((D_MODEL, D_MODEL)), full((1, D_MODEL)), full((1, D_MODEL))],
        out_specs=tok, out_shape=jax.ShapeDtypeStruct(x3.shape, jnp.float32),
        compiler_params=cp, name="mem_attn_sample",
    )(x3, q3, mem_k, mem_v, wmo, g, b)
    return out.reshape(x1.shape)


def _mem_kv_kernel(m_ref, w_ref, o_ref):
    o_ref[...] = jnp.dot(m_ref[...].astype(jnp.bfloat16), w_ref[...], preferred_element_type=jnp.float32)


def mem_kv(mem, w_mk, w_mv):
    B = mem.shape[0]
    w = jnp.concatenate([w_mk.reshape(D_MODEL, D_MODEL), w_mv.reshape(D_MODEL, D_MODEL)], axis=1).astype(jnp.bfloat16)
    out = pl.pallas_call(
        _mem_kv_kernel, grid=(B,),
        in_specs=[pl.BlockSpec((N_MEM, D_MODEL), lambda i: (i, 0)), pl.BlockSpec((D_MODEL, 2 * D_MODEL), lambda i: (0, 0))],
        out_specs=pl.BlockSpec((N_MEM, 2 * D_MODEL), lambda i: (i, 0)),
        out_shape=jax.ShapeDtypeStruct((B * N_MEM, 2 * D_MODEL), jnp.float32),
        compiler_params=pltpu.CompilerParams(dimension_semantics=("arbitrary",), vmem_limit_bytes=VMEM_LIMIT),
        name="mem_kv",
    )(mem.reshape(B * N_MEM, D_MODEL), w)
    mk = out[:, :D_MODEL].reshape(B, N_MEM, MEM_HEADS, MEM_HD)
    mv = out[:, D_MODEL:].reshape(B, N_MEM, MEM_HEADS, MEM_HD)
    return mk, mv


PEER_RT = 256
PEER_TB = 512
PEER_EB = 1024
PEER_VMEM_LIMIT = 58 * 1024 * 1024


def _top16_rows(s, row_id):
    big = float(2 ** 20)
    out_id = lax.broadcasted_iota(jnp.int32, (PEER_TOPK, s.shape[1]), 0)
    stacked = jnp.zeros((PEER_TOPK, s.shape[1]), jnp.float32)
    rank = jnp.full(s.shape, float(PEER_TOPK), jnp.float32)
    rows, firsts = [], []
    for k in range(PEER_TOPK):
        m = jnp.max(s, axis=0, keepdims=True)
        first = jnp.min(jnp.where(s == m, row_id, big), axis=0, keepdims=True)
        hit = row_id == first
        s = jnp.where(hit, NEG_INF, s)
        rank = jnp.where(hit, float(k), rank)
        rows.append(m)
        firsts.append(first)
        stacked = jnp.where(out_id == k, m, stacked)
    return rows, stacked, firsts, rank, s


def _peer_route_kernel(x_ref, wq_ref, k1_ref, k2_ref, rk2_ref, lim_ref, p1_ref, p2_ref):
    half = PEER_DKEY // 2
    tb = x_ref.shape[0]
    xb = x_ref[...].astype(jnp.bfloat16)
    qt = lax.dot_general(wq_ref[...], xb, (((1,), (1,)), ((), ())), preferred_element_type=jnp.float32)
    row128 = lax.broadcasted_iota(jnp.int32, (PEER_NKEYS, tb), 0).astype(jnp.float32)
    sub8 = lax.broadcasted_iota(jnp.int32, (8, tb), 0)
    sub8f = sub8.astype(jnp.float32)
    for h in range(PEER_HEADS):
        q1 = qt[h * PEER_DKEY:h * PEER_DKEY + half].astype(jnp.bfloat16)
        q2 = qt[h * PEER_DKEY + half:(h + 1) * PEER_DKEY].astype(jnp.bfloat16)
        s1 = jnp.dot(k1_ref[...], q1, preferred_element_type=jnp.float32)
        s2 = jnp.dot(k2_ref[...], q2, preferred_element_type=jnp.float32)
        r1, v1, first1, _, _ = _top16_rows(s1, row128)
        r2, v2, _, rank2, _ = _top16_rows(s2, row128)
        groups, ids = [], []
        for b in range(8):
            lim = PEER_TOPK // (b + 1)
            for a0 in range(0, lim, 8):
                g = v1[a0:a0 + 8] + r2[b]
                if lim - a0 < 8:
                    g = jnp.where(sub8 < lim - a0, g, NEG_INF)
                groups.append(g)
                ids.append((sub8f + float(a0)) * float(PEER_TOPK) + float(b))
        groups.append(r1[0] + v2[8:16])
        ids.append(sub8f + 8.0)
        cand = jnp.concatenate(groups, axis=0)
        vals, _, _, _, left = _top16_rows(cand, jnp.concatenate(ids, axis=0))
        z = jnp.ones_like(vals[0])
        for k in range(1, PEER_TOPK):
            z = z + jnp.exp(vals[k] - vals[0])
        taken = jnp.where((left == NEG_INF) & (cand > NEG_INF), 1.0, 0.0)
        cnt_lo = jnp.zeros((8, tb), jnp.float32)
        gi = 0
        for b in range(8):
            for a0 in range(0, PEER_TOPK // (b + 1), 8):
                if a0 == 0:
                    cnt_lo = cnt_lo + taken[gi * 8:(gi + 1) * 8]
                else:
                    cnt_hi = taken[gi * 8:(gi + 1) * 8]
                gi += 1
        tail = jnp.sum(taken[gi * 8:(gi + 1) * 8], axis=0, keepdims=True)
        cnt_lo = cnt_lo + jnp.where(sub8 == 0, tail, 0.0)
        lim_full = jnp.full((PEER_NKEYS, tb), -1.0, jnp.float32)
        for a in range(PEER_TOPK):
            cnt = cnt_lo if a < 8 else cnt_hi
            lim_full = jnp.where(row128 == first1[a], cnt[a % 8:a % 8 + 1] - 1.0, lim_full)
        rk2_ref[h] = rank2.astype(jnp.bfloat16)
        lim_ref[h] = lim_full
        p1_ref[h] = jnp.exp(s1 - r1[0]) / z
        p2_ref[h] = jnp.exp(s2 - r2[0]).astype(jnp.bfloat16)


def _row_bf16(row):
    r16 = jnp.broadcast_to(row, (16, row.shape[1])).astype(jnp.bfloat16)
    return jnp.concatenate([r16] * (PEER_NKEYS // 16), axis=0)


def _peer_weights(rk2_ref, lim_ref, p1_ref, p2_ref, i1, tb):
    w = jnp.zeros((PEER_NKEYS, tb), jnp.bfloat16)
    for h in range(PEER_HEADS):
        p2h = p2_ref[h]
        sel = rk2_ref[h] <= _row_bf16(lim_ref[h, pl.ds(i1, 1), :])
        w = w + _row_bf16(p1_ref[h, pl.ds(i1, 1), :]) * jnp.where(sel, p2h, jnp.zeros_like(p2h))
    return w


def _peer_dense_kernel(x_ref, rk2_ref, lim_ref, p1_ref, p2_ref, u0_ref, ua_ref, ub_ref, vt_ref, g_ref, b_ref,
                       o_ref, xb_scr, yt_scr, ht0_scr, ht1_scr):
    k = pl.program_id(1)
    n_i1 = PEER_EB // PEER_NKEYS
    tb = x_ref.shape[0]
    nt = (((1,), (1,)), ((), ()))
    f32 = jnp.float32

    @pl.when(k == 0)
    def _():
        xb_scr[...] = x_ref[...].astype(jnp.bfloat16)
        yt_scr[...] = jnp.zeros_like(yt_scr)
        ht0_scr[...] = lax.dot_general(u0_ref[...], xb_scr[...], nt, preferred_element_type=f32)

    xb = xb_scr[...]
    n_sub = 2
    sub = PEER_EB // n_sub

    def second_half(ht_scr, blk, vt_off, acc):
        for j in range(n_sub):
            pieces = []
            for cc in range(sub // PEER_NKEYS):
                i1 = blk * n_i1 + j * (sub // PEER_NKEYS) + cc
                w = _peer_weights(rk2_ref, lim_ref, p1_ref, p2_ref, i1, tb)
                r0 = j * sub + cc * PEER_NKEYS
                hc = ht_scr[r0:r0 + PEER_NKEYS, :]
                gelu = 0.5 * hc * (1.0 + lax.erf(hc * (2.0 ** -0.5)))
                pieces.append(w * gelu.astype(jnp.bfloat16))
            at = jnp.concatenate(pieces, axis=0)
            c0 = vt_off + j * sub
            acc = acc + jnp.dot(vt_ref[:, c0:c0 + sub], at, preferred_element_type=f32)
        return acc

    acc = yt_scr[...]
    ht1_scr[...] = lax.dot_general(ua_ref[...], xb, nt, preferred_element_type=f32)
    acc = second_half(ht0_scr, 2 * k, 0, acc)
    ht0_scr[...] = lax.dot_general(ub_ref[...], xb, nt, preferred_element_type=f32)
    acc = second_half(ht1_scr, 2 * k + 1, PEER_EB, acc)
    yt_scr[...] = acc

    @pl.when(k == pl.num_programs(1) - 1)
    def _():
        z = ALPHA * x_ref[...] + yt_scr[...].T
        o_ref[...] = _layer_norm_rows(z, g_ref[...], b_ref[...])


def peer_ln(x, w_pq, sub_k1, sub_k2, peer_u, peer_v, ln_g, ln_b):
    T = x.shape[0]
    assert T % PEER_TB == 0 and T % PEER_RT == 0
    nt = T // PEER_TB
    half = PEER_DKEY // 2
    wq_t = w_pq.reshape(D_MODEL, PEER_HEADS * PEER_DKEY).T.astype(jnp.bfloat16)
    sshape = jax.ShapeDtypeStruct((PEER_HEADS, PEER_NKEYS, T), jnp.float32)
    sspec = pl.BlockSpec((PEER_HEADS, PEER_NKEYS, PEER_RT), lambda j: (0, 0, j))
    rk2, lim, p1, p2 = pl.pallas_call(
        _peer_route_kernel,
        grid=(T // PEER_RT,),
        in_specs=[pl.BlockSpec((PEER_RT, D_MODEL), lambda j: (j, 0)),
                  pl.BlockSpec((PEER_HEADS * PEER_DKEY, D_MODEL), lambda j: (0, 0)),
                  pl.BlockSpec((PEER_NKEYS, half), lambda j: (0, 0)),
                  pl.BlockSpec((PEER_NKEYS, half), lambda j: (0, 0))],
        out_specs=[sspec, sspec, sspec, sspec],
        out_shape=[jax.ShapeDtypeStruct(sshape.shape, jnp.bfloat16), sshape, sshape,
                   jax.ShapeDtypeStruct(sshape.shape, jnp.bfloat16)],
        compiler_params=pltpu.CompilerParams(dimension_semantics=("arbitrary",), vmem_limit_bytes=VMEM_LIMIT),
        name="peer_route",
    )(x, wq_t, sub_k1.astype(jnp.bfloat16), sub_k2.astype(jnp.bfloat16))

    u_b = peer_u.astype(jnp.bfloat16)
    vt_b = peer_v.T.astype(jnp.bfloat16)
    ne = PEER_N // PEER_EB
    assert ne % 2 == 0
    sspec2 = pl.BlockSpec((PEER_HEADS, PEER_NKEYS, PEER_TB), lambda j, k: (0, 0, j))
    return pl.pallas_call(
        _peer_dense_kernel,
        grid=(nt, ne // 2),
        in_specs=[pl.BlockSpec((PEER_TB, D_MODEL), lambda j, k: (j, 0)),
                  sspec2, sspec2, sspec2, sspec2,
                  pl.BlockSpec((PEER_EB, D_MODEL), lambda j, k: (0, 0)),
                  pl.BlockSpec((PEER_EB, D_MODEL), lambda j, k: (2 * k + 1, 0)),
                  pl.BlockSpec((PEER_EB, D_MODEL), lambda j, k: (jnp.minimum(2 * k + 2, ne - 1), 0)),
                  pl.BlockSpec((D_MODEL, 2 * PEER_EB), lambda j, k: (0, k)),
                  pl.BlockSpec((1, D_MODEL), lambda j, k: (0, 0)),
                  pl.BlockSpec((1, D_MODEL), lambda j, k: (0, 0))],
        out_specs=pl.BlockSpec((PEER_TB, D_MODEL), lambda j, k: (j, 0)),
        out_shape=jax.ShapeDtypeStruct((T, D_MODEL), jnp.float32),
        scratch_shapes=[pltpu.VMEM((PEER_TB, D_MODEL), jnp.bfloat16),
                        pltpu.VMEM((D_MODEL, PEER_TB), jnp.float32),
                        pltpu.VMEM((PEER_EB, PEER_TB), jnp.float32),
                        pltpu.VMEM((PEER_EB, PEER_TB), jnp.float32)],
        compiler_params=pltpu.CompilerParams(dimension_semantics=("arbitrary", "arbitrary"),
                                             vmem_limit_bytes=PEER_VMEM_LIMIT),
        name="peer_dense",
    )(x, rk2, lim, p1, p2, u_b, u_b, u_b, vt_b, ln_g.reshape(1, -1), ln_b.reshape(1, -1))


def kernel(x_prompt, x_sample, cache_kv_latent, cache_k_rope, state_C, state_n, state_m,
           cache_mem_k, cache_mem_v, page_table, mem_prompt, ln0_g, ln0_b, w_in, b_i, b_f,
           g_q, w_uq, g_kv, w_uk, w_uv, w_out, ln1_g, ln1_b, w_mq, w_mk, w_mv, w_mo,
           ln2_g, ln2_b, w_pq, sub_k1, sub_k2, peer_u, peer_v, ln3_g, ln3_b):
    B, S = x_prompt.shape[:2]
    NB, TQ = x_sample.shape[:2]
    past = page_table.shape[1] * PAGE_SIZE
    n_p = B * S
    l = 0
    x_all = jnp.concatenate([x_prompt.reshape(n_p, D_MODEL), x_sample.reshape(NB * TQ, D_MODEL)], axis=0)
    pos = jnp.concatenate([jnp.tile(jnp.arange(S), B), jnp.tile(past + jnp.arange(TQ), NB)])
    (xn, q_lat, q_rope, kv, kvk, kr, mq, mk, mv, gates, o_gate) = mix_in(
        x_all, pos, ln0_g, ln0_b, w_in[l], b_i[l], b_f[l], g_q[l], w_uq[l], g_kv[l], w_uk[l])

    mla_p = mla_attend_prompt(q_lat, q_rope, kvk, w_uv[l], B, S)
    mlh_p, C_p, n_pst, m_p = mlstm_prompt(mq, mk, mv, gates, B, S)

    kv_s = kv[n_p:].reshape(NB, TQ, MLA_KV_LORA)
    kr_s = kr[n_p:, :MLA_ROPE].reshape(NB, TQ, MLA_ROPE)
    mla_s = mla_attend_sample(q_lat[n_p:].reshape(NB, TQ, -1), q_rope[n_p:].reshape(NB, TQ, -1), kv_s, kr_s,
                              cache_kv_latent.reshape(cache_kv_latent.shape[1:]),
                              jnp.swapaxes(cache_k_rope.reshape(cache_k_rope.shape[1:]), 1, 2), page_table, w_uv[l])
    mlh_s, C_s, n_s, m_s = mlstm_step(mq[n_p:], mk[n_p:], mv[n_p:], gates[n_p:], state_C.reshape(state_C.shape[1:]),
                                      state_n.reshape(state_n.shape[1:]), state_m.reshape(state_m.shape[1:]), NB, TQ)

    mla_all = jnp.concatenate([mla_p, mla_s], axis=0)
    mlh_all = jnp.concatenate([mlh_p, mlh_s], axis=0)
    x1, qm = mix_out(xn, mla_all, mlh_all, o_gate, w_out[l], ln1_g[l], ln1_b[l], w_mq[l])
    mk_p, mv_p = mem_kv(mem_prompt, w_mk[l], w_mv[l])
    x2_p = mem_attend_ln(x1, qm, mk_p.reshape(B, N_MEM, D_MODEL), mv_p.reshape(B, N_MEM, D_MODEL),
                         w_mo[l], ln2_g[l], ln2_b[l], B, S)
    x2_s = mem_attend_ln(x1[n_p:], qm[n_p:], cache_mem_k.reshape(cache_mem_k.shape[1:]),
                         cache_mem_v.reshape(cache_mem_v.shape[1:]), w_mo[l], ln2_g[l], ln2_b[l], NB, TQ)
    x2 = jnp.concatenate([x2_p, x2_s], axis=0)
    x3 = peer_ln(x2, w_pq[l], sub_k1[l], sub_k2[l], peer_u[l], peer_v[l], ln3_g[l], ln3_b[l])
    st = lambda t: t[None]
    return (x3[:n_p].reshape(B, S, D_MODEL), x3[n_p:].reshape(NB, TQ, D_MODEL),
            st(kv[:n_p].reshape(B, S, MLA_KV_LORA)), st(kr[:n_p, :MLA_ROPE].reshape(B, S, MLA_ROPE)),
            st(C_p), st(n_pst), st(m_p), st(mk_p), st(mv_p),
            st(kv_s), st(kr_s), st(C_s), st(n_s), st(m_s))
```

```python
import jax, jax.numpy as jnp
from jax import lax
import numpy as np
from jax.experimental import pallas as pl
from jax.experimental.pallas import tpu as pltpu

D_MODEL = 1024
PAGE_SIZE = 128

MLA_HEADS = 8
MLA_NOPE = 64
MLA_ROPE = 32
MLA_V = 64
MLA_KV_LORA = 256
MLA_Q_LORA = 384
MLA_SCALE = (MLA_NOPE + MLA_ROPE) ** -0.5
ROPE_BASE = 10000.0
ML_HEADS = 4
ML_DH = 128
ML_CHUNK = 64
MLA_WIDTH = MLA_HEADS * MLA_V
ML_WIDTH = ML_HEADS * ML_DH
D_MIX = MLA_WIDTH + ML_WIDTH
N_MEM = 256
MEM_HEADS = 4
MEM_HD = D_MODEL // MEM_HEADS
PEER_HEADS = 8
PEER_NKEYS = 128
PEER_N = PEER_NKEYS * PEER_NKEYS
PEER_DKEY = 128
PEER_TOPK = 16
LN_EPS = 1e-5
RMS_EPS = 1e-6
DEPTH = 1
ALPHA = (2 * DEPTH) ** 0.25
NEG_INF = float('-inf')

VMEM_LIMIT = 48 * 1024 * 1024

IN_PAD = 2944
OFF_CQ, OFF_CKV, OFF_KR, OFF_MQ, OFF_MK, OFF_MV, OFF_G, OFF_O = 0, 384, 640, 768, 1280, 1792, 2304, 2432
MIX_TM = 256


def _layer_norm_rows(z, g, b):
    mu = jnp.mean(z, axis=-1, keepdims=True)
    zc = z - mu
    var = jnp.mean(zc * zc, axis=-1, keepdims=True)
    return zc * lax.rsqrt(var + LN_EPS) * g + b


def _rope_lanes(x, cos, sin_signed):
    n = x.shape[1]
    lane = lax.broadcasted_iota(jnp.int32, x.shape, 1)
    partner = jnp.where((lane & 31) < 16, pltpu.roll(x, n - 16, axis=1), pltpu.roll(x, 16, axis=1))
    return x * cos + partner * sin_signed


def _mix_in_kernel(x_ref, g0_ref, b0_ref, win_ref, gq_ref, wuq_ref, wuk_ref, gkv_ref, cos_ref, sin_ref, gb_ref,
                   xn_ref, ql_ref, qr_ref, kv_ref, kvk_ref, kr_ref, mq_ref, mk_ref, mv_ref, gate_ref, og_ref):
    xn = _layer_norm_rows(x_ref[...], g0_ref[...], b0_ref[...])
    xn_ref[...] = xn
    z = jnp.dot(xn.astype(jnp.bfloat16), win_ref[...], preferred_element_type=jnp.float32)
    cos = cos_ref[...]
    sin = sin_ref[...]
    cq = z[:, OFF_CQ:OFF_CQ + MLA_Q_LORA]
    cq = cq * lax.rsqrt(jnp.mean(cq * cq, axis=-1, keepdims=True) + RMS_EPS) * gq_ref[...]
    q = jnp.dot(cq.astype(jnp.bfloat16), wuq_ref[...], preferred_element_type=jnp.float32)
    n_nope = MLA_HEADS * MLA_NOPE
    qrope = _rope_lanes(q[:, n_nope:], jnp.concatenate([cos, cos], axis=1), jnp.concatenate([sin, sin], axis=1))
    qr_ref[...] = qrope.astype(jnp.bfloat16)
    ql_ref[...] = jnp.dot(q[:, :n_nope].astype(jnp.bfloat16), wuk_ref[...],
                          preferred_element_type=jnp.float32).astype(jnp.bfloat16)
    ckv = z[:, OFF_CKV:OFF_CKV + MLA_KV_LORA]
    kv = ckv * lax.rsqrt(jnp.mean(ckv * ckv, axis=-1, keepdims=True) + RMS_EPS) * gkv_ref[...]
    kv_ref[...] = kv
    kr = _rope_lanes(z[:, OFF_KR:OFF_KR + 128], cos, sin)
    kr_ref[...] = kr
    krt = kr + pltpu.roll(kr, 32, axis=1) + pltpu.roll(kr, 64, axis=1) + pltpu.roll(kr, 96, axis=1)
    kvk_ref[...] = jnp.concatenate([kv, krt, krt], axis=1).astype(jnp.bfloat16)
    mq_ref[...] = z[:, OFF_MQ:OFF_MQ + ML_WIDTH]
    mk_ref[...] = z[:, OFF_MK:OFF_MK + ML_WIDTH] * (ML_DH ** -0.5)
    mv_ref[...] = z[:, OFF_MV:OFF_MV + ML_WIDTH]
    g = z[:, OFF_G:OFF_G + 128] + gb_ref[...]
    lane = lax.broadcasted_iota(jnp.int32, g.shape, 1)
    gate_ref[...] = jnp.where(lane < ML_HEADS, g, jax.nn.log_sigmoid(g))
    og_ref[...] = jax.nn.sigmoid(z[:, OFF_O:OFF_O + ML_WIDTH])


def mix_in(x, pos, ln0_g, ln0_b, w_in, b_i, b_f, g_q, w_uq, g_kv, w_uk):
    T = x.shape[0]
    f32, bf = jnp.float32, jnp.bfloat16
    zc = lambda n: jnp.zeros((D_MODEL, n), f32)
    win_p = jnp.concatenate([w_in[:, :672], zc(96), w_in[:, 672:2208], w_in[:, 2208:2216], zc(120), w_in[:, 2216:]],
                            axis=1).astype(bf)
    assert win_p.shape[1] == IN_PAD
    wuq_p = jnp.concatenate([w_uq[:, :, :MLA_NOPE].reshape(MLA_Q_LORA, -1),
                             w_uq[:, :, MLA_NOPE:].reshape(MLA_Q_LORA, -1)], axis=1).astype(bf)
    hh = jnp.arange(MLA_HEADS)
    wuk_blk = jnp.zeros((MLA_HEADS, MLA_NOPE, MLA_HEADS, MLA_KV_LORA), f32)
    wuk_blk = wuk_blk.at[hh, :, hh, :].set(jnp.transpose(w_uk, (1, 2, 0)))
    wuk_blk = wuk_blk.reshape(MLA_HEADS * MLA_NOPE, MLA_HEADS * MLA_KV_LORA).astype(bf)
    inv = 1.0 / (ROPE_BASE ** (jnp.arange(0, MLA_ROPE, 2, dtype=f32) / MLA_ROPE))
    ang = pos.astype(f32)[:, None] * inv[None, :]
    c, s = jnp.cos(ang), jnp.sin(ang)
    cos128 = jnp.tile(jnp.concatenate([c, c], axis=1), (1, 4))
    sin128 = jnp.tile(jnp.concatenate([-s, s], axis=1), (1, 4))
    gbias = jnp.concatenate([b_i, b_f, jnp.zeros((120,), f32)]).reshape(1, 128)
    row = lambda n: pl.BlockSpec((MIX_TM, n), lambda i: (i, 0))
    full = lambda a: pl.BlockSpec(a.shape, lambda i: (0,) * a.ndim)
    ins = [x, ln0_g.reshape(1, -1), ln0_b.reshape(1, -1), win_p, g_q.reshape(1, -1), wuq_p, wuk_blk,
           g_kv.reshape(1, -1), cos128, sin128, gbias]
    in_specs = [row(D_MODEL)] + [full(a) for a in ins[1:8]] + [row(128), row(128), full(gbias)]
    outs = [(D_MODEL, f32), (MLA_HEADS * MLA_KV_LORA, bf), (MLA_HEADS * MLA_ROPE, bf), (MLA_KV_LORA, f32),
            (MLA_KV_LORA + MLA_HEADS * MLA_ROPE, bf), (128, f32), (ML_WIDTH, f32), (ML_WIDTH, f32),
            (ML_WIDTH, f32), (128, f32), (ML_WIDTH, f32)]
    return pl.pallas_call(
        _mix_in_kernel,
        grid=(T // MIX_TM,),
        in_specs=in_specs,
        out_specs=[row(n) for n, _ in outs],
        out_shape=[jax.ShapeDtypeStruct((T, n), dt) for n, dt in outs],
        compiler_params=pltpu.CompilerParams(dimension_semantics=("arbitrary",), vmem_limit_bytes=VMEM_LIMIT),
        name="mix_in",
    )(*ins)


ATT_BQ = 128
ATT_BK = 512


def _mla_prompt_kernel(ql_ref, qr_ref, kvk_ref, wuv_ref, o_ref, acc_scr, m_scr, l_scr):
    qi = pl.program_id(1)
    ql = jnp.concatenate([ql_ref[:, h * MLA_KV_LORA:(h + 1) * MLA_KV_LORA] for h in range(MLA_HEADS)], axis=0)
    qr_all = qr_ref[...]
    lane_head = lax.broadcasted_iota(jnp.int32, qr_all.shape, 1) // MLA_ROPE
    qr = jnp.concatenate([jnp.where(lane_head == h, qr_all, jnp.zeros_like(qr_all)) for h in range(MLA_HEADS)],
                         axis=0)
    q = jnp.concatenate([ql, qr], axis=1)
    acc_scr[...] = jnp.zeros_like(acc_scr)
    m_scr[...] = jnp.full_like(m_scr, NEG_INF)
    l_scr[...] = jnp.zeros_like(l_scr)
    nt = (((1,), (1,)), ((), ()))
    n_last = (qi * ATT_BQ) // ATT_BK
    rows = q.shape[0]
    tok = qi * ATT_BQ + (lax.broadcasted_iota(jnp.int32, (rows, ATT_BK), 0) & (ATT_BQ - 1))
    col = lax.broadcasted_iota(jnp.int32, (rows, ATT_BK), 1)

    def scores(kj):
        k0 = pl.multiple_of(kj * ATT_BK, ATT_BK)
        s = lax.dot_general(q, kvk_ref[pl.ds(k0, ATT_BK), :], nt, preferred_element_type=jnp.float32) * MLA_SCALE
        return jnp.where(col + k0 <= tok, s, NEG_INF)

    def accumulate(kj, s):
        k0 = pl.multiple_of(kj * ATT_BK, ATT_BK)
        kvb = kvk_ref[pl.ds(k0, ATT_BK), :MLA_KV_LORA]
        m_old = m_scr[...]
        m_new = jnp.maximum(m_old, jnp.max(s, axis=1, keepdims=True))
        alpha = jnp.exp(m_old - m_new)
        p = jnp.exp(s - m_new)
        l_scr[...] = alpha * l_scr[...] + jnp.sum(p, axis=1, keepdims=True)
        acc_scr[...] = alpha * acc_scr[...] + jnp.dot(p.astype(jnp.bfloat16), kvb,
                                                      preferred_element_type=jnp.float32)
        m_scr[...] = m_new

    def body(kj, s_cur):
        s_next = scores(kj + 1)
        accumulate(kj, s_cur)
        return s_next

    s_last = lax.fori_loop(0, n_last, body, scores(0))
    accumulate(n_last, s_last)
    o = (acc_scr[...] / l_scr[...]).astype(jnp.bfloat16)
    r = jnp.dot(o, wuv_ref[...], preferred_element_type=jnp.float32)
    col_head = lax.broadcasted_iota(jnp.int32, (ATT_BQ, MLA_WIDTH), 1) // MLA_V
    out = jnp.zeros((ATT_BQ, MLA_WIDTH), jnp.float32)
    for h in range(MLA_HEADS):
        out = out + jnp.where(col_head == h, r[h * ATT_BQ:(h + 1) * ATT_BQ], 0.0)
    o_ref[...] = out


def mla_attend_prompt(q_lat, q_rope, kvk, w_uv, n_seq, seq):
    assert seq % ATT_BK == 0 and ATT_BK % ATT_BQ == 0
    H, C = MLA_HEADS, MLA_KV_LORA
    nq = seq // ATT_BQ
    rows = H * ATT_BQ
    return pl.pallas_call(
        _mla_prompt_kernel,
        grid=(n_seq, nq),
        in_specs=[pl.BlockSpec((ATT_BQ, H * C), lambda b, i: (b * nq + i, 0)),
                  pl.BlockSpec((ATT_BQ, H * MLA_ROPE), lambda b, i: (b * nq + i, 0)),
                  pl.BlockSpec((seq, C + H * MLA_ROPE), lambda b, i: (b, 0)),
                  pl.BlockSpec((C, H * MLA_V), lambda b, i: (0, 0))],
        out_specs=pl.BlockSpec((ATT_BQ, H * MLA_V), lambda b, i: (b * nq + i, 0)),
        out_shape=jax.ShapeDtypeStruct((n_seq * seq, H * MLA_V), jnp.float32),
        scratch_shapes=[pltpu.VMEM((rows, C), jnp.float32),
                        pltpu.VMEM((rows, 1), jnp.float32),
                        pltpu.VMEM((rows, 1), jnp.float32)],
        compiler_params=pltpu.CompilerParams(dimension_semantics=("arbitrary", "arbitrary"),
                                             vmem_limit_bytes=VMEM_LIMIT),
        name="mla_prompt",
    )(q_lat, q_rope, kvk, w_uv.reshape(C, H * MLA_V).astype(jnp.bfloat16))


SMP_KC = 2048


def _mla_sample_kernel(pt_ref, ql_ref, qr_ref, kvn_ref, krn_ref, wuv_ref, lat_hbm, rope_hbm, o_ref,
                       lat_buf, rope_buf, lat_bf, sem_lat, sem_rope):
    b = pl.program_id(0)
    nb = pl.num_programs(0)
    n_pages = pt_ref.shape[1]
    rows = ql_ref.shape[0]
    tq = kvn_ref.shape[0]
    n_keys = n_pages * PAGE_SIZE

    def page_copies(seq, slot, p):
        page = pt_ref[seq, p]
        dst = pl.ds(p * PAGE_SIZE, PAGE_SIZE)
        return (pltpu.make_async_copy(lat_hbm.at[page], lat_buf.at[slot, dst], sem_lat.at[slot]),
                pltpu.make_async_copy(rope_hbm.at[page], rope_buf.at[slot, :, dst], sem_rope.at[slot]))

    def start_fetch(seq, slot):
        def body(p, c):
            for cp in page_copies(seq, slot, p):
                cp.start()
            return c
        lax.fori_loop(0, n_pages, body, 0)

    def wait_fetch(seq, slot):
        def body(p, c):
            for cp in page_copies(seq, slot, p):
                cp.wait()
            return c
        lax.fori_loop(0, n_pages, body, 0)

    slot = b % 2

    @pl.when(b == 0)
    def _():
        start_fetch(0, 0)

    @pl.when(b + 1 < nb)
    def _():
        start_fetch(b + 1, 1 - slot)

    wait_fetch(b, slot)

    bf, f32 = jnp.bfloat16, jnp.float32
    ql = ql_ref[...]
    qr = qr_ref[...]
    nt = (((1,), (1,)), ((), ()))
    parts = []
    for c in range(n_keys // SMP_KC):
        r = pl.ds(c * SMP_KC, SMP_KC)
        lb = lat_buf[slot, r, :].astype(bf)
        rb = rope_buf[slot, :, r].astype(bf)
        lat_bf[r, :] = lb
        parts.append(lax.dot_general(ql, lb, nt, preferred_element_type=f32)
                     + jnp.dot(qr, rb, preferred_element_type=f32))
    s_past = jnp.concatenate(parts, axis=1) * MLA_SCALE
    qlf, qrf = ql.astype(f32), qr.astype(f32)
    kvn = kvn_ref[...].astype(f32)
    krn = krn_ref[...].astype(f32)
    tok = lax.broadcasted_iota(jnp.int32, (rows, 1), 0) % tq
    s_new = []
    for j in range(tq):
        sj = (jnp.sum(qlf * kvn[j:j + 1, :], axis=1, keepdims=True)
              + jnp.sum(qrf * krn[j:j + 1, :], axis=1, keepdims=True)) * MLA_SCALE
        s_new.append(jnp.where(tok >= j, sj, NEG_INF))
    m = jnp.max(s_past, axis=1, keepdims=True)
    for sj in s_new:
        m = jnp.maximum(m, sj)
    p_past = jnp.exp(s_past - m)
    p_new = [jnp.exp(sj - m) for sj in s_new]
    l = jnp.sum(p_past, axis=1, keepdims=True)
    for pj in p_new:
        l = l + pj
    inv = 1.0 / l
    o = jnp.dot((p_past * inv).astype(bf), lat_bf[...], preferred_element_type=f32)
    for j in range(tq):
        o = o + (p_new[j] * inv).astype(bf).astype(f32) * kvn[j:j + 1, :]
    r = jnp.dot(o.astype(bf), wuv_ref[...], preferred_element_type=f32)
    col_head = lax.broadcasted_iota(jnp.int32, (tq, MLA_WIDTH), 1) // MLA_V
    out = jnp.zeros((tq, MLA_WIDTH), f32)
    for h in range(MLA_HEADS):
        out = out + jnp.where(col_head == h, r[h * tq:(h + 1) * tq, :], 0.0)
    o_ref[...] = out


def mla_attend_sample(q_lat, q_rope, kv_new, kr_new, pool_lat, pool_rope_t, page_table, w_uv):
    NB, T = q_lat.shape[:2]
    H, C, R = MLA_HEADS, MLA_KV_LORA, MLA_ROPE
    bf = jnp.bfloat16
    n_pages = page_table.shape[1]
    n_keys = n_pages * PAGE_SIZE
    assert n_keys % SMP_KC == 0
    ql = q_lat.reshape(NB, T, H, C).transpose(0, 2, 1, 3).reshape(NB, H * T, C)
    qr = q_rope.reshape(NB, T, H, R).transpose(0, 2, 1, 3).reshape(NB, H * T, R)
    seq = lambda n, w: pl.BlockSpec((None, n, w), lambda b, pt: (b, 0, 0))
    grid_spec = pltpu.PrefetchScalarGridSpec(
        num_scalar_prefetch=1,
        grid=(NB,),
        in_specs=[seq(H * T, C), seq(H * T, R), seq(T, C), seq(T, R),
                  pl.BlockSpec((C, H * MLA_V), lambda b, pt: (0, 0)),
                  pl.BlockSpec(memory_space=pl.ANY), pl.BlockSpec(memory_space=pl.ANY)],
        out_specs=seq(T, H * MLA_V),
        scratch_shapes=[pltpu.VMEM((2, n_keys, C), jnp.float32), pltpu.VMEM((2, R, n_keys), jnp.float32),
                        pltpu.VMEM((n_keys, C), bf),
                        pltpu.SemaphoreType.DMA((2,)), pltpu.SemaphoreType.DMA((2,))],
    )
    out = pl.pallas_call(
        _mla_sample_kernel,
        grid_spec=grid_spec,
        out_shape=jax.ShapeDtypeStruct((NB, T, H * MLA_V), jnp.float32),
        compiler_params=pltpu.CompilerParams(dimension_semantics=("arbitrary",), vmem_limit_bytes=VMEM_LIMIT),
        name="mla_sample",
    )(page_table, ql, qr, kv_new.astype(bf), kr_new.astype(bf), w_uv.reshape(C, H * MLA_V).astype(bf), pool_lat, pool_rope_t)
    return out.reshape(NB * T, H * MLA_V)


def _split3(x):
    hi = x.astype(jnp.bfloat16)
    r1 = x - hi.astype(jnp.float32)
    mid = r1.astype(jnp.bfloat16)
    lo = (r1 - mid.astype(jnp.float32)).astype(jnp.bfloat16)
    return hi, mid, lo


def _mlstm_chunk(q_all, k_all, v_all, gates, c_refs, n_refs, m_refs):
    L = q_all.shape[0]
    bf, f32 = jnp.bfloat16, jnp.float32
    row_t = lax.broadcasted_iota(jnp.int32, (L, L), 0)
    col_s = lax.broadcasted_iota(jnp.int32, (L, L), 1)
    causal = col_s <= row_t
    tril = jnp.where(causal, 1.0, 0.0).astype(bf)
    cum = sum(jnp.dot(tril, part, preferred_element_type=f32) for part in _split3(gates))
    gates_t = gates.T
    cum_t = cum.T
    nt = (((1,), (1,)), ((), ()))
    outs = []
    for h in range(ML_HEADS):
        c = slice(h * ML_DH, (h + 1) * ML_DH)
        q, k, v = q_all[:, c], k_all[:, c], v_all[:, c]
        C, n, m = c_refs[0](h), n_refs[0](h), m_refs[0](h)
        ig_col = gates[:, h:h + 1]
        b_col = cum[:, ML_HEADS + h:ML_HEADS + h + 1]
        ig_row = gates_t[h:h + 1, :]
        b_row = cum_t[ML_HEADS + h:ML_HEADS + h + 1, :]
        D = jnp.where(causal, b_col - b_row + ig_row, NEG_INF)
        inter = b_col + m
        m_t = jnp.maximum(inter, jnp.max(D, axis=1, keepdims=True))
        qb, kb, vb = q.astype(bf), k.astype(bf), v.astype(bf)
        A = jnp.exp(D - m_t) * lax.dot_general(qb, kb, nt, preferred_element_type=f32)
        w_inter = jnp.exp(inter - m_t)
        num = w_inter * jnp.dot(qb, C.astype(bf), preferred_element_type=f32) \
            + jnp.dot(A.astype(bf), vb, preferred_element_type=f32)
        qn = jnp.sum(qb.astype(f32) * n.astype(bf).astype(f32), axis=1, keepdims=True)
        den = w_inter * qn + jnp.sum(A, axis=1, keepdims=True)
        outs.append(num / jnp.maximum(jnp.abs(den), jnp.exp(-m_t)))
        b_end = b_col[L - 1:L, :]
        m_new = jnp.maximum(b_end + m, jnp.max(b_end - b_row + ig_row, axis=1, keepdims=True))
        a_prev = jnp.exp(b_end + m - m_new)
        kw = k * jnp.exp(b_end - b_col + ig_col - m_new)
        c_refs[1](h, a_prev * C + jnp.dot(kw.T.astype(bf), vb, preferred_element_type=f32))
        n_refs[1](h, a_prev * n + jnp.sum(kw, axis=0, keepdims=True))
        m_refs[1](h, m_new)
    return jnp.concatenate(outs, axis=1)


def _state_access(c_get, n_get, m_get, c_set, n_set, m_set):
    c_refs = (lambda h: c_get[h], lambda h, val: c_set.__setitem__(h, val))
    n_refs = (lambda h: n_get[h:h + 1, :], lambda h, val: n_set.__setitem__((slice(h, h + 1), slice(None)), val))
    m_refs = (lambda h: m_get[h:h + 1, 0:1],
              lambda h, val: m_set.__setitem__((slice(h, h + 1), slice(None)), jnp.broadcast_to(val, (1, 128))))
    return c_refs, n_refs, m_refs


def _mlstm_prompt_kernel(q_ref, k_ref, v_ref, g_ref, h_ref, c_out, n_out, m_out, c_scr, n_scr, m_scr):
    j = pl.program_id(1)

    @pl.when(j == 0)
    def _():
        c_scr[...] = jnp.zeros_like(c_scr)
        n_scr[...] = jnp.zeros_like(n_scr)
        m_scr[...] = jnp.zeros_like(m_scr)

    h_ref[...] = _mlstm_chunk(q_ref[...], k_ref[...], v_ref[...], g_ref[...],
                              *_state_access(c_scr, n_scr, m_scr, c_scr, n_scr, m_scr))

    @pl.when(j == pl.num_programs(1) - 1)
    def _():
        c_out[...] = c_scr[...]
        n_out[...] = n_scr[0:ML_HEADS, :]
        m_out[...] = m_scr[0:ML_HEADS, :]


def mlstm_prompt(mq, mk, mv, gates, n_seq, seq):
    nc = seq // ML_CHUNK
    f32 = jnp.float32
    tok = lambda w: pl.BlockSpec((ML_CHUNK, w), lambda b, j: (b * nc + j, 0))
    h, C, n, m = pl.pallas_call(
        _mlstm_prompt_kernel,
        grid=(n_seq, nc),
        in_specs=[tok(ML_WIDTH), tok(ML_WIDTH), tok(ML_WIDTH), tok(128)],
        out_specs=[tok(ML_WIDTH),
                   pl.BlockSpec((None, ML_HEADS, ML_DH, ML_DH), lambda b, j: (b, 0, 0, 0)),
                   pl.BlockSpec((None, ML_HEADS, ML_DH), lambda b, j: (b, 0, 0)),
                   pl.BlockSpec((None, ML_HEADS, 128), lambda b, j: (b, 0, 0))],
        out_shape=[jax.ShapeDtypeStruct((n_seq * seq, ML_WIDTH), f32),
                   jax.ShapeDtypeStruct((n_seq, ML_HEADS, ML_DH, ML_DH), f32),
                   jax.ShapeDtypeStruct((n_seq, ML_HEADS, ML_DH), f32),
                   jax.ShapeDtypeStruct((n_seq, ML_HEADS, 128), f32)],
        scratch_shapes=[pltpu.VMEM((ML_HEADS, ML_DH, ML_DH), f32), pltpu.VMEM((8, ML_DH), f32),
                        pltpu.VMEM((8, 128), f32)],
        compiler_params=pltpu.CompilerParams(dimension_semantics=("arbitrary", "arbitrary"),
                                             vmem_limit_bytes=VMEM_LIMIT),
        name="mlstm_prompt",
    )(mq, mk, mv, gates)
    return h, C, n, m[:, :, 0]


def _mlstm_step_kernel(q_ref, k_ref, v_ref, g_ref, c_in, n_in, m_in, h_ref, c_out, n_out, m_out):
    h_ref[...] = _mlstm_chunk(q_ref[...], k_ref[...], v_ref[...], g_ref[...],
                              *_state_access(c_in, n_in, m_in, c_out, n_out, m_out))


def mlstm_step(mq, mk, mv, gates, state_C, state_n, state_m, n_seq, rows):
    f32 = jnp.float32
    tok = lambda w: pl.BlockSpec((None, rows, w), lambda b: (b, 0, 0))
    st_c = pl.BlockSpec((None, ML_HEADS, ML_DH, ML_DH), lambda b: (b, 0, 0, 0))
    st_n = pl.BlockSpec((None, ML_HEADS, ML_DH), lambda b: (b, 0, 0))
    st_m = pl.BlockSpec((None, ML_HEADS, 128), lambda b: (b, 0, 0))
    r3 = lambda t: t.reshape(n_seq, rows, t.shape[-1])
    m_in = jnp.broadcast_to(state_m[:, :, None], (n_seq, ML_HEADS, 128))
    h, C, n, m = pl.pallas_call(
        _mlstm_step_kernel,
        grid=(n_seq,),
        in_specs=[tok(ML_WIDTH), tok(ML_WIDTH), tok(ML_WIDTH), tok(128), st_c, st_n, st_m],
        out_specs=[tok(ML_WIDTH), st_c, st_n, st_m],
        out_shape=[jax.ShapeDtypeStruct((n_seq, rows, ML_WIDTH), f32),
                   jax.ShapeDtypeStruct((n_seq, ML_HEADS, ML_DH, ML_DH), f32),
                   jax.ShapeDtypeStruct((n_seq, ML_HEADS, ML_DH), f32),
                   jax.ShapeDtypeStruct((n_seq, ML_HEADS, 128), f32)],
        compiler_params=pltpu.CompilerParams(dimension_semantics=("arbitrary",), vmem_limit_bytes=VMEM_LIMIT),
        name="mlstm_step",
    )(r3(mq), r3(mk), r3(mv), r3(gates), state_C, state_n, m_in)
    return h.reshape(n_seq * rows, ML_WIDTH), C, n, m[:, :, 0]


POST_TM = 256


def _mix_out_kernel(xn_ref, mla_ref, mlh_ref, og_ref, wout_ref, g_ref, b_ref, wmq_ref, x1_ref, qm_ref):
    mixed = jnp.concatenate([mla_ref[...], og_ref[...] * mlh_ref[...]], axis=1).astype(jnp.bfloat16)
    mix = jnp.dot(mixed, wout_ref[...], preferred_element_type=jnp.float32)
    x1 = _layer_norm_rows(ALPHA * xn_ref[...] + mix, g_ref[...], b_ref[...])
    x1_ref[...] = x1
    qm_ref[...] = jnp.dot(x1.astype(jnp.bfloat16), wmq_ref[...],
                          preferred_element_type=jnp.float32).astype(jnp.bfloat16)


def mix_out(xn, mla_o, ml_h, o_gate, w_out, ln_g, ln_b, w_mq):
    T = xn.shape[0]
    bf = jnp.bfloat16
    row = lambda n: pl.BlockSpec((POST_TM, n), lambda i: (i, 0))
    full = lambda shape: pl.BlockSpec(shape, lambda i: (0,) * len(shape))
    return pl.pallas_call(
        _mix_out_kernel,
        grid=(T // POST_TM,),
        in_specs=[row(D_MODEL), row(MLA_WIDTH), row(ML_WIDTH), row(ML_WIDTH), full((D_MIX, D_MODEL)),
                  full((1, D_MODEL)), full((1, D_MODEL)), full((D_MODEL, D_MODEL))],
        out_specs=[row(D_MODEL), row(D_MODEL)],
        out_shape=[jax.ShapeDtypeStruct((T, D_MODEL), jnp.float32), jax.ShapeDtypeStruct((T, D_MODEL), bf)],
        compiler_params=pltpu.CompilerParams(dimension_semantics=("arbitrary",), vmem_limit_bytes=VMEM_LIMIT),
        name="mix_out",
    )(xn, mla_o, ml_h, o_gate, w_out.astype(bf), ln_g.reshape(1, -1), ln_b.reshape(1, -1),
      w_mq.reshape(D_MODEL, D_MODEL).astype(bf))


def _mem_attn_kernel(x1_ref, qm_ref, mk_ref, mv_ref, wmo_ref, g_ref, b_ref, x2_ref):
    q = qm_ref[...]
    nt = (((1,), (1,)), ((), ()))
    outs = []
    for h in range(MEM_HEADS):
        c = slice(h * MEM_HD, (h + 1) * MEM_HD)
        mk, mv = mk_ref[:, c].astype(jnp.bfloat16), mv_ref[:, c].astype(jnp.bfloat16)
        s = lax.dot_general(q[:, c], mk, nt, preferred_element_type=jnp.float32) * (MEM_HD ** -0.5)
        m = jnp.max(s, axis=-1, keepdims=True)
        p = jnp.exp(s - m)
        p = p / jnp.sum(p, axis=-1, keepdims=True)
        outs.append(jnp.dot(p.astype(jnp.bfloat16), mv, preferred_element_type=jnp.float32))
    o = jnp.concatenate(outs, axis=1).astype(jnp.bfloat16)
    att = jnp.dot(o, wmo_ref[...], preferred_element_type=jnp.float32)
    x2_ref[...] = _layer_norm_rows(ALPHA * x1_ref[...] + att, g_ref[...], b_ref[...])


def _mem_attn_sample_kernel(x1_ref, qm_ref, mk_ref, mv_ref, wmo_ref, g_ref, b_ref, x2_ref):
    tq = x1_ref.shape[0]
    bf, f32 = jnp.bfloat16, jnp.float32
    q = qm_ref[...]
    qs = jnp.concatenate([q[:, h * MEM_HD:(h + 1) * MEM_HD] for h in range(MEM_HEADS)], axis=0)
    kall = mk_ref[...].reshape(N_MEM * MEM_HEADS, MEM_HD).astype(bf)
    vall = mv_ref[...].reshape(N_MEM * MEM_HEADS, MEM_HD).astype(bf)
    s = lax.dot_general(qs, kall, (((1,), (1,)), ((), ())), preferred_element_type=f32) * (MEM_HD ** -0.5)
    row_h = lax.broadcasted_iota(jnp.int32, s.shape, 0) // tq
    col_h = lax.broadcasted_iota(jnp.int32, s.shape, 1) % MEM_HEADS
    s = jnp.where(row_h == col_h, s, NEG_INF)
    p = jnp.exp(s - jnp.max(s, axis=-1, keepdims=True))
    p = p / jnp.sum(p, axis=-1, keepdims=True)
    o = jnp.dot(p.astype(bf), vall, preferred_element_type=f32)
    o = jnp.concatenate([o[h * tq:(h + 1) * tq] for h in range(MEM_HEADS)], axis=1).astype(bf)
    att = jnp.dot(o, wmo_ref[...], preferred_element_type=f32)
    x2_ref[...] = _layer_norm_rows(ALPHA * x1_ref[...] + att, g_ref[...], b_ref[...])


def mem_attend_ln(x1, qm, mem_k, mem_v, w_mo, ln_g, ln_b, n_seq, rows_per_seq):
    bf = jnp.bfloat16
    wmo = w_mo.reshape(D_MODEL, D_MODEL).astype(bf)
    g, b = ln_g.reshape(1, -1), ln_b.reshape(1, -1)
    cp = pltpu.CompilerParams(dimension_semantics=("arbitrary",) * 2, vmem_limit_bytes=VMEM_LIMIT)
    mem = pl.BlockSpec((None, N_MEM, D_MODEL), lambda s, i: (s, 0, 0))
    full = lambda shape: pl.BlockSpec(shape, lambda s, i: (0,) * len(shape))
    if rows_per_seq % POST_TM == 0:
        nb = rows_per_seq // POST_TM
        tok = pl.BlockSpec((POST_TM, D_MODEL), lambda s, i: (s * nb + i, 0))
        return pl.pallas_call(
            _mem_attn_kernel, grid=(n_seq, nb),
            in_specs=[tok, tok, mem, mem, full((D_MODEL, D_MODEL)), full((1, D_MODEL)), full((1, D_MODEL))],
            out_specs=tok, out_shape=jax.ShapeDtypeStruct((n_seq * rows_per_seq, D_MODEL), jnp.float32),
            compiler_params=cp, name="mem_attn_prompt",
        )(x1, qm, mem_k, mem_v, wmo, g, b)
    mem = pl.BlockSpec((None, N_MEM, MEM_HEADS, MEM_HD), lambda s, i: (s, 0, 0, 0))
    x3 = x1.reshape(n_seq, rows_per_seq, D_MODEL)
    q3 = qm.reshape(n_seq, rows_per_seq, D_MODEL)
    tok = pl.BlockSpec((None, rows_per_seq, D_MODEL), lambda s, i: (s, 0, 0))
    out = pl.pallas_call(
        _mem_attn_sample_kernel, grid=(n_seq, 1),
        in_specs=[tok, tok, mem, mem, full((D_MODEL, D_MODEL)), full((1, D_MODEL)), full((1, D_MODEL))],
        out_specs=tok, out_shape=jax.ShapeDtypeStruct(x3.shape, jnp.float32),
        compiler_params=cp, name="mem_attn_sample",
    )(x3, q3, mem_k, mem_v, wmo, g, b)
    return out.reshape(x1.shape)


def _mem_kv_kernel(m_ref, w_ref, o_ref):
    o_ref[...] = jnp.dot(m_ref[...].astype(jnp.bfloat16), w_ref[...], preferred_element_type=jnp.float32)


def mem_kv(mem, w_mk, w_mv):
    B = mem.shape[0]
    w = jnp.concatenate([w_mk.reshape(D_MODEL, D_MODEL), w_mv.reshape(D_MODEL, D_MODEL)], axis=1).astype(jnp.bfloat16)
    out = pl.pallas_call(
        _mem_kv_kernel, grid=(B,),
        in_specs=[pl.BlockSpec((N_MEM, D_MODEL), lambda i: (i, 0)), pl.BlockSpec((D_MODEL, 2 * D_MODEL), lambda i: (0, 0))],
        out_specs=pl.BlockSpec((N_MEM, 2 * D_MODEL), lambda i: (i, 0)),
        out_shape=jax.ShapeDtypeStruct((B * N_MEM, 2 * D_MODEL), jnp.float32),
        compiler_params=pltpu.CompilerParams(dimension_semantics=("arbitrary",), vmem_limit_bytes=VMEM_LIMIT),
        name="mem_kv",
    )(mem.reshape(B * N_MEM, D_MODEL), w)
    mk = out[:, :D_MODEL].reshape(B, N_MEM, MEM_HEADS, MEM_HD)
    mv = out[:, D_MODEL:].reshape(B, N_MEM, MEM_HEADS, MEM_HD)
    return mk, mv


PEER_RT = 256
PEER_TB = 512
PEER_EB = 1024
PEER_VMEM_LIMIT = 58 * 1024 * 1024


def _top16_rows(s, row_id, exact_ties):
    big = float(2 ** 20)
    out_id = lax.broadcasted_iota(jnp.int32, (PEER_TOPK, s.shape[1]), 0)
    stacked = jnp.zeros((PEER_TOPK, s.shape[1]), jnp.float32)
    rank = jnp.full(s.shape, float(PEER_TOPK), jnp.float32)
    rows = []
    for k in range(PEER_TOPK):
        m = jnp.max(s, axis=0, keepdims=True)
        hit = s == m
        if exact_ties:
            hit = row_id == jnp.min(jnp.where(hit, row_id, big), axis=0, keepdims=True)
        s = jnp.where(hit, NEG_INF, s)
        rank = jnp.where(hit, float(k), rank)
        rows.append(m)
        stacked = jnp.where(out_id == k, m, stacked)
    return rows, stacked, rank, s


def _peer_route_head(s1, s2, exact_ties):
    tb = s1.shape[1]
    row128 = lax.broadcasted_iota(jnp.int32, (PEER_NKEYS, tb), 0).astype(jnp.float32)
    sub8 = lax.broadcasted_iota(jnp.int32, (8, tb), 0)
    sub8f = sub8.astype(jnp.float32)
    r1, v1, rank1, left1 = _top16_rows(s1, row128, exact_ties)
    r2, v2, rank2, left2 = _top16_rows(s2, row128, exact_ties)
    groups, ids = [], []
    for b in range(8):
        lim = PEER_TOPK // (b + 1)
        for a0 in range(0, lim, 8):
            g = v1[a0:a0 + 8] + r2[b]
            if lim - a0 < 8:
                g = jnp.where(sub8 < lim - a0, g, NEG_INF)
            groups.append(g)
            ids.append((sub8f + float(a0)) * float(PEER_TOPK) + float(b))
    groups.append(r1[0] + v2[8:16])
    ids.append(sub8f + 8.0)
    cand = jnp.concatenate(groups, axis=0)
    vals, _, _, left = _top16_rows(cand, jnp.concatenate(ids, axis=0), exact_ties)
    z = jnp.ones_like(vals[0])
    for k in range(1, PEER_TOPK):
        z = z + jnp.exp(vals[k] - vals[0])
    taken = jnp.where((left == NEG_INF) & (cand > NEG_INF), 1.0, 0.0)
    cnt_lo = jnp.zeros((8, tb), jnp.float32)
    gi = 0
    for b in range(8):
        for a0 in range(0, PEER_TOPK // (b + 1), 8):
            if a0 == 0:
                cnt_lo = cnt_lo + taken[gi * 8:(gi + 1) * 8]
            else:
                cnt_hi = taken[gi * 8:(gi + 1) * 8]
            gi += 1
    tail = jnp.sum(taken[gi * 8:(gi + 1) * 8], axis=0, keepdims=True)
    cnt_lo = cnt_lo + jnp.where(sub8 == 0, tail, 0.0)
    lim_full = jnp.full((PEER_NKEYS, tb), -1.0, jnp.float32)
    for a in range(PEER_TOPK):
        cnt = cnt_lo if a < 8 else cnt_hi
        lim_full = jnp.where(rank1 == float(a), cnt[a % 8:a % 8 + 1] - 1.0, lim_full)
    n_taken = (jnp.sum(jnp.where(left1 == NEG_INF, 1.0, 0.0), axis=0, keepdims=True)
               + jnp.sum(jnp.where(left2 == NEG_INF, 1.0, 0.0), axis=0, keepdims=True)
               + jnp.sum(taken, axis=0, keepdims=True))
    return (rank2.astype(jnp.bfloat16), lim_full, jnp.exp(s1 - r1[0]) / z,
            jnp.exp(s2 - r2[0]).astype(jnp.bfloat16), n_taken)


def _peer_route_kernel(x_ref, wq_ref, k1_ref, k2_ref, rk2_ref, lim_ref, p1_ref, p2_ref, qt_scr):
    half = PEER_DKEY // 2
    xb = x_ref[...].astype(jnp.bfloat16)
    qt_scr[...] = lax.dot_general(wq_ref[...], xb, (((1,), (1,)), ((), ())), preferred_element_type=jnp.float32)

    def head(h, carry):
        r0 = pl.multiple_of(h * PEER_DKEY, PEER_DKEY)
        q1 = qt_scr[pl.ds(r0, half), :].astype(jnp.bfloat16)
        q2 = qt_scr[pl.ds(r0 + half, half), :].astype(jnp.bfloat16)
        s1 = jnp.dot(k1_ref[...], q1, preferred_element_type=jnp.float32)
        s2 = jnp.dot(k2_ref[...], q2, preferred_element_type=jnp.float32)

        def emit(exact_ties):
            rk2, lim, p1, p2, n_taken = _peer_route_head(s1, s2, exact_ties)
            rk2_ref[h] = rk2
            lim_ref[h] = lim
            p1_ref[h] = p1
            p2_ref[h] = p2
            return n_taken

        n_taken = emit(False)
        merged = jnp.max(jnp.abs(n_taken - 3.0 * PEER_TOPK)) > 0.0

        @pl.when(merged)
        def _():
            emit(True)

        return carry

    lax.fori_loop(0, PEER_HEADS, head, 0)


def _row_bf16(row):
    r16 = jnp.broadcast_to(row, (16, row.shape[1])).astype(jnp.bfloat16)
    return jnp.concatenate([r16] * (PEER_NKEYS // 16), axis=0)


def _peer_weights(rk2_ref, lim_ref, p1_ref, p2_ref, i1, tb):
    w = jnp.zeros((PEER_NKEYS, tb), jnp.bfloat16)
    for h in range(PEER_HEADS):
        p2h = p2_ref[h]
        sel = rk2_ref[h] <= _row_bf16(lim_ref[h, pl.ds(i1, 1), :])
        w = w + _row_bf16(p1_ref[h, pl.ds(i1, 1), :]) * jnp.where(sel, p2h, jnp.zeros_like(p2h))
    return w


def _peer_dense_kernel(x_ref, rk2_ref, lim_ref, p1_ref, p2_ref, u0_ref, ua_ref, ub_ref, vt_ref, g_ref, b_ref,
                       o_ref, xb_scr, yt_scr, ht0_scr, ht1_scr):
    k = pl.program_id(1)
    n_i1 = PEER_EB // PEER_NKEYS
    tb = x_ref.shape[0]
    nt = (((1,), (1,)), ((), ()))
    f32 = jnp.float32

    @pl.when(k == 0)
    def _():
        xb_scr[...] = x_ref[...].astype(jnp.bfloat16)
        yt_scr[...] = jnp.zeros_like(yt_scr)
        ht0_scr[...] = lax.dot_general(u0_ref[...], xb_scr[...], nt, preferred_element_type=f32)

    xb = xb_scr[...]
    n_sub = 2
    sub = PEER_EB // n_sub

    def second_half(ht_scr, blk, vt_off, acc):
        for j in range(n_sub):
            pieces = []
            for cc in range(sub // PEER_NKEYS):
                i1 = blk * n_i1 + j * (sub // PEER_NKEYS) + cc
                w = _peer_weights(rk2_ref, lim_ref, p1_ref, p2_ref, i1, tb)
                r0 = j * sub + cc * PEER_NKEYS
                hc = ht_scr[r0:r0 + PEER_NKEYS, :]
                gelu = 0.5 * hc * (1.0 + lax.erf(hc * (2.0 ** -0.5)))
                pieces.append(w * gelu.astype(jnp.bfloat16))
            at = jnp.concatenate(pieces, axis=0)
            c0 = vt_off + j * sub
            acc = acc + jnp.dot(vt_ref[:, c0:c0 + sub], at, preferred_element_type=f32)
        return acc

    acc = yt_scr[...]
    ht1_scr[...] = lax.dot_general(ua_ref[...], xb, nt, preferred_element_type=f32)
    acc = second_half(ht0_scr, 2 * k, 0, acc)
    ht0_scr[...] = lax.dot_general(ub_ref[...], xb, nt, preferred_element_type=f32)
    acc = second_half(ht1_scr, 2 * k + 1, PEER_EB, acc)
    yt_scr[...] = acc

    @pl.when(k == pl.num_programs(1) - 1)
    def _():
        z = ALPHA * x_ref[...] + yt_scr[...].T
        o_ref[...] = _layer_norm_rows(z, g_ref[...], b_ref[...])


def peer_ln(x, w_pq, sub_k1, sub_k2, peer_u, peer_v, ln_g, ln_b):
    T = x.shape[0]
    assert T % PEER_TB == 0 and T % PEER_RT == 0
    nt = T // PEER_TB
    half = PEER_DKEY // 2
    wq_t = w_pq.reshape(D_MODEL, PEER_HEADS * PEER_DKEY).T.astype(jnp.bfloat16)
    sshape = jax.ShapeDtypeStruct((PEER_HEADS, PEER_NKEYS, T), jnp.float32)
    sspec = pl.BlockSpec((PEER_HEADS, PEER_NKEYS, PEER_RT), lambda j: (0, 0, j))
    rk2, lim, p1, p2 = pl.pallas_call(
        _peer_route_kernel,
        grid=(T // PEER_RT,),
        in_specs=[pl.BlockSpec((PEER_RT, D_MODEL), lambda j: (j, 0)),
                  pl.BlockSpec((PEER_HEADS * PEER_DKEY, D_MODEL), lambda j: (0, 0)),
                  pl.BlockSpec((PEER_NKEYS, half), lambda j: (0, 0)),
                  pl.BlockSpec((PEER_NKEYS, half), lambda j: (0, 0))],
        out_specs=[sspec, sspec, sspec, sspec],
        out_shape=[jax.ShapeDtypeStruct(sshape.shape, jnp.bfloat16), sshape, sshape,
                   jax.ShapeDtypeStruct(sshape.shape, jnp.bfloat16)],
        scratch_shapes=[pltpu.VMEM((PEER_HEADS * PEER_DKEY, PEER_RT), jnp.float32)],
        compiler_params=pltpu.CompilerParams(dimension_semantics=("arbitrary",), vmem_limit_bytes=VMEM_LIMIT),
        name="peer_route",
    )(x, wq_t, sub_k1.astype(jnp.bfloat16), sub_k2.astype(jnp.bfloat16))

    u_b = peer_u.astype(jnp.bfloat16)
    vt_b = peer_v.T.astype(jnp.bfloat16)
    ne = PEER_N // PEER_EB
    assert ne % 2 == 0
    sspec2 = pl.BlockSpec((PEER_HEADS, PEER_NKEYS, PEER_TB), lambda j, k: (0, 0, j))
    return pl.pallas_call(
        _peer_dense_kernel,
        grid=(nt, ne // 2),
        in_specs=[pl.BlockSpec((PEER_TB, D_MODEL), lambda j, k: (j, 0)),
                  sspec2, sspec2, sspec2, sspec2,
                  pl.BlockSpec((PEER_EB, D_MODEL), lambda j, k: (0, 0)),
                  pl.BlockSpec((PEER_EB, D_MODEL), lambda j, k: (2 * k + 1, 0)),
                  pl.BlockSpec((PEER_EB, D_MODEL), lambda j, k: (jnp.minimum(2 * k + 2, ne - 1), 0)),
                  pl.BlockSpec((D_MODEL, 2 * PEER_EB), lambda j, k: (0, k)),
                  pl.BlockSpec((1, D_MODEL), lambda j, k: (0, 0)),
                  pl.BlockSpec((1, D_MODEL), lambda j, k: (0, 0))],
        out_specs=pl.BlockSpec((PEER_TB, D_MODEL), lambda j, k: (j, 0)),
        out_shape=jax.ShapeDtypeStruct((T, D_MODEL), jnp.float32),
        scratch_shapes=[pltpu.VMEM((PEER_TB, D_MODEL), jnp.bfloat16),
                        pltpu.VMEM((D_MODEL, PEER_TB), jnp.float32),
                        pltpu.VMEM((PEER_EB, PEER_TB), jnp.float32),
                        pltpu.VMEM((PEER_EB, PEER_TB), jnp.float32)],
        compiler_params=pltpu.CompilerParams(dimension_semantics=("arbitrary", "arbitrary"),
                                             vmem_limit_bytes=PEER_VMEM_LIMIT),
        name="peer_dense",
    )(x, rk2, lim, p1, p2, u_b, u_b, u_b, vt_b, ln_g.reshape(1, -1), ln_b.reshape(1, -1))


def kernel(x_prompt, x_sample, cache_kv_latent, cache_k_rope, state_C, state_n, state_m,
           cache_mem_k, cache_mem_v, page_table, mem_prompt, ln0_g, ln0_b, w_in, b_i, b_f,
           g_q, w_uq, g_kv, w_uk, w_uv, w_out, ln1_g, ln1_b, w_mq, w_mk, w_mv, w_mo,
           ln2_g, ln2_b, w_pq, sub_k1, sub_k2, peer_u, peer_v, ln3_g, ln3_b):
    B, S = x_prompt.shape[:2]
    NB, TQ = x_sample.shape[:2]
    past = page_table.shape[1] * PAGE_SIZE
    n_p = B * S
    l = 0
    x_all = jnp.concatenate([x_prompt.reshape(n_p, D_MODEL), x_sample.reshape(NB * TQ, D_MODEL)], axis=0)
    pos = jnp.concatenate([jnp.tile(jnp.arange(S), B), jnp.tile(past + jnp.arange(TQ), NB)])
    (xn, q_lat, q_rope, kv, kvk, kr, mq, mk, mv, gates, o_gate) = mix_in(
        x_all, pos, ln0_g, ln0_b, w_in[l], b_i[l], b_f[l], g_q[l], w_uq[l], g_kv[l], w_uk[l])

    mla_p = mla_attend_prompt(q_lat, q_rope, kvk, w_uv[l], B, S)
    mlh_p, C_p, n_pst, m_p = mlstm_prompt(mq, mk, mv, gates, B, S)

    kv_s = kv[n_p:].reshape(NB, TQ, MLA_KV_LORA)
    kr_s = kr[n_p:, :MLA_ROPE].reshape(NB, TQ, MLA_ROPE)
    mla_s = mla_attend_sample(q_lat[n_p:].reshape(NB, TQ, -1), q_rope[n_p:].reshape(NB, TQ, -1), kv_s, kr_s,
                              cache_kv_latent.reshape(cache_kv_latent.shape[1:]),
                              jnp.swapaxes(cache_k_rope.reshape(cache_k_rope.shape[1:]), 1, 2), page_table, w_uv[l])
    mlh_s, C_s, n_s, m_s = mlstm_step(mq[n_p:], mk[n_p:], mv[n_p:], gates[n_p:], state_C.reshape(state_C.shape[1:]),
                                      state_n.reshape(state_n.shape[1:]), state_m.reshape(state_m.shape[1:]), NB, TQ)

    mla_all = jnp.concatenate([mla_p, mla_s], axis=0)
    mlh_all = jnp.concatenate([mlh_p, mlh_s], axis=0)
    x1, qm = mix_out(xn, mla_all, mlh_all, o_gate, w_out[l], ln1_g[l], ln1_b[l], w_mq[l])
    mk_p, mv_p = mem_kv(mem_prompt, w_mk[l], w_mv[l])
    x2_p = mem_attend_ln(x1, qm, mk_p.reshape(B, N_MEM, D_MODEL), mv_p.reshape(B, N_MEM, D_MODEL),
                         w_mo[l], ln2_g[l], ln2_b[l], B, S)
    x2_s = mem_attend_ln(x1[n_p:], qm[n_p:], cache_mem_k.reshape(cache_mem_k.shape[1:]),
                         cache_mem_v.reshape(cache_mem_v.shape[1:]), w_mo[l], ln2_g[l], ln2_b[l], NB, TQ)
    x2 = jnp.concatenate([x2_p, x2_s], axis=0)
    x3 = peer_ln(x2, w_pq[l], sub_k1[l], sub_k2[l], peer_u[l], peer_v[l], ln3_g[l], ln3_b[l])
    st = lambda t: t[None]
    return (x3[:n_p].reshape(B, S, D_MODEL), x3[n_p:].reshape(NB, TQ, D_MODEL),
            st(kv[:n_p].reshape(B, S, MLA_KV_LORA)), st(kr[:n_p, :MLA_ROPE].reshape(B, S, MLA_ROPE)),
            st(C_p), st(n_pst), st(m_p), st(mk_p), st(mv_p),
            st(kv_s), st(kr_s), st(C_s), st(n_s), st(m_s))
```

```python
import jax, jax.numpy as jnp
from jax import lax
import numpy as np
from jax.experimental import pallas as pl
from jax.experimental.pallas import tpu as pltpu

D_MODEL = 1024
PAGE_SIZE = 128

MLA_HEADS = 8
MLA_NOPE = 64
MLA_ROPE = 32
MLA_V = 64
MLA_KV_LORA = 256
MLA_Q_LORA = 384
MLA_SCALE = (MLA_NOPE + MLA_ROPE) ** -0.5
ROPE_BASE = 10000.0
ML_HEADS = 4
ML_DH = 128
ML_CHUNK = 64
MLA_WIDTH = MLA_HEADS * MLA_V
ML_WIDTH = ML_HEADS * ML_DH
D_MIX = MLA_WIDTH + ML_WIDTH
N_MEM = 256
MEM_HEADS = 4
MEM_HD = D_MODEL // MEM_HEADS
PEER_HEADS = 8
PEER_NKEYS = 128
PEER_N = PEER_NKEYS * PEER_NKEYS
PEER_DKEY = 128
PEER_TOPK = 16
LN_EPS = 1e-5
RMS_EPS = 1e-6
DEPTH = 1
ALPHA = (2 * DEPTH) ** 0.25
NEG_INF = float('-inf')

VMEM_LIMIT = 48 * 1024 * 1024

IN_PAD = 2944
OFF_CQ, OFF_CKV, OFF_KR, OFF_MQ, OFF_MK, OFF_MV, OFF_G, OFF_O = 0, 384, 640, 768, 1280, 1792, 2304, 2432
MIX_TM = 256


def _layer_norm_rows(z, g, b):
    mu = jnp.mean(z, axis=-1, keepdims=True)
    zc = z - mu
    var = jnp.mean(zc * zc, axis=-1, keepdims=True)
    return zc * lax.rsqrt(var + LN_EPS) * g + b


def _rope_lanes(x, cos, sin_signed):
    n = x.shape[1]
    lane = lax.broadcasted_iota(jnp.int32, x.shape, 1)
    partner = jnp.where((lane & 31) < 16, pltpu.roll(x, n - 16, axis=1), pltpu.roll(x, 16, axis=1))
    return x * cos + partner * sin_signed


def _mix_in_kernel(x_ref, g0_ref, b0_ref, win_ref, gq_ref, wuq_ref, wuk_ref, gkv_ref, cos_ref, sin_ref, gb_ref,
                   xn_ref, ql_ref, qr_ref, kv_ref, kvk_ref, kr_ref, mq_ref, mk_ref, mv_ref, gate_ref, og_ref):
    xn = _layer_norm_rows(x_ref[...], g0_ref[...], b0_ref[...])
    xn_ref[...] = xn
    z = jnp.dot(xn.astype(jnp.bfloat16), win_ref[...], preferred_element_type=jnp.float32)
    cos = cos_ref[...]
    sin = sin_ref[...]
    cq = z[:, OFF_CQ:OFF_CQ + MLA_Q_LORA]
    cq = cq * lax.rsqrt(jnp.mean(cq * cq, axis=-1, keepdims=True) + RMS_EPS) * gq_ref[...]
    q = jnp.dot(cq.astype(jnp.bfloat16), wuq_ref[...], preferred_element_type=jnp.float32)
    n_nope = MLA_HEADS * MLA_NOPE
    qrope = _rope_lanes(q[:, n_nope:], jnp.concatenate([cos, cos], axis=1), jnp.concatenate([sin, sin], axis=1))
    qr_ref[...] = qrope.astype(jnp.bfloat16)
    ql_ref[...] = jnp.dot(q[:, :n_nope].astype(jnp.bfloat16), wuk_ref[...],
                          preferred_element_type=jnp.float32).astype(jnp.bfloat16)
    ckv = z[:, OFF_CKV:OFF_CKV + MLA_KV_LORA]
    kv = ckv * lax.rsqrt(jnp.mean(ckv * ckv, axis=-1, keepdims=True) + RMS_EPS) * gkv_ref[...]
    kv_ref[...] = kv
    kr = _rope_lanes(z[:, OFF_KR:OFF_KR + 128], cos, sin)
    kr_ref[...] = kr
    krt = kr + pltpu.roll(kr, 32, axis=1) + pltpu.roll(kr, 64, axis=1) + pltpu.roll(kr, 96, axis=1)
    kvk_ref[...] = jnp.concatenate([kv, krt, krt], axis=1).astype(jnp.bfloat16)
    mq_ref[...] = z[:, OFF_MQ:OFF_MQ + ML_WIDTH]
    mk_ref[...] = z[:, OFF_MK:OFF_MK + ML_WIDTH] * (ML_DH ** -0.5)
    mv_ref[...] = z[:, OFF_MV:OFF_MV + ML_WIDTH]
    g = z[:, OFF_G:OFF_G + 128] + gb_ref[...]
    lane = lax.broadcasted_iota(jnp.int32, g.shape, 1)
    gate_ref[...] = jnp.where(lane < ML_HEADS, g, jax.nn.log_sigmoid(g))
    og_ref[...] = jax.nn.sigmoid(z[:, OFF_O:OFF_O + ML_WIDTH])


def mix_in(x, pos, ln0_g, ln0_b, w_in, b_i, b_f, g_q, w_uq, g_kv, w_uk):
    T = x.shape[0]
    f32, bf = jnp.float32, jnp.bfloat16
    zc = lambda n: jnp.zeros((D_MODEL, n), f32)
    win_p = jnp.concatenate([w_in[:, :672], zc(96), w_in[:, 672:2208], w_in[:, 2208:2216], zc(120), w_in[:, 2216:]],
                            axis=1).astype(bf)
    assert win_p.shape[1] == IN_PAD
    wuq_p = jnp.concatenate([w_uq[:, :, :MLA_NOPE].reshape(MLA_Q_LORA, -1),
                             w_uq[:, :, MLA_NOPE:].reshape(MLA_Q_LORA, -1)], axis=1).astype(bf)
    hh = jnp.arange(MLA_HEADS)
    wuk_blk = jnp.zeros((MLA_HEADS, MLA_NOPE, MLA_HEADS, MLA_KV_LORA), f32)
    wuk_blk = wuk_blk.at[hh, :, hh, :].set(jnp.transpose(w_uk, (1, 2, 0)))
    wuk_blk = wuk_blk.reshape(MLA_HEADS * MLA_NOPE, MLA_HEADS * MLA_KV_LORA).astype(bf)
    inv = 1.0 / (ROPE_BASE ** (jnp.arange(0, MLA_ROPE, 2, dtype=f32) / MLA_ROPE))
    ang = pos.astype(f32)[:, None] * inv[None, :]
    c, s = jnp.cos(ang), jnp.sin(ang)
    cos128 = jnp.tile(jnp.concatenate([c, c], axis=1), (1, 4))
    sin128 = jnp.tile(jnp.concatenate([-s, s], axis=1), (1, 4))
    gbias = jnp.concatenate([b_i, b_f, jnp.zeros((120,), f32)]).reshape(1, 128)
    row = lambda n: pl.BlockSpec((MIX_TM, n), lambda i: (i, 0))
    full = lambda a: pl.BlockSpec(a.shape, lambda i: (0,) * a.ndim)
    ins = [x, ln0_g.reshape(1, -1), ln0_b.reshape(1, -1), win_p, g_q.reshape(1, -1), wuq_p, wuk_blk,
           g_kv.reshape(1, -1), cos128, sin128, gbias]
    in_specs = [row(D_MODEL)] + [full(a) for a in ins[1:8]] + [row(128), row(128), full(gbias)]
    outs = [(D_MODEL, f32), (MLA_HEADS * MLA_KV_LORA, bf), (MLA_HEADS * MLA_ROPE, bf), (MLA_KV_LORA, f32),
            (MLA_KV_LORA + MLA_HEADS * MLA_ROPE, bf), (128, f32), (ML_WIDTH, f32), (ML_WIDTH, f32),
            (ML_WIDTH, f32), (128, f32), (ML_WIDTH, f32)]
    return pl.pallas_call(
        _mix_in_kernel,
        grid=(T // MIX_TM,),
        in_specs=in_specs,
        out_specs=[row(n) for n, _ in outs],
        out_shape=[jax.ShapeDtypeStruct((T, n), dt) for n, dt in outs],
        compiler_params=pltpu.CompilerParams(dimension_semantics=("arbitrary",), vmem_limit_bytes=VMEM_LIMIT),
        name="mix_in",
    )(*ins)


ATT_BQ = 128
ATT_BK = 512


def _mla_prompt_kernel(ql_ref, qr_ref, kvk_ref, wuv_ref, o_ref, acc_scr, m_scr, l_scr):
    qi = pl.program_id(1)
    ql = jnp.concatenate([ql_ref[:, h * MLA_KV_LORA:(h + 1) * MLA_KV_LORA] for h in range(MLA_HEADS)], axis=0)
    qr_all = qr_ref[...]
    lane_head = lax.broadcasted_iota(jnp.int32, qr_all.shape, 1) // MLA_ROPE
    qr = jnp.concatenate([jnp.where(lane_head == h, qr_all, jnp.zeros_like(qr_all)) for h in range(MLA_HEADS)],
                         axis=0)
    q = jnp.concatenate([ql, qr], axis=1)
    acc_scr[...] = jnp.zeros_like(acc_scr)
    m_scr[...] = jnp.full_like(m_scr, NEG_INF)
    l_scr[...] = jnp.zeros_like(l_scr)
    nt = (((1,), (1,)), ((), ()))
    n_last = (qi * ATT_BQ) // ATT_BK
    rows = q.shape[0]
    tok = qi * ATT_BQ + (lax.broadcasted_iota(jnp.int32, (rows, ATT_BK), 0) & (ATT_BQ - 1))
    col = lax.broadcasted_iota(jnp.int32, (rows, ATT_BK), 1)

    def scores(kj):
        k0 = pl.multiple_of(kj * ATT_BK, ATT_BK)
        s = lax.dot_general(q, kvk_ref[pl.ds(k0, ATT_BK), :], nt, preferred_element_type=jnp.float32) * MLA_SCALE
        return jnp.where(col + k0 <= tok, s, NEG_INF)

    def accumulate(kj, s):
        k0 = pl.multiple_of(kj * ATT_BK, ATT_BK)
        kvb = kvk_ref[pl.ds(k0, ATT_BK), :MLA_KV_LORA]
        m_old = m_scr[...]
        m_new = jnp.maximum(m_old, jnp.max(s, axis=1, keepdims=True))
        alpha = jnp.exp(m_old - m_new)
        p = jnp.exp(s - m_new)
        l_scr[...] = alpha * l_scr[...] + jnp.sum(p, axis=1, keepdims=True)
        acc_scr[...] = alpha * acc_scr[...] + jnp.dot(p.astype(jnp.bfloat16), kvb,
                                                      preferred_element_type=jnp.float32)
        m_scr[...] = m_new

    def body(kj, s_cur):
        s_next = scores(kj + 1)
        accumulate(kj, s_cur)
        return s_next

    s_last = lax.fori_loop(0, n_last, body, scores(0))
    accumulate(n_last, s_last)
    o = (acc_scr[...] / l_scr[...]).astype(jnp.bfloat16)
    r = jnp.dot(o, wuv_ref[...], preferred_element_type=jnp.float32)
    col_head = lax.broadcasted_iota(jnp.int32, (ATT_BQ, MLA_WIDTH), 1) // MLA_V
    out = jnp.zeros((ATT_BQ, MLA_WIDTH), jnp.float32)
    for h in range(MLA_HEADS):
        out = out + jnp.where(col_head == h, r[h * ATT_BQ:(h + 1) * ATT_BQ], 0.0)
    o_ref[...] = out


def mla_attend_prompt(q_lat, q_rope, kvk, w_uv, n_seq, seq):
    assert seq % ATT_BK == 0 and ATT_BK % ATT_BQ == 0
    H, C = MLA_HEADS, MLA_KV_LORA
    nq = seq // ATT_BQ
    rows = H * ATT_BQ
    return pl.pallas_call(
        _mla_prompt_kernel,
        grid=(n_seq, nq),
        in_specs=[pl.BlockSpec((ATT_BQ, H * C), lambda b, i: (b * nq + i, 0)),
                  pl.BlockSpec((ATT_BQ, H * MLA_ROPE), lambda b, i: (b * nq + i, 0)),
                  pl.BlockSpec((seq, C + H * MLA_ROPE), lambda b, i: (b, 0)),
                  pl.BlockSpec((C, H * MLA_V), lambda b, i: (0, 0))],
        out_specs=pl.BlockSpec((ATT_BQ, H * MLA_V), lambda b, i: (b * nq + i, 0)),
        out_shape=jax.ShapeDtypeStruct((n_seq * seq, H * MLA_V), jnp.float32),
        scratch_shapes=[pltpu.VMEM((rows, C), jnp.float32),
                        pltpu.VMEM((rows, 1), jnp.float32),
                        pltpu.VMEM((rows, 1), jnp.float32)],
        compiler_params=pltpu.CompilerParams(dimension_semantics=("arbitrary", "arbitrary"),
                                             vmem_limit_bytes=VMEM_LIMIT),
        name="mla_prompt",
    )(q_lat, q_rope, kvk, w_uv.reshape(C, H * MLA_V).astype(jnp.bfloat16))


SMP_KC = 2048


def _mla_sample_kernel(pt_ref, ql_ref, qr_ref, kvn_ref, krn_ref, wuv_ref, lat_hbm, rope_hbm, o_ref,
                       lat_buf, rope_buf, lat_bf, sem_lat, sem_rope):
    b = pl.program_id(0)
    nb = pl.num_programs(0)
    n_pages = pt_ref.shape[1]
    rows = ql_ref.shape[0]
    tq = kvn_ref.shape[0]
    n_keys = n_pages * PAGE_SIZE

    def page_copies(seq, slot, p):
        page = pt_ref[seq, p]
        dst = pl.ds(p * PAGE_SIZE, PAGE_SIZE)
        return (pltpu.make_async_copy(lat_hbm.at[page], lat_buf.at[slot, dst], sem_lat.at[slot]),
                pltpu.make_async_copy(rope_hbm.at[page], rope_buf.at[slot, :, dst], sem_rope.at[slot]))

    def start_fetch(seq, slot):
        def body(p, c):
            for cp in page_copies(seq, slot, p):
                cp.start()
            return c
        lax.fori_loop(0, n_pages, body, 0)

    def wait_fetch(seq, slot):
        def body(p, c):
            for cp in page_copies(seq, slot, p):
                cp.wait()
            return c
        lax.fori_loop(0, n_pages, body, 0)

    slot = b % 2

    @pl.when(b == 0)
    def _():
        start_fetch(0, 0)

    @pl.when(b + 1 < nb)
    def _():
        start_fetch(b + 1, 1 - slot)

    wait_fetch(b, slot)

    bf, f32 = jnp.bfloat16, jnp.float32
    ql = ql_ref[...]
    qr = qr_ref[...]
    nt = (((1,), (1,)), ((), ()))
    parts = []
    for c in range(n_keys // SMP_KC):
        r = pl.ds(c * SMP_KC, SMP_KC)
        lb = lat_buf[slot, r, :].astype(bf)
        rb = rope_buf[slot, :, r].astype(bf)
        lat_bf[r, :] = lb
        parts.append(lax.dot_general(ql, lb, nt, preferred_element_type=f32)
                     + jnp.dot(qr, rb, preferred_element_type=f32))
    s_past = jnp.concatenate(parts, axis=1) * MLA_SCALE
    qlf, qrf = ql.astype(f32), qr.astype(f32)
    kvn = kvn_ref[...].astype(f32)
    krn = krn_ref[...].astype(f32)
    tok = lax.broadcasted_iota(jnp.int32, (rows, 1), 0) % tq
    s_new = []
    for j in range(tq):
        sj = (jnp.sum(qlf * kvn[j:j + 1, :], axis=1, keepdims=True)
              + jnp.sum(qrf * krn[j:j + 1, :], axis=1, keepdims=True)) * MLA_SCALE
        s_new.append(jnp.where(tok >= j, sj, NEG_INF))
    m = jnp.max(s_past, axis=1, keepdims=True)
    for sj in s_new:
        m = jnp.maximum(m, sj)
    p_past = jnp.exp(s_past - m)
    p_new = [jnp.exp(sj - m) for sj in s_new]
    l = jnp.sum(p_past, axis=1, keepdims=True)
    for pj in p_new:
        l = l + pj
    inv = 1.0 / l
    o = jnp.dot((p_past * inv).astype(bf), lat_bf[...], preferred_element_type=f32)
    for j in range(tq):
        o = o + (p_new[j] * inv).astype(bf).astype(f32) * kvn[j:j + 1, :]
    r = jnp.dot(o.astype(bf), wuv_ref[...], preferred_element_type=f32)
    col_head = lax.broadcasted_iota(jnp.int32, (tq, MLA_WIDTH), 1) // MLA_V
    out = jnp.zeros((tq, MLA_WIDTH), f32)
    for h in range(MLA_HEADS):
        out = out + jnp.where(col_head == h, r[h * tq:(h + 1) * tq, :], 0.0)
    o_ref[...] = out


def mla_attend_sample(q_lat, q_rope, kv_new, kr_new, pool_lat, pool_rope_t, page_table, w_uv):
    NB, T = q_lat.shape[:2]
    H, C, R = MLA_HEADS, MLA_KV_LORA, MLA_ROPE
    bf = jnp.bfloat16
    n_pages = page_table.shape[1]
    n_keys = n_pages * PAGE_SIZE
    assert n_keys % SMP_KC == 0
    ql = q_lat.reshape(NB, T, H, C).transpose(0, 2, 1, 3).reshape(NB, H * T, C)
    qr = q_rope.reshape(NB, T, H, R).transpose(0, 2, 1, 3).reshape(NB, H * T, R)
    seq = lambda n, w: pl.BlockSpec((None, n, w), lambda b, pt: (b, 0, 0))
    grid_spec = pltpu.PrefetchScalarGridSpec(
        num_scalar_prefetch=1,
        grid=(NB,),
        in_specs=[seq(H * T, C), seq(H * T, R), seq(T, C), seq(T, R),
                  pl.BlockSpec((C, H * MLA_V), lambda b, pt: (0, 0)),
                  pl.BlockSpec(memory_space=pl.ANY), pl.BlockSpec(memory_space=pl.ANY)],
        out_specs=seq(T, H * MLA_V),
        scratch_shapes=[pltpu.VMEM((2, n_keys, C), jnp.float32), pltpu.VMEM((2, R, n_keys), jnp.float32),
                        pltpu.VMEM((n_keys, C), bf),
                        pltpu.SemaphoreType.DMA((2,)), pltpu.SemaphoreType.DMA((2,))],
    )
    out = pl.pallas_call(
        _mla_sample_kernel,
        grid_spec=grid_spec,
        out_shape=jax.ShapeDtypeStruct((NB, T, H * MLA_V), jnp.float32),
        compiler_params=pltpu.CompilerParams(dimension_semantics=("arbitrary",), vmem_limit_bytes=VMEM_LIMIT),
        name="mla_sample",
    )(page_table, ql, qr, kv_new.astype(bf), kr_new.astype(bf), w_uv.reshape(C, H * MLA_V).astype(bf), pool_lat, pool_rope_t)
    return out.reshape(NB * T, H * MLA_V)


def _split3(x):
    hi = x.astype(jnp.bfloat16)
    r1 = x - hi.astype(jnp.float32)
    mid = r1.astype(jnp.bfloat16)
    lo = (r1 - mid.astype(jnp.float32)).astype(jnp.bfloat16)
    return hi, mid, lo


def _mlstm_chunk(q_all, k_all, v_all, gates, c_refs, n_refs, m_refs):
    L = q_all.shape[0]
    bf, f32 = jnp.bfloat16, jnp.float32
    row_t = lax.broadcasted_iota(jnp.int32, (L, L), 0)
    col_s = lax.broadcasted_iota(jnp.int32, (L, L), 1)
    causal = col_s <= row_t
    tril = jnp.where(causal, 1.0, 0.0).astype(bf)
    cum = sum(jnp.dot(tril, part, preferred_element_type=f32) for part in _split3(gates))
    gates_t = gates.T
    cum_t = cum.T
    nt = (((1,), (1,)), ((), ()))
    outs = []
    for h in range(ML_HEADS):
        c = slice(h * ML_DH, (h + 1) * ML_DH)
        q, k, v = q_all[:, c], k_all[:, c], v_all[:, c]
        C, n, m = c_refs[0](h), n_refs[0](h), m_refs[0](h)
        ig_col = gates[:, h:h + 1]
        b_col = cum[:, ML_HEADS + h:ML_HEADS + h + 1]
        ig_row = gates_t[h:h + 1, :]
        b_row = cum_t[ML_HEADS + h:ML_HEADS + h + 1, :]
        D = jnp.where(causal, b_col - b_row + ig_row, NEG_INF)
        inter = b_col + m
        m_t = jnp.maximum(inter, jnp.max(D, axis=1, keepdims=True))
        qb, kb, vb = q.astype(bf), k.astype(bf), v.astype(bf)
        A = jnp.exp(D - m_t) * lax.dot_general(qb, kb, nt, preferred_element_type=f32)
        w_inter = jnp.exp(inter - m_t)
        num = w_inter * jnp.dot(qb, C.astype(bf), preferred_element_type=f32) \
            + jnp.dot(A.astype(bf), vb, preferred_element_type=f32)
        qn = jnp.sum(qb.astype(f32) * n.astype(bf).astype(f32), axis=1, keepdims=True)
        den = w_inter * qn + jnp.sum(A, axis=1, keepdims=True)
        outs.append(num / jnp.maximum(jnp.abs(den), jnp.exp(-m_t)))
        b_end = b_col[L - 1:L, :]
        m_new = jnp.maximum(b_end + m, jnp.max(b_end - b_row + ig_row, axis=1, keepdims=True))
        a_prev = jnp.exp(b_end + m - m_new)
        kw = k * jnp.exp(b_end - b_col + ig_col - m_new)
        c_refs[1](h, a_prev * C + jnp.dot(kw.T.astype(bf), vb, preferred_element_type=f32))
        n_refs[1](h, a_prev * n + jnp.sum(kw, axis=0, keepdims=True))
        m_refs[1](h, m_new)
    return jnp.concatenate(outs, axis=1)


def _state_access(c_get, n_get, m_get, c_set, n_set, m_set):
    c_refs = (lambda h: c_get[h], lambda h, val: c_set.__setitem__(h, val))
    n_refs = (lambda h: n_get[h:h + 1, :], lambda h, val: n_set.__setitem__((slice(h, h + 1), slice(None)), val))
    m_refs = (lambda h: m_get[h:h + 1, 0:1],
              lambda h, val: m_set.__setitem__((slice(h, h + 1), slice(None)), jnp.broadcast_to(val, (1, 128))))
    return c_refs, n_refs, m_refs


def _mlstm_prompt_kernel(q_ref, k_ref, v_ref, g_ref, h_ref, c_out, n_out, m_out, c_scr, n_scr, m_scr):
    j = pl.program_id(1)

    @pl.when(j == 0)
    def _():
        c_scr[...] = jnp.zeros_like(c_scr)
        n_scr[...] = jnp.zeros_like(n_scr)
        m_scr[...] = jnp.zeros_like(m_scr)

    h_ref[...] = _mlstm_chunk(q_ref[...], k_ref[...], v_ref[...], g_ref[...],
                              *_state_access(c_scr, n_scr, m_scr, c_scr, n_scr, m_scr))

    @pl.when(j == pl.num_programs(1) - 1)
    def _():
        c_out[...] = c_scr[...]
        n_out[...] = n_scr[0:ML_HEADS, :]
        m_out[...] = m_scr[0:ML_HEADS, :]


def mlstm_prompt(mq, mk, mv, gates, n_seq, seq):
    nc = seq // ML_CHUNK
    f32 = jnp.float32
    tok = lambda w: pl.BlockSpec((ML_CHUNK, w), lambda b, j: (b * nc + j, 0))
    h, C, n, m = pl.pallas_call(
        _mlstm_prompt_kernel,
        grid=(n_seq, nc),
        in_specs=[tok(ML_WIDTH), tok(ML_WIDTH), tok(ML_WIDTH), tok(128)],
        out_specs=[tok(ML_WIDTH),
                   pl.BlockSpec((None, ML_HEADS, ML_DH, ML_DH), lambda b, j: (b, 0, 0, 0)),
                   pl.BlockSpec((None, ML_HEADS, ML_DH), lambda b, j: (b, 0, 0)),
                   pl.BlockSpec((None, ML_HEADS, 128), lambda b, j: (b, 0, 0))],
        out_shape=[jax.ShapeDtypeStruct((n_seq * seq, ML_WIDTH), f32),
                   jax.ShapeDtypeStruct((n_seq, ML_HEADS, ML_DH, ML_DH), f32),
                   jax.ShapeDtypeStruct((n_seq, ML_HEADS, ML_DH), f32),
                   jax.ShapeDtypeStruct((n_seq, ML_HEADS, 128), f32)],
        scratch_shapes=[pltpu.VMEM((ML_HEADS, ML_DH, ML_DH), f32), pltpu.VMEM((8, ML_DH), f32),
                        pltpu.VMEM((8, 128), f32)],
        compiler_params=pltpu.CompilerParams(dimension_semantics=("arbitrary", "arbitrary"),
                                             vmem_limit_bytes=VMEM_LIMIT),
        name="mlstm_prompt",
    )(mq, mk, mv, gates)
    return h, C, n, m[:, :, 0]


def _mlstm_step_kernel(q_ref, k_ref, v_ref, g_ref, c_in, n_in, m_in, h_ref, c_out, n_out, m_out):
    h_ref[...] = _mlstm_chunk(q_ref[...], k_ref[...], v_ref[...], g_ref[...],
                              *_state_access(c_in, n_in, m_in, c_out, n_out, m_out))


def mlstm_step(mq, mk, mv, gates, state_C, state_n, state_m, n_seq, rows):
    f32 = jnp.float32
    tok = lambda w: pl.BlockSpec((None, rows, w), lambda b: (b, 0, 0))
    st_c = pl.BlockSpec((None, ML_HEADS, ML_DH, ML_DH), lambda b: (b, 0, 0, 0))
    st_n = pl.BlockSpec((None, ML_HEADS, ML_DH), lambda b: (b, 0, 0))
    st_m = pl.BlockSpec((None, ML_HEADS, 128), lambda b: (b, 0, 0))
    r3 = lambda t: t.reshape(n_seq, rows, t.shape[-1])
    m_in = jnp.broadcast_to(state_m[:, :, None], (n_seq, ML_HEADS, 128))
    h, C, n, m = pl.pallas_call(
        _mlstm_step_kernel,
        grid=(n_seq,),
        in_specs=[tok(ML_WIDTH), tok(ML_WIDTH), tok(ML_WIDTH), tok(128), st_c, st_n, st_m],
        out_specs=[tok(ML_WIDTH), st_c, st_n, st_m],
        out_shape=[jax.ShapeDtypeStruct((n_seq, rows, ML_WIDTH), f32),
                   jax.ShapeDtypeStruct((n_seq, ML_HEADS, ML_DH, ML_DH), f32),
                   jax.ShapeDtypeStruct((n_seq, ML_HEADS, ML_DH), f32),
                   jax.ShapeDtypeStruct((n_seq, ML_HEADS, 128), f32)],
        compiler_params=pltpu.CompilerParams(dimension_semantics=("arbitrary",), vmem_limit_bytes=VMEM_LIMIT),
        name="mlstm_step",
    )(r3(mq), r3(mk), r3(mv), r3(gates), state_C, state_n, m_in)
    return h.reshape(n_seq * rows, ML_WIDTH), C, n, m[:, :, 0]


POST_TM = 256


def _mix_out_kernel(xn_ref, mla_ref, mlh_ref, og_ref, wout_ref, g_ref, b_ref, wmq_ref, x1_ref, qm_ref):
    mixed = jnp.concatenate([mla_ref[...], og_ref[...] * mlh_ref[...]], axis=1).astype(jnp.bfloat16)
    mix = jnp.dot(mixed, wout_ref[...], preferred_element_type=jnp.float32)
    x1 = _layer_norm_rows(ALPHA * xn_ref[...] + mix, g_ref[...], b_ref[...])
    x1_ref[...] = x1
    qm_ref[...] = jnp.dot(x1.astype(jnp.bfloat16), wmq_ref[...],
                          preferred_element_type=jnp.float32).astype(jnp.bfloat16)


def mix_out(xn, mla_o, ml_h, o_gate, w_out, ln_g, ln_b, w_mq):
    T = xn.shape[0]
    bf = jnp.bfloat16
    row = lambda n: pl.BlockSpec((POST_TM, n), lambda i: (i, 0))
    full = lambda shape: pl.BlockSpec(shape, lambda i: (0,) * len(shape))
    return pl.pallas_call(
        _mix_out_kernel,
        grid=(T // POST_TM,),
        in_specs=[row(D_MODEL), row(MLA_WIDTH), row(ML_WIDTH), row(ML_WIDTH), full((D_MIX, D_MODEL)),
                  full((1, D_MODEL)), full((1, D_MODEL)), full((D_MODEL, D_MODEL))],
        out_specs=[row(D_MODEL), row(D_MODEL)],
        out_shape=[jax.ShapeDtypeStruct((T, D_MODEL), jnp.float32), jax.ShapeDtypeStruct((T, D_MODEL), bf)],
        compiler_params=pltpu.CompilerParams(dimension_semantics=("arbitrary",), vmem_limit_bytes=VMEM_LIMIT),
        name="mix_out",
    )(xn, mla_o, ml_h, o_gate, w_out.astype(bf), ln_g.reshape(1, -1), ln_b.reshape(1, -1),
      w_mq.reshape(D_MODEL, D_MODEL).astype(bf))


def _mem_attn_kernel(x1_ref, qm_ref, mk_ref, mv_ref, wmo_ref, g_ref, b_ref, x2_ref):
    q = qm_ref[...]
    nt = (((1,), (1,)), ((), ()))
    outs = []
    for h in range(MEM_HEADS):
        c = slice(h * MEM_HD, (h + 1) * MEM_HD)
        mk, mv = mk_ref[:, c].astype(jnp.bfloat16), mv_ref[:, c].astype(jnp.bfloat16)
        s = lax.dot_general(q[:, c], mk, nt, preferred_element_type=jnp.float32) * (MEM_HD ** -0.5)
        m = jnp.max(s, axis=-1, keepdims=True)
        p = jnp.exp(s - m)
        p = p / jnp.sum(p, axis=-1, keepdims=True)
        outs.append(jnp.dot(p.astype(jnp.bfloat16), mv, preferred_element_type=jnp.float32))
    o = jnp.concatenate(outs, axis=1).astype(jnp.bfloat16)
    att = jnp.dot(o, wmo_ref[...], preferred_element_type=jnp.float32)
    x2_ref[...] = _layer_norm_rows(ALPHA * x1_ref[...] + att, g_ref[...], b_ref[...])


def _mem_attn_sample_kernel(x1_ref, qm_ref, mk_ref, mv_ref, wmo_ref, g_ref, b_ref, x2_ref):
    tq = x1_ref.shape[0]
    bf, f32 = jnp.bfloat16, jnp.float32
    q = qm_ref[...]
    qs = jnp.concatenate([q[:, h * MEM_HD:(h + 1) * MEM_HD] for h in range(MEM_HEADS)], axis=0)
    kall = mk_ref[...].reshape(N_MEM * MEM_HEADS, MEM_HD).astype(bf)
    vall = mv_ref[...].reshape(N_MEM * MEM_HEADS, MEM_HD).astype(bf)
    s = lax.dot_general(qs, kall, (((1,), (1,)), ((), ())), preferred_element_type=f32) * (MEM_HD ** -0.5)
    row_h = lax.broadcasted_iota(jnp.int32, s.shape, 0) // tq
    col_h = lax.broadcasted_iota(jnp.int32, s.shape, 1) % MEM_HEADS
    s = jnp.where(row_h == col_h, s, NEG_INF)
    p = jnp.exp(s - jnp.max(s, axis=-1, keepdims=True))
    p = p / jnp.sum(p, axis=-1, keepdims=True)
    o = jnp.dot(p.astype(bf), vall, preferred_element_type=f32)
    o = jnp.concatenate([o[h * tq:(h + 1) * tq] for h in range(MEM_HEADS)], axis=1).astype(bf)
    att = jnp.dot(o, wmo_ref[...], preferred_element_type=f32)
    x2_ref[...] = _layer_norm_rows(ALPHA * x1_ref[...] + att, g_ref[...], b_ref[...])


def mem_attend_ln(x1, qm, mem_k, mem_v, w_mo, ln_g, ln_b, n_seq, rows_per_seq):
    bf = jnp.bfloat16
    wmo = w_mo.reshape(D_MODEL, D_MODEL).astype(bf)
    g, b = ln_g.reshape(1, -1), ln_b.reshape(1, -1)
    cp = pltpu.CompilerParams(dimension_semantics=("arbitrary",) * 2, vmem_limit_bytes=VMEM_LIMIT)
    mem = pl.BlockSpec((None, N_MEM, D_MODEL), lambda s, i: (s, 0, 0))
    full = lambda shape: pl.BlockSpec(shape, lambda s, i: (0,) * len(shape))
    if rows_per_seq % POST_TM == 0:
        nb = rows_per_seq // POST_TM
        tok = pl.BlockSpec((POST_TM, D_MODEL), lambda s, i: (s * nb + i, 0))
        return pl.pallas_call(
            _mem_attn_kernel, grid=(n_seq, nb),
            in_specs=[tok, tok, mem, mem, full((D_MODEL, D_MODEL)), full((1, D_MODEL)), full((1, D_MODEL))],
            out_specs=tok, out_shape=jax.ShapeDtypeStruct((n_seq * rows_per_seq, D_MODEL), jnp.float32),
            compiler_params=cp, name="mem_attn_prompt",
        )(x1, qm, mem_k, mem_v, wmo, g, b)
    mem = pl.BlockSpec((None, N_MEM, MEM_HEADS, MEM_HD), lambda s, i: (s, 0, 0, 0))
    x3 = x1.reshape(n_seq, rows_per_seq, D_MODEL)
    q3 = qm.reshape(n_seq, rows_per_seq, D_MODEL)
    tok = pl.BlockSpec((None, rows_per_seq, D_MODEL), lambda s, i: (s, 0, 0))
    out = pl.pallas_call(
        _mem_attn_sample_kernel, grid=(n_seq, 1),
        in_specs=[tok, tok, mem, mem, full((D_MODEL, D_MODEL)), full((1, D_MODEL)), full((1, D_MODEL))],
        out_specs=tok, out_shape=jax.ShapeDtypeStruct(x3.shape, jnp.float32),
        compiler_params=cp, name="mem_attn_sample",
    )(x3, q3, mem_k, mem_v, wmo, g, b)
    return out.reshape(x1.shape)


def _mem_kv_kernel(m_ref, w_ref, o_ref):
    o_ref[...] = jnp.dot(m_ref[...].astype(jnp.bfloat16), w_ref[...], preferred_element_type=jnp.float32)


def mem_kv(mem, w_mk, w_mv):
    B = mem.shape[0]
    w = jnp.concatenate([w_mk.reshape(D_MODEL, D_MODEL), w_mv.reshape(D_MODEL, D_MODEL)], axis=1).astype(jnp.bfloat16)
    out = pl.pallas_call(
        _mem_kv_kernel, grid=(B,),
        in_specs=[pl.BlockSpec((N_MEM, D_MODEL), lambda i: (i, 0)), pl.BlockSpec((D_MODEL, 2 * D_MODEL), lambda i: (0, 0))],
        out_specs=pl.BlockSpec((N_MEM, 2 * D_MODEL), lambda i: (i, 0)),
        out_shape=jax.ShapeDtypeStruct((B * N_MEM, 2 * D_MODEL), jnp.float32),
        compiler_params=pltpu.CompilerParams(dimension_semantics=("arbitrary",), vmem_limit_bytes=VMEM_LIMIT),
        name="mem_kv",
    )(mem.reshape(B * N_MEM, D_MODEL), w)
    mk = out[:, :D_MODEL].reshape(B, N_MEM, MEM_HEADS, MEM_HD)
    mv = out[:, D_MODEL:].reshape(B, N_MEM, MEM_HEADS, MEM_HD)
    return mk, mv


PEER_RT = 256
PEER_TB = 512
PEER_EB = 1024
PEER_VMEM_LIMIT = 58 * 1024 * 1024


def _top16_rows(s, row_id, exact_ties):
    big = float(2 ** 20)
    out_id = lax.broadcasted_iota(jnp.int32, (PEER_TOPK, s.shape[1]), 0)
    stacked = jnp.zeros((PEER_TOPK, s.shape[1]), jnp.float32)
    rank = jnp.full(s.shape, float(PEER_TOPK), jnp.float32)
    rows = []
    for k in range(PEER_TOPK):
        m = jnp.max(s, axis=0, keepdims=True)
        hit = s == m
        if exact_ties:
            hit = row_id == jnp.min(jnp.where(hit, row_id, big), axis=0, keepdims=True)
        s = jnp.where(hit, NEG_INF, s)
        rank = jnp.where(hit, float(k), rank)
        rows.append(m)
        stacked = jnp.where(out_id == k, m, stacked)
    return rows, stacked, rank, s


def _peer_route_head(s1, s2, exact_ties):
    tb = s1.shape[1]
    row128 = lax.broadcasted_iota(jnp.int32, (PEER_NKEYS, tb), 0).astype(jnp.float32)
    sub8 = lax.broadcasted_iota(jnp.int32, (8, tb), 0)
    sub8f = sub8.astype(jnp.float32)
    r1, v1, rank1, left1 = _top16_rows(s1, row128, exact_ties)
    r2, v2, rank2, left2 = _top16_rows(s2, row128, exact_ties)
    groups, ids = [], []
    for b in range(8):
        lim = PEER_TOPK // (b + 1)
        for a0 in range(0, lim, 8):
            g = v1[a0:a0 + 8] + r2[b]
            if lim - a0 < 8:
                g = jnp.where(sub8 < lim - a0, g, NEG_INF)
            groups.append(g)
            ids.append((sub8f + float(a0)) * float(PEER_TOPK) + float(b))
    groups.append(r1[0] + v2[8:16])
    ids.append(sub8f + 8.0)
    cand = jnp.concatenate(groups, axis=0)
    vals, _, _, left = _top16_rows(cand, jnp.concatenate(ids, axis=0), exact_ties)
    z = jnp.ones_like(vals[0])
    for k in range(1, PEER_TOPK):
        z = z + jnp.exp(vals[k] - vals[0])
    taken = jnp.where((left == NEG_INF) & (cand > NEG_INF), 1.0, 0.0)
    cnt_lo = jnp.zeros((8, tb), jnp.float32)
    gi = 0
    for b in range(8):
        for a0 in range(0, PEER_TOPK // (b + 1), 8):
            if a0 == 0:
                cnt_lo = cnt_lo + taken[gi * 8:(gi + 1) * 8]
            else:
                cnt_hi = taken[gi * 8:(gi + 1) * 8]
            gi += 1
    tail = jnp.sum(taken[gi * 8:(gi + 1) * 8], axis=0, keepdims=True)
    cnt_lo = cnt_lo + jnp.where(sub8 == 0, tail, 0.0)
    lim_full = jnp.full((PEER_NKEYS, tb), -1.0, jnp.float32)
    for a in range(PEER_TOPK):
        cnt = cnt_lo if a < 8 else cnt_hi
        lim_full = jnp.where(rank1 == float(a), cnt[a % 8:a % 8 + 1] - 1.0, lim_full)
    n_taken = (jnp.sum(jnp.where(left1 == NEG_INF, 1.0, 0.0), axis=0, keepdims=True)
               + jnp.sum(jnp.where(left2 == NEG_INF, 1.0, 0.0), axis=0, keepdims=True)
               + jnp.sum(taken, axis=0, keepdims=True))
    return (rank2.astype(jnp.bfloat16), lim_full, jnp.exp(s1 - r1[0]) / z,
            jnp.exp(s2 - r2[0]).astype(jnp.bfloat16), n_taken)


def _peer_route_kernel(x_ref, wq_ref, k1_ref, k2_ref, rk2_ref, lim_ref, p1_ref, p2_ref, qt_scr):
    half = PEER_DKEY // 2
    xb = x_ref[...].astype(jnp.bfloat16)
    qt_scr[...] = lax.dot_general(wq_ref[...], xb, (((1,), (1,)), ((), ())), preferred_element_type=jnp.float32)

    def head(h, carry):
        r0 = pl.multiple_of(h * PEER_DKEY, PEER_DKEY)
        q1 = qt_scr[pl.ds(r0, half), :].astype(jnp.bfloat16)
        q2 = qt_scr[pl.ds(r0 + half, half), :].astype(jnp.bfloat16)
        s1 = jnp.dot(k1_ref[...], q1, preferred_element_type=jnp.float32)
        s2 = jnp.dot(k2_ref[...], q2, preferred_element_type=jnp.float32)

        def emit(exact_ties):
            rk2, lim, p1, p2, n_taken = _peer_route_head(s1, s2, exact_ties)
            rk2_ref[h] = rk2
            lim_ref[h] = lim
            p1_ref[h] = p1
            p2_ref[h] = p2
            return n_taken

        n_taken = emit(False)
        merged = jnp.max(jnp.abs(n_taken - 3.0 * PEER_TOPK)) > 0.0

        @pl.when(merged)
        def _():
            emit(True)

        return carry

    lax.fori_loop(0, PEER_HEADS, head, 0)


def _row_bf16(row):
    r16 = jnp.broadcast_to(row, (16, row.shape[1])).astype(jnp.bfloat16)
    return jnp.concatenate([r16] * (PEER_NKEYS // 16), axis=0)


def _peer_weights(rk2_ref, lim_ref, p1_ref, p2_ref, row0, r, tb):
    w = jnp.zeros((PEER_NKEYS, tb), jnp.bfloat16)
    for h in range(PEER_HEADS):
        p2h = p2_ref[h]
        lim8 = lim_ref[h, pl.ds(row0, 8), :]
        p18 = p1_ref[h, pl.ds(row0, 8), :]
        sel = rk2_ref[h] <= _row_bf16(lim8[r:r + 1, :])
        w = w + jnp.where(sel, p2h, jnp.zeros_like(p2h)) * _row_bf16(p18[r:r + 1, :])
    return w


def _peer_dense_kernel(x_ref, rk2_ref, lim_ref, p1_ref, p2_ref, u0_ref, ua_ref, ub_ref, vt_ref, g_ref, b_ref,
                       o_ref, xb_scr, yt_scr, ht0_scr, ht1_scr):
    k = pl.program_id(1)
    n_i1 = PEER_EB // PEER_NKEYS
    tb = x_ref.shape[0]
    nt = (((1,), (1,)), ((), ()))
    f32 = jnp.float32

    @pl.when(k == 0)
    def _():
        xb_scr[...] = x_ref[...].astype(jnp.bfloat16)
        yt_scr[...] = jnp.zeros_like(yt_scr)
        ht0_scr[...] = lax.dot_general(u0_ref[...], xb_scr[...], nt, preferred_element_type=f32)

    xb = xb_scr[...]
    n_sub = 2
    sub = PEER_EB // n_sub

    def second_half(ht_scr, blk, vt_off, acc):
        for j in range(n_sub):
            pieces = []
            for cc in range(sub // PEER_NKEYS):
                w = _peer_weights(rk2_ref, lim_ref, p1_ref, p2_ref, pl.multiple_of(blk * n_i1, 8),
                                  j * (sub // PEER_NKEYS) + cc, tb)
                r0 = j * sub + cc * PEER_NKEYS
                hc = ht_scr[r0:r0 + PEER_NKEYS, :]
                gelu = 0.5 * hc * (1.0 + lax.erf(hc * (2.0 ** -0.5)))
                pieces.append(w * gelu.astype(jnp.bfloat16))
            at = jnp.concatenate(pieces, axis=0)
            c0 = vt_off + j * sub
            acc = acc + jnp.dot(vt_ref[:, c0:c0 + sub], at, preferred_element_type=f32)
        return acc

    acc = yt_scr[...]
    ht1_scr[...] = lax.dot_general(ua_ref[...], xb, nt, preferred_element_type=f32)
    acc = second_half(ht0_scr, 2 * k, 0, acc)
    ht0_scr[...] = lax.dot_general(ub_ref[...], xb, nt, preferred_element_type=f32)
    acc = second_half(ht1_scr, 2 * k + 1, PEER_EB, acc)
    yt_scr[...] = acc

    @pl.when(k == pl.num_programs(1) - 1)
    def _():
        z = ALPHA * x_ref[...] + yt_scr[...].T
        o_ref[...] = _layer_norm_rows(z, g_ref[...], b_ref[...])


def peer_ln(x, w_pq, sub_k1, sub_k2, peer_u, peer_v, ln_g, ln_b):
    T = x.shape[0]
    assert T % PEER_TB == 0 and T % PEER_RT == 0
    nt = T // PEER_TB
    half = PEER_DKEY // 2
    wq_t = w_pq.reshape(D_MODEL, PEER_HEADS * PEER_DKEY).T.astype(jnp.bfloat16)
    sshape = jax.ShapeDtypeStruct((PEER_HEADS, PEER_NKEYS, T), jnp.float32)
    sspec = pl.BlockSpec((PEER_HEADS, PEER_NKEYS, PEER_RT), lambda j: (0, 0, j))
    rk2, lim, p1, p2 = pl.pallas_call(
        _peer_route_kernel,
        grid=(T // PEER_RT,),
        in_specs=[pl.BlockSpec((PEER_RT, D_MODEL), lambda j: (j, 0)),
                  pl.BlockSpec((PEER_HEADS * PEER_DKEY, D_MODEL), lambda j: (0, 0)),
                  pl.BlockSpec((PEER_NKEYS, half), lambda j: (0, 0)),
                  pl.BlockSpec((PEER_NKEYS, half), lambda j: (0, 0))],
        out_specs=[sspec, sspec, sspec, sspec],
        out_shape=[jax.ShapeDtypeStruct(sshape.shape, jnp.bfloat16), sshape, sshape,
                   jax.ShapeDtypeStruct(sshape.shape, jnp.bfloat16)],
        scratch_shapes=[pltpu.VMEM((PEER_HEADS * PEER_DKEY, PEER_RT), jnp.float32)],
        compiler_params=pltpu.CompilerParams(dimension_semantics=("arbitrary",), vmem_limit_bytes=VMEM_LIMIT),
        name="peer_route",
    )(x, wq_t, sub_k1.astype(jnp.bfloat16), sub_k2.astype(jnp.bfloat16))

    u_b = peer_u.astype(jnp.bfloat16)
    vt_b = peer_v.T.astype(jnp.bfloat16)
    ne = PEER_N // PEER_EB
    assert ne % 2 == 0
    sspec2 = pl.BlockSpec((PEER_HEADS, PEER_NKEYS, PEER_TB), lambda j, k: (0, 0, j))
    return pl.pallas_call(
        _peer_dense_kernel,
        grid=(nt, ne // 2),
        in_specs=[pl.BlockSpec((PEER_TB, D_MODEL), lambda j, k: (j, 0)),
                  sspec2, sspec2, sspec2, sspec2,
                  pl.BlockSpec((PEER_EB, D_MODEL), lambda j, k: (0, 0)),
                  pl.BlockSpec((PEER_EB, D_MODEL), lambda j, k: (2 * k + 1, 0)),
                  pl.BlockSpec((PEER_EB, D_MODEL), lambda j, k: (jnp.minimum(2 * k + 2, ne - 1), 0)),
                  pl.BlockSpec((D_MODEL, 2 * PEER_EB), lambda j, k: (0, k)),
                  pl.BlockSpec((1, D_MODEL), lambda j, k: (0, 0)),
                  pl.BlockSpec((1, D_MODEL), lambda j, k: (0, 0))],
        out_specs=pl.BlockSpec((PEER_TB, D_MODEL), lambda j, k: (j, 0)),
        out_shape=jax.ShapeDtypeStruct((T, D_MODEL), jnp.float32),
        scratch_shapes=[pltpu.VMEM((PEER_TB, D_MODEL), jnp.bfloat16),
                        pltpu.VMEM((D_MODEL, PEER_TB), jnp.float32),
                        pltpu.VMEM((PEER_EB, PEER_TB), jnp.float32),
                        pltpu.VMEM((PEER_EB, PEER_TB), jnp.float32)],
        compiler_params=pltpu.CompilerParams(dimension_semantics=("arbitrary", "arbitrary"),
                                             vmem_limit_bytes=PEER_VMEM_LIMIT),
        name="peer_dense",
    )(x, rk2, lim, p1, p2, u_b, u_b, u_b, vt_b, ln_g.reshape(1, -1), ln_b.reshape(1, -1))


def kernel(x_prompt, x_sample, cache_kv_latent, cache_k_rope, state_C, state_n, state_m,
           cache_mem_k, cache_mem_v, page_table, mem_prompt, ln0_g, ln0_b, w_in, b_i, b_f,
           g_q, w_uq, g_kv, w_uk, w_uv, w_out, ln1_g, ln1_b, w_mq, w_mk, w_mv, w_mo,
           ln2_g, ln2_b, w_pq, sub_k1, sub_k2, peer_u, peer_v, ln3_g, ln3_b):
    B, S = x_prompt.shape[:2]
    NB, TQ = x_sample.shape[:2]
    past = page_table.shape[1] * PAGE_SIZE
    n_p = B * S
    l = 0
    x_all = jnp.concatenate([x_prompt.reshape(n_p, D_MODEL), x_sample.reshape(NB * TQ, D_MODEL)], axis=0)
    pos = jnp.concatenate([jnp.tile(jnp.arange(S), B), jnp.tile(past + jnp.arange(TQ), NB)])
    (xn, q_lat, q_rope, kv, kvk, kr, mq, mk, mv, gates, o_gate) = mix_in(
        x_all, pos, ln0_g, ln0_b, w_in[l], b_i[l], b_f[l], g_q[l], w_uq[l], g_kv[l], w_uk[l])

    mla_p = mla_attend_prompt(q_lat, q_rope, kvk, w_uv[l], B, S)
    mlh_p, C_p, n_pst, m_p = mlstm_prompt(mq, mk, mv, gates, B, S)

    kv_s = kv[n_p:].reshape(NB, TQ, MLA_KV_LORA)
    kr_s = kr[n_p:, :MLA_ROPE].reshape(NB, TQ, MLA_ROPE)
    mla_s = mla_attend_sample(q_lat[n_p:].reshape(NB, TQ, -1), q_rope[n_p:].reshape(NB, TQ, -1), kv_s, kr_s,
                              cache_kv_latent.reshape(cache_kv_latent.shape[1:]),
                              jnp.swapaxes(cache_k_rope.reshape(cache_k_rope.shape[1:]), 1, 2), page_table, w_uv[l])
    mlh_s, C_s, n_s, m_s = mlstm_step(mq[n_p:], mk[n_p:], mv[n_p:], gates[n_p:], state_C.reshape(state_C.shape[1:]),
                                      state_n.reshape(state_n.shape[1:]), state_m.reshape(state_m.shape[1:]), NB, TQ)

    mla_all = jnp.concatenate([mla_p, mla_s], axis=0)
    mlh_all = jnp.concatenate([mlh_p, mlh_s], axis=0)
    x1, qm = mix_out(xn, mla_all, mlh_all, o_gate, w_out[l], ln1_g[l], ln1_b[l], w_mq[l])
    mk_p, mv_p = mem_kv(mem_prompt, w_mk[l], w_mv[l])
    x2_p = mem_attend_ln(x1, qm, mk_p.reshape(B, N_MEM, D_MODEL), mv_p.reshape(B, N_MEM, D_MODEL),
                         w_mo[l], ln2_g[l], ln2_b[l], B, S)
    x2_s = mem_attend_ln(x1[n_p:], qm[n_p:], cache_mem_k.reshape(cache_mem_k.shape[1:]),
                         cache_mem_v.reshape(cache_mem_v.shape[1:]), w_mo[l], ln2_g[l], ln2_b[l], NB, TQ)
    x2 = jnp.concatenate([x2_p, x2_s], axis=0)
    x3 = peer_ln(x2, w_pq[l], sub_k1[l], sub_k2[l], peer_u[l], peer_v[l], ln3_g[l], ln3_b[l])
    st = lambda t: t[None]
    return (x3[:n_p].reshape(B, S, D_MODEL), x3[n_p:].reshape(NB, TQ, D_MODEL),
            st(kv[:n_p].reshape(B, S, MLA_KV_LORA)), st(kr[:n_p, :MLA_ROPE].reshape(B, S, MLA_ROPE)),
            st(C_p), st(n_pst), st(m_p), st(mk_p), st(mv_p),
            st(kv_s), st(kr_s), st(C_s), st(n_s), st(m_s))
```

```python
import functools

import jax, jax.numpy as jnp
from jax import lax
import numpy as np
from jax.experimental import pallas as pl
from jax.experimental.pallas import tpu as pltpu

D_MODEL = 1024
PAGE_SIZE = 128

MLA_HEADS = 8
MLA_NOPE = 64
MLA_ROPE = 32
MLA_V = 64
MLA_KV_LORA = 256
MLA_Q_LORA = 384
MLA_SCALE = (MLA_NOPE + MLA_ROPE) ** -0.5
ROPE_BASE = 10000.0
ML_HEADS = 4
ML_DH = 128
ML_CHUNK = 64
MLA_WIDTH = MLA_HEADS * MLA_V
ML_WIDTH = ML_HEADS * ML_DH
D_MIX = MLA_WIDTH + ML_WIDTH
N_MEM = 256
MEM_HEADS = 4
MEM_HD = D_MODEL // MEM_HEADS
PEER_HEADS = 8
PEER_NKEYS = 128
PEER_N = PEER_NKEYS * PEER_NKEYS
PEER_DKEY = 128
PEER_TOPK = 16
LN_EPS = 1e-5
RMS_EPS = 1e-6
DEPTH = 1
ALPHA = (2 * DEPTH) ** 0.25
NEG_INF = float('-inf')

VMEM_LIMIT = 48 * 1024 * 1024

IN_PAD = 2944
OFF_CQ, OFF_CKV, OFF_KR, OFF_MQ, OFF_MK, OFF_MV, OFF_G, OFF_O = 0, 384, 640, 768, 1280, 1792, 2304, 2432
MIX_TM = 256


def _split_specs(tm, width, n_first):
    return (pl.BlockSpec((tm, width), lambda i, *_: (jnp.minimum(i, n_first - 1), 0)),
            pl.BlockSpec((tm, width), lambda i, *_: (jnp.maximum(i - n_first, 0), 0)))


def _pick(n_first, a_ref, b_ref):
    return jnp.where(pl.program_id(0) < n_first, a_ref[...], b_ref[...])


def _layer_norm_rows(z, g, b):
    mu = jnp.mean(z, axis=-1, keepdims=True)
    zc = z - mu
    var = jnp.mean(zc * zc, axis=-1, keepdims=True)
    return zc * lax.rsqrt(var + LN_EPS) * g + b


def _rope_lanes(x, cos, sin_signed):
    n = x.shape[1]
    lane = lax.broadcasted_iota(jnp.int32, x.shape, 1)
    partner = jnp.where((lane & 31) < 16, pltpu.roll(x, n - 16, axis=1), pltpu.roll(x, 16, axis=1))
    return x * cos + partner * sin_signed


def _mix_in_kernel(n_first, xa_ref, xb_ref, g0_ref, b0_ref, win_ref, gq_ref, wuq_ref, wuk_ref, gkv_ref, cos_ref, sin_ref, gb_ref,
                   xn_ref, ql_ref, qr_ref, kv_ref, kvk_ref, kr_ref, mq_ref, mk_ref, mv_ref, gate_ref, og_ref):
    xn = _layer_norm_rows(_pick(n_first, xa_ref, xb_ref), g0_ref[...], b0_ref[...])
    xn_ref[...] = xn
    z = jnp.dot(xn.astype(jnp.bfloat16), win_ref[...], preferred_element_type=jnp.float32)
    cos = cos_ref[...]
    sin = sin_ref[...]
    cq = z[:, OFF_CQ:OFF_CQ + MLA_Q_LORA]
    cq = cq * lax.rsqrt(jnp.mean(cq * cq, axis=-1, keepdims=True) + RMS_EPS) * gq_ref[...]
    q = jnp.dot(cq.astype(jnp.bfloat16), wuq_ref[...], preferred_element_type=jnp.float32)
    n_nope = MLA_HEADS * MLA_NOPE
    qrope = _rope_lanes(q[:, n_nope:], jnp.concatenate([cos, cos], axis=1), jnp.concatenate([sin, sin], axis=1))
    qr_ref[...] = qrope.astype(jnp.bfloat16)
    ql_ref[...] = jnp.dot(q[:, :n_nope].astype(jnp.bfloat16), wuk_ref[...],
                          preferred_element_type=jnp.float32).astype(jnp.bfloat16)
    ckv = z[:, OFF_CKV:OFF_CKV + MLA_KV_LORA]
    kv = ckv * lax.rsqrt(jnp.mean(ckv * ckv, axis=-1, keepdims=True) + RMS_EPS) * gkv_ref[...]
    kv_ref[...] = kv
    kr = _rope_lanes(z[:, OFF_KR:OFF_KR + 128], cos, sin)
    kr_ref[...] = kr
    krt = kr + pltpu.roll(kr, 32, axis=1) + pltpu.roll(kr, 64, axis=1) + pltpu.roll(kr, 96, axis=1)
    kvk_ref[...] = jnp.concatenate([kv, krt, krt], axis=1).astype(jnp.bfloat16)
    mq_ref[...] = z[:, OFF_MQ:OFF_MQ + ML_WIDTH]
    mk_ref[...] = z[:, OFF_MK:OFF_MK + ML_WIDTH] * (ML_DH ** -0.5)
    mv_ref[...] = z[:, OFF_MV:OFF_MV + ML_WIDTH]
    g = z[:, OFF_G:OFF_G + 128] + gb_ref[...]
    lane = lax.broadcasted_iota(jnp.int32, g.shape, 1)
    gate_ref[...] = jnp.where(lane < ML_HEADS, g, jax.nn.log_sigmoid(g))
    og_ref[...] = jax.nn.sigmoid(z[:, OFF_O:OFF_O + ML_WIDTH])


def mix_in(x_a, x_b, pos, ln0_g, ln0_b, w_in, b_i, b_f, g_q, w_uq, g_kv, w_uk):
    T = x_a.shape[0] + x_b.shape[0]
    n_first = x_a.shape[0] // MIX_TM
    assert x_a.shape[0] % MIX_TM == 0 and x_b.shape[0] % MIX_TM == 0
    f32, bf = jnp.float32, jnp.bfloat16
    zc = lambda n: jnp.zeros((D_MODEL, n), f32)
    win_p = jnp.concatenate([w_in[:, :672], zc(96), w_in[:, 672:2208], w_in[:, 2208:2216], zc(120), w_in[:, 2216:]],
                            axis=1).astype(bf)
    assert win_p.shape[1] == IN_PAD
    wuq_p = jnp.concatenate([w_uq[:, :, :MLA_NOPE].reshape(MLA_Q_LORA, -1),
                             w_uq[:, :, MLA_NOPE:].reshape(MLA_Q_LORA, -1)], axis=1).astype(bf)
    hh = jnp.arange(MLA_HEADS)
    wuk_blk = jnp.zeros((MLA_HEADS, MLA_NOPE, MLA_HEADS, MLA_KV_LORA), f32)
    wuk_blk = wuk_blk.at[hh, :, hh, :].set(jnp.transpose(w_uk, (1, 2, 0)))
    wuk_blk = wuk_blk.reshape(MLA_HEADS * MLA_NOPE, MLA_HEADS * MLA_KV_LORA).astype(bf)
    inv = 1.0 / (ROPE_BASE ** (jnp.arange(0, MLA_ROPE, 2, dtype=f32) / MLA_ROPE))
    ang = pos.astype(f32)[:, None] * inv[None, :]
    c, s = jnp.cos(ang), jnp.sin(ang)
    cos128 = jnp.tile(jnp.concatenate([c, c], axis=1), (1, 4))
    sin128 = jnp.tile(jnp.concatenate([-s, s], axis=1), (1, 4))
    gbias = jnp.concatenate([b_i, b_f, jnp.zeros((120,), f32)]).reshape(1, 128)
    row = lambda n: pl.BlockSpec((MIX_TM, n), lambda i: (i, 0))
    full = lambda a: pl.BlockSpec(a.shape, lambda i: (0,) * a.ndim)
    ins = [x_a, x_b, ln0_g.reshape(1, -1), ln0_b.reshape(1, -1), win_p, g_q.reshape(1, -1), wuq_p, wuk_blk,
           g_kv.reshape(1, -1), cos128, sin128, gbias]
    in_specs = list(_split_specs(MIX_TM, D_MODEL, n_first)) + [full(a) for a in ins[2:9]] + [row(128), row(128), full(gbias)]
    outs = [(D_MODEL, f32), (MLA_HEADS * MLA_KV_LORA, bf), (MLA_HEADS * MLA_ROPE, bf), (MLA_KV_LORA, f32),
            (MLA_KV_LORA + MLA_HEADS * MLA_ROPE, bf), (128, f32), (ML_WIDTH, f32), (ML_WIDTH, f32),
            (ML_WIDTH, f32), (128, f32), (ML_WIDTH, f32)]
    return pl.pallas_call(
        functools.partial(_mix_in_kernel, n_first),
        grid=(T // MIX_TM,),
        in_specs=in_specs,
        out_specs=[row(n) for n, _ in outs],
        out_shape=[jax.ShapeDtypeStruct((T, n), dt) for n, dt in outs],
        compiler_params=pltpu.CompilerParams(dimension_semantics=("arbitrary",), vmem_limit_bytes=VMEM_LIMIT),
        name="mix_in",
    )(*ins)


ATT_BQ = 128
ATT_BK = 512


def _mla_prompt_kernel(ql_ref, qr_ref, kvk_ref, wuv_ref, o_ref, acc_scr, m_scr, l_scr):
    qi = pl.program_id(1)
    ql = jnp.concatenate([ql_ref[:, h * MLA_KV_LORA:(h + 1) * MLA_KV_LORA] for h in range(MLA_HEADS)], axis=0)
    qr_all = qr_ref[...]
    lane_head = lax.broadcasted_iota(jnp.int32, qr_all.shape, 1) // MLA_ROPE
    qr = jnp.concatenate([jnp.where(lane_head == h, qr_all, jnp.zeros_like(qr_all)) for h in range(MLA_HEADS)],
                         axis=0)
    q = jnp.concatenate([ql, qr], axis=1)
    acc_scr[...] = jnp.zeros_like(acc_scr)
    m_scr[...] = jnp.full_like(m_scr, NEG_INF)
    l_scr[...] = jnp.zeros_like(l_scr)
    nt = (((1,), (1,)), ((), ()))
    n_last = (qi * ATT_BQ) // ATT_BK
    rows = q.shape[0]
    tok = qi * ATT_BQ + (lax.broadcasted_iota(jnp.int32, (rows, ATT_BK), 0) & (ATT_BQ - 1))
    col = lax.broadcasted_iota(jnp.int32, (rows, ATT_BK), 1)

    def scores(kj):
        k0 = pl.multiple_of(kj * ATT_BK, ATT_BK)
        s = lax.dot_general(q, kvk_ref[pl.ds(k0, ATT_BK), :], nt, preferred_element_type=jnp.float32) * MLA_SCALE
        return jnp.where(col + k0 <= tok, s, NEG_INF)

    def accumulate(kj, s):
        k0 = pl.multiple_of(kj * ATT_BK, ATT_BK)
        kvb = kvk_ref[pl.ds(k0, ATT_BK), :MLA_KV_LORA]
        m_old = m_scr[...]
        m_new = jnp.maximum(m_old, jnp.max(s, axis=1, keepdims=True))
        alpha = jnp.exp(m_old - m_new)
        p = jnp.exp(s - m_new)
        l_scr[...] = alpha * l_scr[...] + jnp.sum(p, axis=1, keepdims=True)
        acc_scr[...] = alpha * acc_scr[...] + jnp.dot(p.astype(jnp.bfloat16), kvb,
                                                      preferred_element_type=jnp.float32)
        m_scr[...] = m_new

    def body(kj, s_cur):
        s_next = scores(kj + 1)
        accumulate(kj, s_cur)
        return s_next

    s_last = lax.fori_loop(0, n_last, body, scores(0))
    accumulate(n_last, s_last)
    o = (acc_scr[...] / l_scr[...]).astype(jnp.bfloat16)
    r = jnp.dot(o, wuv_ref[...], preferred_element_type=jnp.float32)
    col_head = lax.broadcasted_iota(jnp.int32, (ATT_BQ, MLA_WIDTH), 1) // MLA_V
    out = jnp.zeros((ATT_BQ, MLA_WIDTH), jnp.float32)
    for h in range(MLA_HEADS):
        out = out + jnp.where(col_head == h, r[h * ATT_BQ:(h + 1) * ATT_BQ], 0.0)
    o_ref[...] = out


def mla_attend_prompt(q_lat, q_rope, kvk, w_uv, n_seq, seq):
    assert seq % ATT_BK == 0 and ATT_BK % ATT_BQ == 0
    H, C = MLA_HEADS, MLA_KV_LORA
    nq = seq // ATT_BQ
    rows = H * ATT_BQ
    return pl.pallas_call(
        _mla_prompt_kernel,
        grid=(n_seq, nq),
        in_specs=[pl.BlockSpec((ATT_BQ, H * C), lambda b, i: (b * nq + i, 0)),
                  pl.BlockSpec((ATT_BQ, H * MLA_ROPE), lambda b, i: (b * nq + i, 0)),
                  pl.BlockSpec((seq, C + H * MLA_ROPE), lambda b, i: (b, 0)),
                  pl.BlockSpec((C, H * MLA_V), lambda b, i: (0, 0))],
        out_specs=pl.BlockSpec((ATT_BQ, H * MLA_V), lambda b, i: (b * nq + i, 0)),
        out_shape=jax.ShapeDtypeStruct((n_seq * seq, H * MLA_V), jnp.float32),
        scratch_shapes=[pltpu.VMEM((rows, C), jnp.float32),
                        pltpu.VMEM((rows, 1), jnp.float32),
                        pltpu.VMEM((rows, 1), jnp.float32)],
        compiler_params=pltpu.CompilerParams(dimension_semantics=("arbitrary", "arbitrary"),
                                             vmem_limit_bytes=VMEM_LIMIT),
        name="mla_prompt",
    )(q_lat, q_rope, kvk, w_uv.reshape(C, H * MLA_V).astype(jnp.bfloat16))


SMP_KC = 2048


def _mla_sample_kernel(pt_ref, ql_ref, qr_ref, kvn_ref, krn_ref, wuv_ref, lat_hbm, rope_hbm, o_ref,
                       lat_buf, rope_buf, lat_bf, sem_lat, sem_rope):
    b = pl.program_id(0)
    nb = pl.num_programs(0)
    n_pages = pt_ref.shape[1]
    rows = ql_ref.shape[0]
    tq = kvn_ref.shape[0]
    n_keys = n_pages * PAGE_SIZE

    def page_copies(seq, slot, p):
        page = pt_ref[seq, p]
        dst = pl.ds(p * PAGE_SIZE, PAGE_SIZE)
        return (pltpu.make_async_copy(lat_hbm.at[page], lat_buf.at[slot, dst], sem_lat.at[slot]),
                pltpu.make_async_copy(rope_hbm.at[page], rope_buf.at[slot, :, dst], sem_rope.at[slot]))

    def start_fetch(seq, slot):
        for p in range(n_pages):
            for cp in page_copies(seq, slot, p):
                cp.start()

    def wait_fetch(seq, slot):
        for p in range(n_pages):
            for cp in page_copies(seq, slot, p):
                cp.wait()

    slot = b % 2

    @pl.when(b == 0)
    def _():
        start_fetch(0, 0)

    @pl.when(b + 1 < nb)
    def _():
        start_fetch(b + 1, 1 - slot)

    wait_fetch(b, slot)

    bf, f32 = jnp.bfloat16, jnp.float32
    ql = ql_ref[...]
    qr = qr_ref[...]
    nt = (((1,), (1,)), ((), ()))
    parts = []
    for c in range(n_keys // SMP_KC):
        r = pl.ds(c * SMP_KC, SMP_KC)
        lb = lat_buf[slot, r, :].astype(bf)
        rb = rope_buf[slot, :, r].astype(bf)
        lat_bf[r, :] = lb
        parts.append(lax.dot_general(ql, lb, nt, preferred_element_type=f32)
                     + jnp.dot(qr, rb, preferred_element_type=f32))
    s_past = jnp.concatenate(parts, axis=1) * MLA_SCALE
    qlf, qrf = ql.astype(f32), qr.astype(f32)
    kvn = kvn_ref[...].astype(f32)
    krn = krn_ref[...].astype(f32)
    tok = lax.broadcasted_iota(jnp.int32, (rows, 1), 0) % tq
    s_new = []
    for j in range(tq):
        sj = (jnp.sum(qlf * kvn[j:j + 1, :], axis=1, keepdims=True)
              + jnp.sum(qrf * krn[j:j + 1, :], axis=1, keepdims=True)) * MLA_SCALE
        s_new.append(jnp.where(tok >= j, sj, NEG_INF))
    m = jnp.max(s_past, axis=1, keepdims=True)
    for sj in s_new:
        m = jnp.maximum(m, sj)
    p_past = jnp.exp(s_past - m)
    p_new = [jnp.exp(sj - m) for sj in s_new]
    l = jnp.sum(p_past, axis=1, keepdims=True)
    for pj in p_new:
        l = l + pj
    inv = 1.0 / l
    o = jnp.dot((p_past * inv).astype(bf), lat_bf[...], preferred_element_type=f32)
    for j in range(tq):
        o = o + (p_new[j] * inv).astype(bf).astype(f32) * kvn[j:j + 1, :]
    r = jnp.dot(o.astype(bf), wuv_ref[...], preferred_element_type=f32)
    col_head = lax.broadcasted_iota(jnp.int32, (tq, MLA_WIDTH), 1) // MLA_V
    out = jnp.zeros((tq, MLA_WIDTH), f32)
    for h in range(MLA_HEADS):
        out = out + jnp.where(col_head == h, r[h * tq:(h + 1) * tq, :], 0.0)
    o_ref[...] = out


def mla_attend_sample(q_lat, q_rope, kv_new, kr_new, pool_lat, pool_rope_t, page_table, w_uv):
    NB, T = q_lat.shape[:2]
    H, C, R = MLA_HEADS, MLA_KV_LORA, MLA_ROPE
    bf = jnp.bfloat16
    n_pages = page_table.shape[1]
    n_keys = n_pages * PAGE_SIZE
    assert n_keys % SMP_KC == 0
    ql = q_lat.reshape(NB, T, H, C).transpose(0, 2, 1, 3).reshape(NB, H * T, C)
    qr = q_rope.reshape(NB, T, H, R).transpose(0, 2, 1, 3).reshape(NB, H * T, R)
    seq = lambda n, w: pl.BlockSpec((None, n, w), lambda b, pt: (b, 0, 0))
    grid_spec = pltpu.PrefetchScalarGridSpec(
        num_scalar_prefetch=1,
        grid=(NB,),
        in_specs=[seq(H * T, C), seq(H * T, R), seq(T, C), seq(T, R),
                  pl.BlockSpec((C, H * MLA_V), lambda b, pt: (0, 0)),
                  pl.BlockSpec(memory_space=pl.ANY), pl.BlockSpec(memory_space=pl.ANY)],
        out_specs=seq(T, H * MLA_V),
        scratch_shapes=[pltpu.VMEM((2, n_keys, C), jnp.float32), pltpu.VMEM((2, R, n_keys), jnp.float32),
                        pltpu.VMEM((n_keys, C), bf),
                        pltpu.SemaphoreType.DMA((2,)), pltpu.SemaphoreType.DMA((2,))],
    )
    out = pl.pallas_call(
        _mla_sample_kernel,
        grid_spec=grid_spec,
        out_shape=jax.ShapeDtypeStruct((NB, T, H * MLA_V), jnp.float32),
        compiler_params=pltpu.CompilerParams(dimension_semantics=("arbitrary",), vmem_limit_bytes=VMEM_LIMIT),
        name="mla_sample",
    )(page_table, ql, qr, kv_new.astype(bf), kr_new.astype(bf), w_uv.reshape(C, H * MLA_V).astype(bf), pool_lat, pool_rope_t)
    return out.reshape(NB * T, H * MLA_V)


def _split3(x):
    hi = x.astype(jnp.bfloat16)
    r1 = x - hi.astype(jnp.float32)
    mid = r1.astype(jnp.bfloat16)
    lo = (r1 - mid.astype(jnp.float32)).astype(jnp.bfloat16)
    return hi, mid, lo


def _mlstm_chunk(q_all, k_all, v_all, gates, c_refs, n_refs, m_refs):
    L = q_all.shape[0]
    bf, f32 = jnp.bfloat16, jnp.float32
    row_t = lax.broadcasted_iota(jnp.int32, (L, L), 0)
    col_s = lax.broadcasted_iota(jnp.int32, (L, L), 1)
    causal = col_s <= row_t
    tril = jnp.where(causal, 1.0, 0.0).astype(bf)
    cum = sum(jnp.dot(tril, part, preferred_element_type=f32) for part in _split3(gates))
    gates_t = gates.T
    cum_t = cum.T
    nt = (((1,), (1,)), ((), ()))
    outs = []
    for h in range(ML_HEADS):
        c = slice(h * ML_DH, (h + 1) * ML_DH)
        q, k, v = q_all[:, c], k_all[:, c], v_all[:, c]
        C, n, m = c_refs[0](h), n_refs[0](h), m_refs[0](h)
        ig_col = gates[:, h:h + 1]
        b_col = cum[:, ML_HEADS + h:ML_HEADS + h + 1]
        ig_row = gates_t[h:h + 1, :]
        b_row = cum_t[ML_HEADS + h:ML_HEADS + h + 1, :]
        D = jnp.where(causal, b_col - b_row + ig_row, NEG_INF)
        inter = b_col + m
        m_t = jnp.maximum(inter, jnp.max(D, axis=1, keepdims=True))
        qb, kb, vb = q.astype(bf), k.astype(bf), v.astype(bf)
        A = jnp.exp(D - m_t) * lax.dot_general(qb, kb, nt, preferred_element_type=f32)
        w_inter = jnp.exp(inter - m_t)
        num = w_inter * jnp.dot(qb, C.astype(bf), preferred_element_type=f32) \
            + jnp.dot(A.astype(bf), vb, preferred_element_type=f32)
        qn = jnp.sum(qb.astype(f32) * n.astype(bf).astype(f32), axis=1, keepdims=True)
        den = w_inter * qn + jnp.sum(A, axis=1, keepdims=True)
        outs.append(num / jnp.maximum(jnp.abs(den), jnp.exp(-m_t)))
        b_end = b_col[L - 1:L, :]
        m_new = jnp.maximum(b_end + m, jnp.max(b_end - b_row + ig_row, axis=1, keepdims=True))
        a_prev = jnp.exp(b_end + m - m_new)
        kw = k * jnp.exp(b_end - b_col + ig_col - m_new)
        c_refs[1](h, a_prev * C + jnp.dot(kw.T.astype(bf), vb, preferred_element_type=f32))
        n_refs[1](h, a_prev * n + jnp.sum(kw, axis=0, keepdims=True))
        m_refs[1](h, m_new)
    return jnp.concatenate(outs, axis=1)


def _state_access(c_get, n_get, m_get, c_set, n_set, m_set):
    c_refs = (lambda h: c_get[h], lambda h, val: c_set.__setitem__(h, val))
    n_refs = (lambda h: n_get[h:h + 1, :], lambda h, val: n_set.__setitem__((slice(h, h + 1), slice(None)), val))
    m_refs = (lambda h: m_get[h:h + 1, 0:1],
              lambda h, val: m_set.__setitem__((slice(h, h + 1), slice(None)), jnp.broadcast_to(val, (1, 128))))
    return c_refs, n_refs, m_refs


def _mlstm_prompt_kernel(q_ref, k_ref, v_ref, g_ref, h_ref, c_out, n_out, m_out, c_scr, n_scr, m_scr):
    j = pl.program_id(1)

    @pl.when(j == 0)
    def _():
        c_scr[...] = jnp.zeros_like(c_scr)
        n_scr[...] = jnp.zeros_like(n_scr)
        m_scr[...] = jnp.zeros_like(m_scr)

    h_ref[...] = _mlstm_chunk(q_ref[...], k_ref[...], v_ref[...], g_ref[...],
                              *_state_access(c_scr, n_scr, m_scr, c_scr, n_scr, m_scr))

    @pl.when(j == pl.num_programs(1) - 1)
    def _():
        c_out[...] = c_scr[...]
        n_out[...] = n_scr[0:ML_HEADS, :]
        m_out[...] = m_scr[0:ML_HEADS, :]


def mlstm_prompt(mq, mk, mv, gates, n_seq, seq):
    nc = seq // ML_CHUNK
    f32 = jnp.float32
    tok = lambda w: pl.BlockSpec((ML_CHUNK, w), lambda b, j: (b * nc + j, 0))
    h, C, n, m = pl.pallas_call(
        _mlstm_prompt_kernel,
        grid=(n_seq, nc),
        in_specs=[tok(ML_WIDTH), tok(ML_WIDTH), tok(ML_WIDTH), tok(128)],
        out_specs=[tok(ML_WIDTH),
                   pl.BlockSpec((None, ML_HEADS, ML_DH, ML_DH), lambda b, j: (b, 0, 0, 0)),
                   pl.BlockSpec((None, ML_HEADS, ML_DH), lambda b, j: (b, 0, 0)),
                   pl.BlockSpec((None, ML_HEADS, 128), lambda b, j: (b, 0, 0))],
        out_shape=[jax.ShapeDtypeStruct((n_seq * seq, ML_WIDTH), f32),
                   jax.ShapeDtypeStruct((n_seq, ML_HEADS, ML_DH, ML_DH), f32),
                   jax.ShapeDtypeStruct((n_seq, ML_HEADS, ML_DH), f32),
                   jax.ShapeDtypeStruct((n_seq, ML_HEADS, 128), f32)],
        scratch_shapes=[pltpu.VMEM((ML_HEADS, ML_DH, ML_DH), f32), pltpu.VMEM((8, ML_DH), f32),
                        pltpu.VMEM((8, 128), f32)],
        compiler_params=pltpu.CompilerParams(dimension_semantics=("arbitrary", "arbitrary"),
                                             vmem_limit_bytes=VMEM_LIMIT),
        name="mlstm_prompt",
    )(mq, mk, mv, gates)
    return h, C, n, m[:, :, 0]


def _mlstm_step_kernel(q_ref, k_ref, v_ref, g_ref, c_in, n_in, m_in, h_ref, c_out, n_out, m_out):
    h_ref[...] = _mlstm_chunk(q_ref[...], k_ref[...], v_ref[...], g_ref[...],
                              *_state_access(c_in, n_in, m_in, c_out, n_out, m_out))


def mlstm_step(mq, mk, mv, gates, state_C, state_n, state_m, n_seq, rows):
    f32 = jnp.float32
    tok = lambda w: pl.BlockSpec((None, rows, w), lambda b: (b, 0, 0))
    st_c = pl.BlockSpec((None, ML_HEADS, ML_DH, ML_DH), lambda b: (b, 0, 0, 0))
    st_n = pl.BlockSpec((None, ML_HEADS, ML_DH), lambda b: (b, 0, 0))
    st_m = pl.BlockSpec((None, ML_HEADS, 128), lambda b: (b, 0, 0))
    r3 = lambda t: t.reshape(n_seq, rows, t.shape[-1])
    m_in = jnp.broadcast_to(state_m[:, :, None], (n_seq, ML_HEADS, 128))
    h, C, n, m = pl.pallas_call(
        _mlstm_step_kernel,
        grid=(n_seq,),
        in_specs=[tok(ML_WIDTH), tok(ML_WIDTH), tok(ML_WIDTH), tok(128), st_c, st_n, st_m],
        out_specs=[tok(ML_WIDTH), st_c, st_n, st_m],
        out_shape=[jax.ShapeDtypeStruct((n_seq, rows, ML_WIDTH), f32),
                   jax.ShapeDtypeStruct((n_seq, ML_HEADS, ML_DH, ML_DH), f32),
                   jax.ShapeDtypeStruct((n_seq, ML_HEADS, ML_DH), f32),
                   jax.ShapeDtypeStruct((n_seq, ML_HEADS, 128), f32)],
        compiler_params=pltpu.CompilerParams(dimension_semantics=("arbitrary",), vmem_limit_bytes=VMEM_LIMIT),
        name="mlstm_step",
    )(r3(mq), r3(mk), r3(mv), r3(gates), state_C, state_n, m_in)
    return h.reshape(n_seq * rows, ML_WIDTH), C, n, m[:, :, 0]


POST_TM = 256


def _mix_out_kernel(n_first, xn_ref, mla_a, mla_b, mlh_a, mlh_b, og_ref, wout_ref, g_ref, b_ref, wmq_ref, x1_ref, qm_ref):
    mixed = jnp.concatenate([_pick(n_first, mla_a, mla_b), og_ref[...] * _pick(n_first, mlh_a, mlh_b)],
                            axis=1).astype(jnp.bfloat16)
    mix = jnp.dot(mixed, wout_ref[...], preferred_element_type=jnp.float32)
    x1 = _layer_norm_rows(ALPHA * xn_ref[...] + mix, g_ref[...], b_ref[...])
    x1_ref[...] = x1
    qm_ref[...] = jnp.dot(x1.astype(jnp.bfloat16), wmq_ref[...],
                          preferred_element_type=jnp.float32).astype(jnp.bfloat16)


def mix_out(xn, mla_a, mla_b, mlh_a, mlh_b, o_gate, w_out, ln_g, ln_b, w_mq):
    T = xn.shape[0]
    n_first = mla_a.shape[0] // POST_TM
    assert mla_a.shape[0] % POST_TM == 0
    bf = jnp.bfloat16
    row = lambda n: pl.BlockSpec((POST_TM, n), lambda i: (i, 0))
    full = lambda shape: pl.BlockSpec(shape, lambda i: (0,) * len(shape))
    return pl.pallas_call(
        functools.partial(_mix_out_kernel, n_first),
        grid=(T // POST_TM,),
        in_specs=[row(D_MODEL), *_split_specs(POST_TM, MLA_WIDTH, n_first), *_split_specs(POST_TM, ML_WIDTH, n_first),
                  row(ML_WIDTH), full((D_MIX, D_MODEL)),
                  full((1, D_MODEL)), full((1, D_MODEL)), full((D_MODEL, D_MODEL))],
        out_specs=[row(D_MODEL), row(D_MODEL)],
        out_shape=[jax.ShapeDtypeStruct((T, D_MODEL), jnp.float32), jax.ShapeDtypeStruct((T, D_MODEL), bf)],
        compiler_params=pltpu.CompilerParams(dimension_semantics=("arbitrary",), vmem_limit_bytes=VMEM_LIMIT),
        name="mix_out",
    )(xn, mla_a, mla_b, mlh_a, mlh_b, o_gate, w_out.astype(bf), ln_g.reshape(1, -1), ln_b.reshape(1, -1),
      w_mq.reshape(D_MODEL, D_MODEL).astype(bf))


def _mem_attn_kernel(x1_ref, qm_ref, mk_ref, mv_ref, wmo_ref, g_ref, b_ref, x2_ref):
    q = qm_ref[...]
    nt = (((1,), (1,)), ((), ()))
    outs = []
    for h in range(MEM_HEADS):
        c = slice(h * MEM_HD, (h + 1) * MEM_HD)
        mk, mv = mk_ref[:, c].astype(jnp.bfloat16), mv_ref[:, c].astype(jnp.bfloat16)
        s = lax.dot_general(q[:, c], mk, nt, preferred_element_type=jnp.float32) * (MEM_HD ** -0.5)
        m = jnp.max(s, axis=-1, keepdims=True)
        p = jnp.exp(s - m)
        p = p / jnp.sum(p, axis=-1, keepdims=True)
        outs.append(jnp.dot(p.astype(jnp.bfloat16), mv, preferred_element_type=jnp.float32))
    o = jnp.concatenate(outs, axis=1).astype(jnp.bfloat16)
    att = jnp.dot(o, wmo_ref[...], preferred_element_type=jnp.float32)
    x2_ref[...] = _layer_norm_rows(ALPHA * x1_ref[...] + att, g_ref[...], b_ref[...])


def _mem_attn_sample_kernel(x1_ref, qm_ref, mk_ref, mv_ref, wmo_ref, g_ref, b_ref, x2_ref):
    tq = x1_ref.shape[0]
    bf, f32 = jnp.bfloat16, jnp.float32
    q = qm_ref[...]
    qs = jnp.concatenate([q[:, h * MEM_HD:(h + 1) * MEM_HD] for h in range(MEM_HEADS)], axis=0)
    kall = mk_ref[...].reshape(N_MEM * MEM_HEADS, MEM_HD).astype(bf)
    vall = mv_ref[...].reshape(N_MEM * MEM_HEADS, MEM_HD).astype(bf)
    s = lax.dot_general(qs, kall, (((1,), (1,)), ((), ())), preferred_element_type=f32) * (MEM_HD ** -0.5)
    row_h = lax.broadcasted_iota(jnp.int32, s.shape, 0) // tq
    col_h = lax.broadcasted_iota(jnp.int32, s.shape, 1) % MEM_HEADS
    s = jnp.where(row_h == col_h, s, NEG_INF)
    p = jnp.exp(s - jnp.max(s, axis=-1, keepdims=True))
    p = p / jnp.sum(p, axis=-1, keepdims=True)
    o = jnp.dot(p.astype(bf), vall, preferred_element_type=f32)
    o = jnp.concatenate([o[h * tq:(h + 1) * tq] for h in range(MEM_HEADS)], axis=1).astype(bf)
    att = jnp.dot(o, wmo_ref[...], preferred_element_type=f32)
    x2_ref[...] = _layer_norm_rows(ALPHA * x1_ref[...] + att, g_ref[...], b_ref[...])


def mem_attend_ln(x1, qm, mem_k, mem_v, w_mo, ln_g, ln_b, n_seq, rows_per_seq):
    bf = jnp.bfloat16
    wmo = w_mo.reshape(D_MODEL, D_MODEL).astype(bf)
    g, b = ln_g.reshape(1, -1), ln_b.reshape(1, -1)
    cp = pltpu.CompilerParams(dimension_semantics=("arbitrary",) * 2, vmem_limit_bytes=VMEM_LIMIT)
    mem = pl.BlockSpec((None, N_MEM, D_MODEL), lambda s, i: (s, 0, 0))
    full = lambda shape: pl.BlockSpec(shape, lambda s, i: (0,) * len(shape))
    if rows_per_seq % POST_TM == 0:
        nb = rows_per_seq // POST_TM
        tok = pl.BlockSpec((POST_TM, D_MODEL), lambda s, i: (s * nb + i, 0))
        return pl.pallas_call(
            _mem_attn_kernel, grid=(n_seq, nb),
            in_specs=[tok, tok, mem, mem, full((D_MODEL, D_MODEL)), full((1, D_MODEL)), full((1, D_MODEL))],
            out_specs=tok, out_shape=jax.ShapeDtypeStruct((n_seq * rows_per_seq, D_MODEL), jnp.float32),
            compiler_params=cp, name="mem_attn_prompt",
        )(x1, qm, mem_k, mem_v, wmo, g, b)
    mem = pl.BlockSpec((None, N_MEM, MEM_HEADS, MEM_HD), lambda s, i: (s, 0, 0, 0))
    x3 = x1.reshape(n_seq, rows_per_seq, D_MODEL)
    q3 = qm.reshape(n_seq, rows_per_seq, D_MODEL)
    tok = pl.BlockSpec((None, rows_per_seq, D_MODEL), lambda s, i: (s, 0, 0))
    out = pl.pallas_call(
        _mem_attn_sample_kernel, grid=(n_seq, 1),
        in_specs=[tok, tok, mem, mem, full((D_MODEL, D_MODEL)), full((1, D_MODEL)), full((1, D_MODEL))],
        out_specs=tok, out_shape=jax.ShapeDtypeStruct(x3.shape, jnp.float32),
        compiler_params=cp, name="mem_attn_sample",
    )(x3, q3, mem_k, mem_v, wmo, g, b)
    return out.reshape(x1.shape)


def _mem_kv_kernel(m_ref, w_ref, o_ref):
    o_ref[...] = jnp.dot(m_ref[...].astype(jnp.bfloat16), w_ref[...], preferred_element_type=jnp.float32)


def mem_kv(mem, w_mk, w_mv):
    B = mem.shape[0]
    w = jnp.concatenate([w_mk.reshape(D_MODEL, D_MODEL), w_mv.reshape(D_MODEL, D_MODEL)], axis=1).astype(jnp.bfloat16)
    out = pl.pallas_call(
        _mem_kv_kernel, grid=(B,),
        in_specs=[pl.BlockSpec((N_MEM, D_MODEL), lambda i: (i, 0)), pl.BlockSpec((D_MODEL, 2 * D_MODEL), lambda i: (0, 0))],
        out_specs=pl.BlockSpec((N_MEM, 2 * D_MODEL), lambda i: (i, 0)),
        out_shape=jax.ShapeDtypeStruct((B * N_MEM, 2 * D_MODEL), jnp.float32),
        compiler_params=pltpu.CompilerParams(dimension_semantics=("arbitrary",), vmem_limit_bytes=VMEM_LIMIT),
        name="mem_kv",
    )(mem.reshape(B * N_MEM, D_MODEL), w)
    mk = out[:, :D_MODEL].reshape(B, N_MEM, MEM_HEADS, MEM_HD)
    mv = out[:, D_MODEL:].reshape(B, N_MEM, MEM_HEADS, MEM_HD)
    return mk, mv


PEER_RT = 256
PEER_TB = 512
PEER_EB = 1024
PEER_VMEM_LIMIT = 58 * 1024 * 1024


def _top16_rows(s, row_id, exact_ties):
    big = float(2 ** 20)
    out_id = lax.broadcasted_iota(jnp.int32, (PEER_TOPK, s.shape[1]), 0)
    stacked = jnp.zeros((PEER_TOPK, s.shape[1]), jnp.float32)
    rank = jnp.full(s.shape, float(PEER_TOPK), jnp.float32)
    rows = []
    for k in range(PEER_TOPK):
        m = jnp.max(s, axis=0, keepdims=True)
        hit = s == m
        if exact_ties:
            hit = row_id == jnp.min(jnp.where(hit, row_id, big), axis=0, keepdims=True)
        s = jnp.where(hit, NEG_INF, s)
        rank = jnp.where(hit, float(k), rank)
        rows.append(m)
        stacked = jnp.where(out_id == k, m, stacked)
    return rows, stacked, rank, s


def _peer_route_head(s1, s2, exact_ties):
    tb = s1.shape[1]
    row128 = lax.broadcasted_iota(jnp.int32, (PEER_NKEYS, tb), 0).astype(jnp.float32)
    sub8 = lax.broadcasted_iota(jnp.int32, (8, tb), 0)
    sub8f = sub8.astype(jnp.float32)
    r1, v1, rank1, left1 = _top16_rows(s1, row128, exact_ties)
    r2, v2, rank2, left2 = _top16_rows(s2, row128, exact_ties)
    groups, ids = [], []
    for b in range(8):
        lim = PEER_TOPK // (b + 1)
        for a0 in range(0, lim, 8):
            g = v1[a0:a0 + 8] + r2[b]
            if lim - a0 < 8:
                g = jnp.where(sub8 < lim - a0, g, NEG_INF)
            groups.append(g)
            ids.append((sub8f + float(a0)) * float(PEER_TOPK) + float(b))
    groups.append(r1[0] + v2[8:16])
    ids.append(sub8f + 8.0)
    cand = jnp.concatenate(groups, axis=0)
    vals, _, _, left = _top16_rows(cand, jnp.concatenate(ids, axis=0), exact_ties)
    z = jnp.ones_like(vals[0])
    for k in range(1, PEER_TOPK):
        z = z + jnp.exp(vals[k] - vals[0])
    taken = jnp.where((left == NEG_INF) & (cand > NEG_INF), 1.0, 0.0)
    cnt_lo = jnp.zeros((8, tb), jnp.float32)
    gi = 0
    for b in range(8):
        for a0 in range(0, PEER_TOPK // (b + 1), 8):
            if a0 == 0:
                cnt_lo = cnt_lo + taken[gi * 8:(gi + 1) * 8]
            else:
                cnt_hi = taken[gi * 8:(gi + 1) * 8]
            gi += 1
    tail = jnp.sum(taken[gi * 8:(gi + 1) * 8], axis=0, keepdims=True)
    cnt_lo = cnt_lo + jnp.where(sub8 == 0, tail, 0.0)
    lim_full = jnp.full((PEER_NKEYS, tb), -1.0, jnp.float32)
    for a in range(PEER_TOPK):
        cnt = cnt_lo if a < 8 else cnt_hi
        lim_full = jnp.where(rank1 == float(a), cnt[a % 8:a % 8 + 1] - 1.0, lim_full)
    n_taken = (jnp.sum(jnp.where(left1 == NEG_INF, 1.0, 0.0), axis=0, keepdims=True)
               + jnp.sum(jnp.where(left2 == NEG_INF, 1.0, 0.0), axis=0, keepdims=True)
               + jnp.sum(taken, axis=0, keepdims=True))
    return (rank2.astype(jnp.bfloat16), lim_full, jnp.exp(s1 - r1[0]) / z,
            jnp.exp(s2 - r2[0]).astype(jnp.bfloat16), n_taken)


def _peer_route_kernel(n_first, xa_ref, xb_ref, wq_ref, k1_ref, k2_ref, rk2_ref, lim_ref, p1_ref, p2_ref, qt_scr):
    half = PEER_DKEY // 2
    xb = _pick(n_first, xa_ref, xb_ref).astype(jnp.bfloat16)
    qt_scr[...] = lax.dot_general(wq_ref[...], xb, (((1,), (1,)), ((), ())), preferred_element_type=jnp.float32)

    def head(h, carry):
        r0 = pl.multiple_of(h * PEER_DKEY, PEER_DKEY)
        q1 = qt_scr[pl.ds(r0, half), :].astype(jnp.bfloat16)
        q2 = qt_scr[pl.ds(r0 + half, half), :].astype(jnp.bfloat16)
        s1 = jnp.dot(k1_ref[...], q1, preferred_element_type=jnp.float32)
        s2 = jnp.dot(k2_ref[...], q2, preferred_element_type=jnp.float32)

        def emit(exact_ties):
            rk2, lim, p1, p2, n_taken = _peer_route_head(s1, s2, exact_ties)
            rk2_ref[h] = rk2
            lim_ref[h] = lim
            p1_ref[h] = p1
            p2_ref[h] = p2
            return n_taken

        n_taken = emit(False)
        merged = jnp.max(jnp.abs(n_taken - 3.0 * PEER_TOPK)) > 0.0

        @pl.when(merged)
        def _():
            emit(True)

        return carry

    lax.fori_loop(0, PEER_HEADS, head, 0)


def _row_bf16(row):
    r16 = jnp.broadcast_to(row, (16, row.shape[1])).astype(jnp.bfloat16)
    return jnp.concatenate([r16] * (PEER_NKEYS // 16), axis=0)


def _peer_weights(rk2_ref, lim_ref, p1_ref, p2_ref, row0, r, tb):
    w = jnp.zeros((PEER_NKEYS, tb), jnp.bfloat16)
    for h in range(PEER_HEADS):
        p2h = p2_ref[h]
        lim8 = lim_ref[h, pl.ds(row0, 8), :]
        p18 = p1_ref[h, pl.ds(row0, 8), :]
        sel = rk2_ref[h] <= _row_bf16(lim8[r:r + 1, :])
        w = w + jnp.where(sel, p2h, jnp.zeros_like(p2h)) * _row_bf16(p18[r:r + 1, :])
    return w


def _peer_dense_kernel(n_first, xa_ref, xb_ref, rk2_ref, lim_ref, p1_ref, p2_ref, u0_ref, ua_ref, ub_ref, vt_ref, g_ref, b_ref,
                       o_ref, xb_scr, yt_scr, ht0_scr, ht1_scr):
    k = pl.program_id(1)
    n_i1 = PEER_EB // PEER_NKEYS
    tb = xa_ref.shape[0]
    nt = (((1,), (1,)), ((), ()))
    f32 = jnp.float32

    @pl.when(k == 0)
    def _():
        xb_scr[...] = _pick(n_first, xa_ref, xb_ref).astype(jnp.bfloat16)
        yt_scr[...] = jnp.zeros_like(yt_scr)
        ht0_scr[...] = lax.dot_general(u0_ref[...], xb_scr[...], nt, preferred_element_type=f32)

    xb = xb_scr[...]
    n_sub = 2
    sub = PEER_EB // n_sub

    def second_half(ht_scr, blk, vt_off, acc):
        for j in range(n_sub):
            pieces = []
            for cc in range(sub // PEER_NKEYS):
                w = _peer_weights(rk2_ref, lim_ref, p1_ref, p2_ref, pl.multiple_of(blk * n_i1, 8),
                                  j * (sub // PEER_NKEYS) + cc, tb)
                r0 = j * sub + cc * PEER_NKEYS
                hc = ht_scr[r0:r0 + PEER_NKEYS, :]
                gelu = 0.5 * hc * (1.0 + lax.erf(hc * (2.0 ** -0.5)))
                pieces.append(w * gelu.astype(jnp.bfloat16))
            at = jnp.concatenate(pieces, axis=0)
            c0 = vt_off + j * sub
            acc = acc + jnp.dot(vt_ref[:, c0:c0 + sub], at, preferred_element_type=f32)
        return acc

    acc = yt_scr[...]
    ht1_scr[...] = lax.dot_general(ua_ref[...], xb, nt, preferred_element_type=f32)
    acc = second_half(ht0_scr, 2 * k, 0, acc)
    ht0_scr[...] = lax.dot_general(ub_ref[...], xb, nt, preferred_element_type=f32)
    acc = second_half(ht1_scr, 2 * k + 1, PEER_EB, acc)
    yt_scr[...] = acc

    @pl.when(k == pl.num_programs(1) - 1)
    def _():
        z = ALPHA * _pick(n_first, xa_ref, xb_ref) + yt_scr[...].T
        o_ref[...] = _layer_norm_rows(z, g_ref[...], b_ref[...])


def peer_ln(x_a, x_b, w_pq, sub_k1, sub_k2, peer_u, peer_v, ln_g, ln_b):
    T = x_a.shape[0] + x_b.shape[0]
    assert x_a.shape[0] % PEER_TB == 0 and x_b.shape[0] % PEER_TB == 0 and PEER_TB % PEER_RT == 0
    nt = T // PEER_TB
    half = PEER_DKEY // 2
    wq_t = w_pq.reshape(D_MODEL, PEER_HEADS * PEER_DKEY).T.astype(jnp.bfloat16)
    sshape = jax.ShapeDtypeStruct((PEER_HEADS, PEER_NKEYS, T), jnp.float32)
    sspec = pl.BlockSpec((PEER_HEADS, PEER_NKEYS, PEER_RT), lambda j: (0, 0, j))
    rk2, lim, p1, p2 = pl.pallas_call(
        functools.partial(_peer_route_kernel, x_a.shape[0] // PEER_RT),
        grid=(T // PEER_RT,),
        in_specs=[*_split_specs(PEER_RT, D_MODEL, x_a.shape[0] // PEER_RT),
                  pl.BlockSpec((PEER_HEADS * PEER_DKEY, D_MODEL), lambda j: (0, 0)),
                  pl.BlockSpec((PEER_NKEYS, half), lambda j: (0, 0)),
                  pl.BlockSpec((PEER_NKEYS, half), lambda j: (0, 0))],
        out_specs=[sspec, sspec, sspec, sspec],
        out_shape=[jax.ShapeDtypeStruct(sshape.shape, jnp.bfloat16), sshape, sshape,
                   jax.ShapeDtypeStruct(sshape.shape, jnp.bfloat16)],
        scratch_shapes=[pltpu.VMEM((PEER_HEADS * PEER_DKEY, PEER_RT), jnp.float32)],
        compiler_params=pltpu.CompilerParams(dimension_semantics=("arbitrary",), vmem_limit_bytes=VMEM_LIMIT),
        name="peer_route",
    )(x_a, x_b, wq_t, sub_k1.astype(jnp.bfloat16), sub_k2.astype(jnp.bfloat16))

    u_b = peer_u.astype(jnp.bfloat16)
    vt_b = peer_v.T.astype(jnp.bfloat16)
    ne = PEER_N // PEER_EB
    assert ne % 2 == 0
    sspec2 = pl.BlockSpec((PEER_HEADS, PEER_NKEYS, PEER_TB), lambda j, k: (0, 0, j))
    return pl.pallas_call(
        functools.partial(_peer_dense_kernel, x_a.shape[0] // PEER_TB),
        grid=(nt, ne // 2),
        in_specs=[*_split_specs(PEER_TB, D_MODEL, x_a.shape[0] // PEER_TB),
                  sspec2, sspec2, sspec2, sspec2,
                  pl.BlockSpec((PEER_EB, D_MODEL), lambda j, k: (0, 0)),
                  pl.BlockSpec((PEER_EB, D_MODEL), lambda j, k: (2 * k + 1, 0)),
                  pl.BlockSpec((PEER_EB, D_MODEL), lambda j, k: (jnp.minimum(2 * k + 2, ne - 1), 0)),
                  pl.BlockSpec((D_MODEL, 2 * PEER_EB), lambda j, k: (0, k)),
                  pl.BlockSpec((1, D_MODEL), lambda j, k: (0, 0)),
                  pl.BlockSpec((1, D_MODEL), lambda j, k: (0, 0))],
        out_specs=pl.BlockSpec((PEER_TB, D_MODEL), lambda j, k: (j, 0)),
        out_shape=jax.ShapeDtypeStruct((T, D_MODEL), jnp.float32),
        scratch_shapes=[pltpu.VMEM((PEER_TB, D_MODEL), jnp.bfloat16),
                        pltpu.VMEM((D_MODEL, PEER_TB), jnp.float32),
                        pltpu.VMEM((PEER_EB, PEER_TB), jnp.float32),
                        pltpu.VMEM((PEER_EB, PEER_TB), jnp.float32)],
        compiler_params=pltpu.CompilerParams(dimension_semantics=("arbitrary", "arbitrary"),
                                             vmem_limit_bytes=PEER_VMEM_LIMIT),
        name="peer_dense",
    )(x_a, x_b, rk2, lim, p1, p2, u_b, u_b, u_b, vt_b, ln_g.reshape(1, -1), ln_b.reshape(1, -1))


def kernel(x_prompt, x_sample, cache_kv_latent, cache_k_rope, state_C, state_n, state_m,
           cache_mem_k, cache_mem_v, page_table, mem_prompt, ln0_g, ln0_b, w_in, b_i, b_f,
           g_q, w_uq, g_kv, w_uk, w_uv, w_out, ln1_g, ln1_b, w_mq, w_mk, w_mv, w_mo,
           ln2_g, ln2_b, w_pq, sub_k1, sub_k2, peer_u, peer_v, ln3_g, ln3_b):
    B, S = x_prompt.shape[:2]
    NB, TQ = x_sample.shape[:2]
    past = page_table.shape[1] * PAGE_SIZE
    n_p = B * S
    l = 0
    pos = jnp.concatenate([jnp.tile(jnp.arange(S), B), jnp.tile(past + jnp.arange(TQ), NB)])
    (xn, q_lat, q_rope, kv, kvk, kr, mq, mk, mv, gates, o_gate) = mix_in(
        x_prompt.reshape(n_p, D_MODEL), x_sample.reshape(NB * TQ, D_MODEL), pos, ln0_g, ln0_b, w_in[l], b_i[l], b_f[l], g_q[l], w_uq[l], g_kv[l], w_uk[l])

    mla_p = mla_attend_prompt(q_lat, q_rope, kvk, w_uv[l], B, S)
    mlh_p, C_p, n_pst, m_p = mlstm_prompt(mq, mk, mv, gates, B, S)

    kv_s = kv[n_p:].reshape(NB, TQ, MLA_KV_LORA)
    kr_s = kr[n_p:, :MLA_ROPE].reshape(NB, TQ, MLA_ROPE)
    mla_s = mla_attend_sample(q_lat[n_p:].reshape(NB, TQ, -1), q_rope[n_p:].reshape(NB, TQ, -1), kv_s, kr_s,
                              cache_kv_latent.reshape(cache_kv_latent.shape[1:]),
                              jnp.swapaxes(cache_k_rope.reshape(cache_k_rope.shape[1:]), 1, 2), page_table, w_uv[l])
    mlh_s, C_s, n_s, m_s = mlstm_step(mq[n_p:], mk[n_p:], mv[n_p:], gates[n_p:], state_C.reshape(state_C.shape[1:]),
                                      state_n.reshape(state_n.shape[1:]), state_m.reshape(state_m.shape[1:]), NB, TQ)

    x1, qm = mix_out(xn, mla_p, mla_s, mlh_p, mlh_s, o_gate, w_out[l], ln1_g[l], ln1_b[l], w_mq[l])
    mk_p, mv_p = mem_kv(mem_prompt, w_mk[l], w_mv[l])
    x2_p = mem_attend_ln(x1, qm, mk_p.reshape(B, N_MEM, D_MODEL), mv_p.reshape(B, N_MEM, D_MODEL),
                         w_mo[l], ln2_g[l], ln2_b[l], B, S)
    x2_s = mem_attend_ln(x1[n_p:], qm[n_p:], cache_mem_k.reshape(cache_mem_k.shape[1:]),
                         cache_mem_v.reshape(cache_mem_v.shape[1:]), w_mo[l], ln2_g[l], ln2_b[l], NB, TQ)
    x3 = peer_ln(x2_p, x2_s, w_pq[l], sub_k1[l], sub_k2[l], peer_u[l], peer_v[l], ln3_g[l], ln3_b[l])
    st = lambda t: t[None]
    return (x3[:n_p].reshape(B, S, D_MODEL), x3[n_p:].reshape(NB, TQ, D_MODEL),
            st(kv[:n_p].reshape(B, S, MLA_KV_LORA)), st(kr[:n_p, :MLA_ROPE].reshape(B, S, MLA_ROPE)),
            st(C_p), st(n_pst), st(m_p), st(mk_p), st(mv_p),
            st(kv_s), st(kr_s), st(C_s), st(n_s), st(m_s))
```

```python
import functools

import jax, jax.numpy as jnp
from jax import lax
import numpy as np
from jax.experimental import pallas as pl
from jax.experimental.pallas import tpu as pltpu

D_MODEL = 1024
PAGE_SIZE = 128

MLA_HEADS = 8
MLA_NOPE = 64
MLA_ROPE = 32
MLA_V = 64
MLA_KV_LORA = 256
MLA_Q_LORA = 384
MLA_SCALE = (MLA_NOPE + MLA_ROPE) ** -0.5
ROPE_BASE = 10000.0
ML_HEADS = 4
ML_DH = 128
ML_CHUNK = 128
MLA_WIDTH = MLA_HEADS * MLA_V
ML_WIDTH = ML_HEADS * ML_DH
D_MIX = MLA_WIDTH + ML_WIDTH
N_MEM = 256
MEM_HEADS = 4
MEM_HD = D_MODEL // MEM_HEADS
PEER_HEADS = 8
PEER_NKEYS = 128
PEER_N = PEER_NKEYS * PEER_NKEYS
PEER_DKEY = 128
PEER_TOPK = 16
LN_EPS = 1e-5
RMS_EPS = 1e-6
DEPTH = 1
ALPHA = (2 * DEPTH) ** 0.25
NEG_INF = float('-inf')

VMEM_LIMIT = 48 * 1024 * 1024

IN_PAD = 2944
OFF_CQ, OFF_CKV, OFF_KR, OFF_MQ, OFF_MK, OFF_MV, OFF_G, OFF_O = 0, 384, 640, 768, 1280, 1792, 2304, 2432
MIX_TM = 256


def _split_specs(tm, width, n_first):
    return (pl.BlockSpec((tm, width), lambda i, *_: (jnp.minimum(i, n_first - 1), 0)),
            pl.BlockSpec((tm, width), lambda i, *_: (jnp.maximum(i - n_first, 0), 0)))


def _pick(n_first, a_ref, b_ref):
    return jnp.where(pl.program_id(0) < n_first, a_ref[...], b_ref[...])


def _layer_norm_rows(z, g, b):
    mu = jnp.mean(z, axis=-1, keepdims=True)
    zc = z - mu
    var = jnp.mean(zc * zc, axis=-1, keepdims=True)
    return zc * lax.rsqrt(var + LN_EPS) * g + b


def _rope_lanes(x, cos, sin_signed):
    n = x.shape[1]
    lane = lax.broadcasted_iota(jnp.int32, x.shape, 1)
    partner = jnp.where((lane & 31) < 16, pltpu.roll(x, n - 16, axis=1), pltpu.roll(x, 16, axis=1))
    return x * cos + partner * sin_signed


def _mix_in_kernel(n_first, xa_ref, xb_ref, g0_ref, b0_ref, win_ref, gq_ref, wuq_ref, wuk_ref, gkv_ref, cos_ref, sin_ref, gb_ref,
                   xn_ref, ql_ref, qr_ref, kv_ref, kvk_ref, kr_ref, mq_ref, mk_ref, mv_ref, gate_ref, og_ref):
    xn = _layer_norm_rows(_pick(n_first, xa_ref, xb_ref), g0_ref[...], b0_ref[...])
    xn_ref[...] = xn
    z = jnp.dot(xn.astype(jnp.bfloat16), win_ref[...], preferred_element_type=jnp.float32)
    cos = cos_ref[...]
    sin = sin_ref[...]
    cq = z[:, OFF_CQ:OFF_CQ + MLA_Q_LORA]
    cq = cq * lax.rsqrt(jnp.mean(cq * cq, axis=-1, keepdims=True) + RMS_EPS) * gq_ref[...]
    q = jnp.dot(cq.astype(jnp.bfloat16), wuq_ref[...], preferred_element_type=jnp.float32)
    n_nope = MLA_HEADS * MLA_NOPE
    qrope = _rope_lanes(q[:, n_nope:], jnp.concatenate([cos, cos], axis=1), jnp.concatenate([sin, sin], axis=1))
    qr_ref[...] = qrope.astype(jnp.bfloat16)
    ql_ref[...] = jnp.dot(q[:, :n_nope].astype(jnp.bfloat16), wuk_ref[...],
                          preferred_element_type=jnp.float32).astype(jnp.bfloat16)
    ckv = z[:, OFF_CKV:OFF_CKV + MLA_KV_LORA]
    kv = ckv * lax.rsqrt(jnp.mean(ckv * ckv, axis=-1, keepdims=True) + RMS_EPS) * gkv_ref[...]
    kv_ref[...] = kv
    kr = _rope_lanes(z[:, OFF_KR:OFF_KR + 128], cos, sin)
    kr_ref[...] = kr
    krt = kr + pltpu.roll(kr, 32, axis=1) + pltpu.roll(kr, 64, axis=1) + pltpu.roll(kr, 96, axis=1)
    kvk_ref[...] = jnp.concatenate([kv, krt, krt], axis=1).astype(jnp.bfloat16)
    mq_ref[...] = z[:, OFF_MQ:OFF_MQ + ML_WIDTH]
    mk_ref[...] = z[:, OFF_MK:OFF_MK + ML_WIDTH] * (ML_DH ** -0.5)
    mv_ref[...] = z[:, OFF_MV:OFF_MV + ML_WIDTH]
    g = z[:, OFF_G:OFF_G + 128] + gb_ref[...]
    lane = lax.broadcasted_iota(jnp.int32, g.shape, 1)
    gate_ref[...] = jnp.where(lane < ML_HEADS, g, jax.nn.log_sigmoid(g))
    og_ref[...] = jax.nn.sigmoid(z[:, OFF_O:OFF_O + ML_WIDTH])


def mix_in(x_a, x_b, pos, ln0_g, ln0_b, w_in, b_i, b_f, g_q, w_uq, g_kv, w_uk):
    T = x_a.shape[0] + x_b.shape[0]
    n_first = x_a.shape[0] // MIX_TM
    assert x_a.shape[0] % MIX_TM == 0 and x_b.shape[0] % MIX_TM == 0
    f32, bf = jnp.float32, jnp.bfloat16
    zc = lambda n: jnp.zeros((D_MODEL, n), f32)
    win_p = jnp.concatenate([w_in[:, :672], zc(96), w_in[:, 672:2208], w_in[:, 2208:2216], zc(120), w_in[:, 2216:]],
                            axis=1).astype(bf)
    assert win_p.shape[1] == IN_PAD
    wuq_p = jnp.concatenate([w_uq[:, :, :MLA_NOPE].reshape(MLA_Q_LORA, -1),
                             w_uq[:, :, MLA_NOPE:].reshape(MLA_Q_LORA, -1)], axis=1).astype(bf)
    hh = jnp.arange(MLA_HEADS)
    wuk_blk = jnp.zeros((MLA_HEADS, MLA_NOPE, MLA_HEADS, MLA_KV_LORA), f32)
    wuk_blk = wuk_blk.at[hh, :, hh, :].set(jnp.transpose(w_uk, (1, 2, 0)))
    wuk_blk = wuk_blk.reshape(MLA_HEADS * MLA_NOPE, MLA_HEADS * MLA_KV_LORA).astype(bf)
    inv = 1.0 / (ROPE_BASE ** (jnp.arange(0, MLA_ROPE, 2, dtype=f32) / MLA_ROPE))
    ang = pos.astype(f32)[:, None] * inv[None, :]
    c, s = jnp.cos(ang), jnp.sin(ang)
    cos128 = jnp.tile(jnp.concatenate([c, c], axis=1), (1, 4))
    sin128 = jnp.tile(jnp.concatenate([-s, s], axis=1), (1, 4))
    gbias = jnp.concatenate([b_i, b_f, jnp.zeros((120,), f32)]).reshape(1, 128)
    row = lambda n: pl.BlockSpec((MIX_TM, n), lambda i: (i, 0))
    full = lambda a: pl.BlockSpec(a.shape, lambda i: (0,) * a.ndim)
    ins = [x_a, x_b, ln0_g.reshape(1, -1), ln0_b.reshape(1, -1), win_p, g_q.reshape(1, -1), wuq_p, wuk_blk,
           g_kv.reshape(1, -1), cos128, sin128, gbias]
    in_specs = list(_split_specs(MIX_TM, D_MODEL, n_first)) + [full(a) for a in ins[2:9]] + [row(128), row(128), full(gbias)]
    outs = [(D_MODEL, f32), (MLA_HEADS * MLA_KV_LORA, bf), (MLA_HEADS * MLA_ROPE, bf), (MLA_KV_LORA, f32),
            (MLA_KV_LORA + MLA_HEADS * MLA_ROPE, bf), (128, f32), (ML_WIDTH, f32), (ML_WIDTH, f32),
            (ML_WIDTH, f32), (128, f32), (ML_WIDTH, f32)]
    return pl.pallas_call(
        functools.partial(_mix_in_kernel, n_first),
        grid=(T // MIX_TM,),
        in_specs=in_specs,
        out_specs=[row(n) for n, _ in outs],
        out_shape=[jax.ShapeDtypeStruct((T, n), dt) for n, dt in outs],
        compiler_params=pltpu.CompilerParams(dimension_semantics=("arbitrary",), vmem_limit_bytes=VMEM_LIMIT),
        name="mix_in",
    )(*ins)


ATT_BQ = 128
ATT_BK = 512


def _mla_prompt_kernel(ql_ref, qr_ref, kvk_ref, wuv_ref, o_ref, acc_scr, m_scr, l_scr):
    qi = pl.program_id(1)
    ql = jnp.concatenate([ql_ref[:, h * MLA_KV_LORA:(h + 1) * MLA_KV_LORA] for h in range(MLA_HEADS)], axis=0)
    qr_all = qr_ref[...]
    lane_head = lax.broadcasted_iota(jnp.int32, qr_all.shape, 1) // MLA_ROPE
    qr = jnp.concatenate([jnp.where(lane_head == h, qr_all, jnp.zeros_like(qr_all)) for h in range(MLA_HEADS)],
                         axis=0)
    q = jnp.concatenate([ql, qr], axis=1)
    acc_scr[...] = jnp.zeros_like(acc_scr)
    m_scr[...] = jnp.full_like(m_scr, NEG_INF)
    l_scr[...] = jnp.zeros_like(l_scr)
    nt = (((1,), (1,)), ((), ()))
    n_last = (qi * ATT_BQ) // ATT_BK
    rows = q.shape[0]
    tok = qi * ATT_BQ + (lax.broadcasted_iota(jnp.int32, (rows, ATT_BK), 0) & (ATT_BQ - 1))
    col = lax.broadcasted_iota(jnp.int32, (rows, ATT_BK), 1)

    def scores(kj):
        k0 = pl.multiple_of(kj * ATT_BK, ATT_BK)
        s = lax.dot_general(q, kvk_ref[pl.ds(k0, ATT_BK), :], nt, preferred_element_type=jnp.float32) * MLA_SCALE
        return jnp.where(col + k0 <= tok, s, NEG_INF)

    def accumulate(kj, s):
        k0 = pl.multiple_of(kj * ATT_BK, ATT_BK)
        kvb = kvk_ref[pl.ds(k0, ATT_BK), :MLA_KV_LORA]
        m_old = m_scr[...]
        m_new = jnp.maximum(m_old, jnp.max(s, axis=1, keepdims=True))
        alpha = jnp.exp(m_old - m_new)
        p = jnp.exp(s - m_new)
        l_scr[...] = alpha * l_scr[...] + jnp.sum(p, axis=1, keepdims=True)
        acc_scr[...] = alpha * acc_scr[...] + jnp.dot(p.astype(jnp.bfloat16), kvb,
                                                      preferred_element_type=jnp.float32)
        m_scr[...] = m_new

    def body(kj, s_cur):
        s_next = scores(kj + 1)
        accumulate(kj, s_cur)
        return s_next

    s_last = lax.fori_loop(0, n_last, body, scores(0))
    accumulate(n_last, s_last)
    o = (acc_scr[...] / l_scr[...]).astype(jnp.bfloat16)
    r = jnp.dot(o, wuv_ref[...], preferred_element_type=jnp.float32)
    col_head = lax.broadcasted_iota(jnp.int32, (ATT_BQ, MLA_WIDTH), 1) // MLA_V
    out = jnp.zeros((ATT_BQ, MLA_WIDTH), jnp.float32)
    for h in range(MLA_HEADS):
        out = out + jnp.where(col_head == h, r[h * ATT_BQ:(h + 1) * ATT_BQ], 0.0)
    o_ref[...] = out


def mla_attend_prompt(q_lat, q_rope, kvk, w_uv, n_seq, seq):
    assert seq % ATT_BK == 0 and ATT_BK % ATT_BQ == 0
    H, C = MLA_HEADS, MLA_KV_LORA
    nq = seq // ATT_BQ
    rows = H * ATT_BQ
    return pl.pallas_call(
        _mla_prompt_kernel,
        grid=(n_seq, nq),
        in_specs=[pl.BlockSpec((ATT_BQ, H * C), lambda b, i: (b * nq + i, 0)),
                  pl.BlockSpec((ATT_BQ, H * MLA_ROPE), lambda b, i: (b * nq + i, 0)),
                  pl.BlockSpec((seq, C + H * MLA_ROPE), lambda b, i: (b, 0)),
                  pl.BlockSpec((C, H * MLA_V), lambda b, i: (0, 0))],
        out_specs=pl.BlockSpec((ATT_BQ, H * MLA_V), lambda b, i: (b * nq + i, 0)),
        out_shape=jax.ShapeDtypeStruct((n_seq * seq, H * MLA_V), jnp.float32),
        scratch_shapes=[pltpu.VMEM((rows, C), jnp.float32),
                        pltpu.VMEM((rows, 1), jnp.float32),
                        pltpu.VMEM((rows, 1), jnp.float32)],
        compiler_params=pltpu.CompilerParams(dimension_semantics=("arbitrary", "arbitrary"),
                                             vmem_limit_bytes=VMEM_LIMIT),
        name="mla_prompt",
    )(q_lat, q_rope, kvk, w_uv.reshape(C, H * MLA_V).astype(jnp.bfloat16))


SMP_KC = 2048


def _mla_sample_kernel(pt_ref, ql_ref, qr_ref, kvn_ref, krn_ref, wuv_ref, lat_hbm, rope_hbm, o_ref,
                       lat_buf, rope_buf, lat_bf, sem_lat, sem_rope):
    b = pl.program_id(0)
    nb = pl.num_programs(0)
    n_pages = pt_ref.shape[1]
    rows = ql_ref.shape[0]
    tq = kvn_ref.shape[0]
    n_keys = n_pages * PAGE_SIZE

    def page_copies(seq, slot, p):
        page = pt_ref[seq, p]
        dst = pl.ds(p * PAGE_SIZE, PAGE_SIZE)
        return (pltpu.make_async_copy(lat_hbm.at[page], lat_buf.at[slot, dst], sem_lat.at[slot]),
                pltpu.make_async_copy(rope_hbm.at[page], rope_buf.at[slot, :, dst], sem_rope.at[slot]))

    def start_fetch(seq, slot):
        for p in range(n_pages):
            for cp in page_copies(seq, slot, p):
                cp.start()

    def wait_fetch(seq, slot):
        for p in range(n_pages):
            for cp in page_copies(seq, slot, p):
                cp.wait()

    slot = b % 2

    @pl.when(b == 0)
    def _():
        start_fetch(0, 0)

    @pl.when(b + 1 < nb)
    def _():
        start_fetch(b + 1, 1 - slot)

    wait_fetch(b, slot)

    bf, f32 = jnp.bfloat16, jnp.float32
    ql = ql_ref[...]
    qr = qr_ref[...]
    nt = (((1,), (1,)), ((), ()))
    parts = []
    for c in range(n_keys // SMP_KC):
        r = pl.ds(c * SMP_KC, SMP_KC)
        lb = lat_buf[slot, r, :].astype(bf)
        rb = rope_buf[slot, :, r].astype(bf)
        lat_bf[r, :] = lb
        parts.append(lax.dot_general(ql, lb, nt, preferred_element_type=f32)
                     + jnp.dot(qr, rb, preferred_element_type=f32))
    s_past = jnp.concatenate(parts, axis=1) * MLA_SCALE
    qlf, qrf = ql.astype(f32), qr.astype(f32)
    kvn = kvn_ref[...].astype(f32)
    krn = krn_ref[...].astype(f32)
    tok = lax.broadcasted_iota(jnp.int32, (rows, 1), 0) % tq
    s_new = []
    for j in range(tq):
        sj = (jnp.sum(qlf * kvn[j:j + 1, :], axis=1, keepdims=True)
              + jnp.sum(qrf * krn[j:j + 1, :], axis=1, keepdims=True)) * MLA_SCALE
        s_new.append(jnp.where(tok >= j, sj, NEG_INF))
    m = jnp.max(s_past, axis=1, keepdims=True)
    for sj in s_new:
        m = jnp.maximum(m, sj)
    p_past = jnp.exp(s_past - m)
    p_new = [jnp.exp(sj - m) for sj in s_new]
    l = jnp.sum(p_past, axis=1, keepdims=True)
    for pj in p_new:
        l = l + pj
    inv = 1.0 / l
    o = jnp.dot((p_past * inv).astype(bf), lat_bf[...], preferred_element_type=f32)
    for j in range(tq):
        o = o + (p_new[j] * inv).astype(bf).astype(f32) * kvn[j:j + 1, :]
    r = jnp.dot(o.astype(bf), wuv_ref[...], preferred_element_type=f32)
    col_head = lax.broadcasted_iota(jnp.int32, (tq, MLA_WIDTH), 1) // MLA_V
    out = jnp.zeros((tq, MLA_WIDTH), f32)
    for h in range(MLA_HEADS):
        out = out + jnp.where(col_head == h, r[h * tq:(h + 1) * tq, :], 0.0)
    o_ref[...] = out


def mla_attend_sample(q_lat, q_rope, kv_new, kr_new, pool_lat, pool_rope_t, page_table, w_uv):
    NB, T = q_lat.shape[:2]
    H, C, R = MLA_HEADS, MLA_KV_LORA, MLA_ROPE
    bf = jnp.bfloat16
    n_pages = page_table.shape[1]
    n_keys = n_pages * PAGE_SIZE
    assert n_keys % SMP_KC == 0
    ql = q_lat.reshape(NB, T, H, C).transpose(0, 2, 1, 3).reshape(NB, H * T, C)
    qr = q_rope.reshape(NB, T, H, R).transpose(0, 2, 1, 3).reshape(NB, H * T, R)
    seq = lambda n, w: pl.BlockSpec((None, n, w), lambda b, pt: (b, 0, 0))
    grid_spec = pltpu.PrefetchScalarGridSpec(
        num_scalar_prefetch=1,
        grid=(NB,),
        in_specs=[seq(H * T, C), seq(H * T, R), seq(T, C), seq(T, R),
                  pl.BlockSpec((C, H * MLA_V), lambda b, pt: (0, 0)),
                  pl.BlockSpec(memory_space=pl.ANY), pl.BlockSpec(memory_space=pl.ANY)],
        out_specs=seq(T, H * MLA_V),
        scratch_shapes=[pltpu.VMEM((2, n_keys, C), jnp.float32), pltpu.VMEM((2, R, n_keys), jnp.float32),
                        pltpu.VMEM((n_keys, C), bf),
                        pltpu.SemaphoreType.DMA((2,)), pltpu.SemaphoreType.DMA((2,))],
    )
    out = pl.pallas_call(
        _mla_sample_kernel,
        grid_spec=grid_spec,
        out_shape=jax.ShapeDtypeStruct((NB, T, H * MLA_V), jnp.float32),
        compiler_params=pltpu.CompilerParams(dimension_semantics=("arbitrary",), vmem_limit_bytes=VMEM_LIMIT),
        name="mla_sample",
    )(page_table, ql, qr, kv_new.astype(bf), kr_new.astype(bf), w_uv.reshape(C, H * MLA_V).astype(bf), pool_lat, pool_rope_t)
    return out.reshape(NB * T, H * MLA_V)


def _split3(x):
    hi = x.astype(jnp.bfloat16)
    r1 = x - hi.astype(jnp.float32)
    mid = r1.astype(jnp.bfloat16)
    lo = (r1 - mid.astype(jnp.float32)).astype(jnp.bfloat16)
    return hi, mid, lo


def _mlstm_chunk(q_all, k_all, v_all, gates, c_refs, n_refs, m_refs):
    L = q_all.shape[0]
    bf, f32 = jnp.bfloat16, jnp.float32
    row_t = lax.broadcasted_iota(jnp.int32, (L, L), 0)
    col_s = lax.broadcasted_iota(jnp.int32, (L, L), 1)
    causal = col_s <= row_t
    tril = jnp.where(causal, 1.0, 0.0).astype(bf)
    cum = sum(jnp.dot(tril, part, preferred_element_type=f32) for part in _split3(gates))
    gates_t = gates.T
    cum_t = cum.T
    nt = (((1,), (1,)), ((), ()))
    outs = []
    for h in range(ML_HEADS):
        c = slice(h * ML_DH, (h + 1) * ML_DH)
        q, k, v = q_all[:, c], k_all[:, c], v_all[:, c]
        C, n, m = c_refs[0](h), n_refs[0](h), m_refs[0](h)
        ig_col = gates[:, h:h + 1]
        b_col = cum[:, ML_HEADS + h:ML_HEADS + h + 1]
        ig_row = gates_t[h:h + 1, :]
        b_row = cum_t[ML_HEADS + h:ML_HEADS + h + 1, :]
        D = jnp.where(causal, b_col - b_row + ig_row, NEG_INF)
        inter = b_col + m
        m_t = jnp.maximum(inter, jnp.max(D, axis=1, keepdims=True))
        qb, kb, vb = q.astype(bf), k.astype(bf), v.astype(bf)
        A = jnp.exp(D - m_t) * lax.dot_general(qb, kb, nt, preferred_element_type=f32)
        w_inter = jnp.exp(inter - m_t)
        num = w_inter * jnp.dot(qb, C.astype(bf), preferred_element_type=f32) \
            + jnp.dot(A.astype(bf), vb, preferred_element_type=f32)
        qn = jnp.sum(qb.astype(f32) * n.astype(bf).astype(f32), axis=1, keepdims=True)
        den = w_inter * qn + jnp.sum(A, axis=1, keepdims=True)
        outs.append(num / jnp.maximum(jnp.abs(den), jnp.exp(-m_t)))
        b_end = b_col[L - 1:L, :]
        m_new = jnp.maximum(b_end + m, jnp.max(b_end - b_row + ig_row, axis=1, keepdims=True))
        a_prev = jnp.exp(b_end + m - m_new)
        kw = k * jnp.exp(b_end - b_col + ig_col - m_new)
        c_refs[1](h, a_prev * C + jnp.dot(kw.T.astype(bf), vb, preferred_element_type=f32))
        n_refs[1](h, a_prev * n + jnp.sum(kw, axis=0, keepdims=True))
        m_refs[1](h, m_new)
    return jnp.concatenate(outs, axis=1)


def _state_access(c_get, n_get, m_get, c_set, n_set, m_set):
    c_refs = (lambda h: c_get[h], lambda h, val: c_set.__setitem__(h, val))
    n_refs = (lambda h: n_get[h:h + 1, :], lambda h, val: n_set.__setitem__((slice(h, h + 1), slice(None)), val))
    m_refs = (lambda h: m_get[h:h + 1, 0:1],
              lambda h, val: m_set.__setitem__((slice(h, h + 1), slice(None)), jnp.broadcast_to(val, (1, 128))))
    return c_refs, n_refs, m_refs


def _mlstm_prompt_kernel(q_ref, k_ref, v_ref, g_ref, h_ref, c_out, n_out, m_out, c_scr, n_scr, m_scr):
    j = pl.program_id(1)

    @pl.when(j == 0)
    def _():
        c_scr[...] = jnp.zeros_like(c_scr)
        n_scr[...] = jnp.zeros_like(n_scr)
        m_scr[...] = jnp.zeros_like(m_scr)

    h_ref[...] = _mlstm_chunk(q_ref[...], k_ref[...], v_ref[...], g_ref[...],
                              *_state_access(c_scr, n_scr, m_scr, c_scr, n_scr, m_scr))

    @pl.when(j == pl.num_programs(1) - 1)
    def _():
        c_out[...] = c_scr[...]
        n_out[...] = n_scr[0:ML_HEADS, :]
        m_out[...] = m_scr[0:ML_HEADS, :]


def mlstm_prompt(mq, mk, mv, gates, n_seq, seq):
    nc = seq // ML_CHUNK
    f32 = jnp.float32
    tok = lambda w: pl.BlockSpec((ML_CHUNK, w), lambda b, j: (b * nc + j, 0))
    h, C, n, m = pl.pallas_call(
        _mlstm_prompt_kernel,
        grid=(n_seq, nc),
        in_specs=[tok(ML_WIDTH), tok(ML_WIDTH), tok(ML_WIDTH), tok(128)],
        out_specs=[tok(ML_WIDTH),
                   pl.BlockSpec((None, ML_HEADS, ML_DH, ML_DH), lambda b, j: (b, 0, 0, 0)),
                   pl.BlockSpec((None, ML_HEADS, ML_DH), lambda b, j: (b, 0, 0)),
                   pl.BlockSpec((None, ML_HEADS, 128), lambda b, j: (b, 0, 0))],
        out_shape=[jax.ShapeDtypeStruct((n_seq * seq, ML_WIDTH), f32),
                   jax.ShapeDtypeStruct((n_seq, ML_HEADS, ML_DH, ML_DH), f32),
                   jax.ShapeDtypeStruct((n_seq, ML_HEADS, ML_DH), f32),
                   jax.ShapeDtypeStruct((n_seq, ML_HEADS, 128), f32)],
        scratch_shapes=[pltpu.VMEM((ML_HEADS, ML_DH, ML_DH), f32), pltpu.VMEM((8, ML_DH), f32),
                        pltpu.VMEM((8, 128), f32)],
        compiler_params=pltpu.CompilerParams(dimension_semantics=("arbitrary", "arbitrary"),
                                             vmem_limit_bytes=VMEM_LIMIT),
        name="mlstm_prompt",
    )(mq, mk, mv, gates)
    return h, C, n, m[:, :, 0]


def _mlstm_step_kernel(q_ref, k_ref, v_ref, g_ref, c_in, n_in, m_in, h_ref, c_out, n_out, m_out):
    h_ref[...] = _mlstm_chunk(q_ref[...], k_ref[...], v_ref[...], g_ref[...],
                              *_state_access(c_in, n_in, m_in, c_out, n_out, m_out))


def mlstm_step(mq, mk, mv, gates, state_C, state_n, state_m, n_seq, rows):
    f32 = jnp.float32
    tok = lambda w: pl.BlockSpec((None, rows, w), lambda b: (b, 0, 0))
    st_c = pl.BlockSpec((None, ML_HEADS, ML_DH, ML_DH), lambda b: (b, 0, 0, 0))
    st_n = pl.BlockSpec((None, ML_HEADS, ML_DH), lambda b: (b, 0, 0))
    st_m = pl.BlockSpec((None, ML_HEADS, 128), lambda b: (b, 0, 0))
    r3 = lambda t: t.reshape(n_seq, rows, t.shape[-1])
    m_in = jnp.broadcast_to(state_m[:, :, None], (n_seq, ML_HEADS, 128))
    h, C, n, m = pl.pallas_call(
        _mlstm_step_kernel,
        grid=(n_seq,),
        in_specs=[tok(ML_WIDTH), tok(ML_WIDTH), tok(ML_WIDTH), tok(128), st_c, st_n, st_m],
        out_specs=[tok(ML_WIDTH), st_c, st_n, st_m],
        out_shape=[jax.ShapeDtypeStruct((n_seq, rows, ML_WIDTH), f32),
                   jax.ShapeDtypeStruct((n_seq, ML_HEADS, ML_DH, ML_DH), f32),
                   jax.ShapeDtypeStruct((n_seq, ML_HEADS, ML_DH), f32),
                   jax.ShapeDtypeStruct((n_seq, ML_HEADS, 128), f32)],
        compiler_params=pltpu.CompilerParams(dimension_semantics=("arbitrary",), vmem_limit_bytes=VMEM_LIMIT),
        name="mlstm_step",
    )(r3(mq), r3(mk), r3(mv), r3(gates), state_C, state_n, m_in)
    return h.reshape(n_seq * rows, ML_WIDTH), C, n, m[:, :, 0]


POST_TM = 256


def _mix_out_kernel(n_first, xn_ref, mla_a, mla_b, mlh_a, mlh_b, og_ref, wout_ref, g_ref, b_ref, wmq_ref, x1_ref, qm_ref):
    mixed = jnp.concatenate([_pick(n_first, mla_a, mla_b), og_ref[...] * _pick(n_first, mlh_a, mlh_b)],
                            axis=1).astype(jnp.bfloat16)
    mix = jnp.dot(mixed, wout_ref[...], preferred_element_type=jnp.float32)
    x1 = _layer_norm_rows(ALPHA * xn_ref[...] + mix, g_ref[...], b_ref[...])
    x1_ref[...] = x1
    qm_ref[...] = jnp.dot(x1.astype(jnp.bfloat16), wmq_ref[...],
                          preferred_element_type=jnp.float32).astype(jnp.bfloat16)


def mix_out(xn, mla_a, mla_b, mlh_a, mlh_b, o_gate, w_out, ln_g, ln_b, w_mq):
    T = xn.shape[0]
    n_first = mla_a.shape[0] // POST_TM
    assert mla_a.shape[0] % POST_TM == 0
    bf = jnp.bfloat16
    row = lambda n: pl.BlockSpec((POST_TM, n), lambda i: (i, 0))
    full = lambda shape: pl.BlockSpec(shape, lambda i: (0,) * len(shape))
    return pl.pallas_call(
        functools.partial(_mix_out_kernel, n_first),
        grid=(T // POST_TM,),
        in_specs=[row(D_MODEL), *_split_specs(POST_TM, MLA_WIDTH, n_first), *_split_specs(POST_TM, ML_WIDTH, n_first),
                  row(ML_WIDTH), full((D_MIX, D_MODEL)),
                  full((1, D_MODEL)), full((1, D_MODEL)), full((D_MODEL, D_MODEL))],
        out_specs=[row(D_MODEL), row(D_MODEL)],
        out_shape=[jax.ShapeDtypeStruct((T, D_MODEL), jnp.float32), jax.ShapeDtypeStruct((T, D_MODEL), bf)],
        compiler_params=pltpu.CompilerParams(dimension_semantics=("arbitrary",), vmem_limit_bytes=VMEM_LIMIT),
        name="mix_out",
    )(xn, mla_a, mla_b, mlh_a, mlh_b, o_gate, w_out.astype(bf), ln_g.reshape(1, -1), ln_b.reshape(1, -1),
      w_mq.reshape(D_MODEL, D_MODEL).astype(bf))


def _mem_attn_kernel(x1_ref, qm_ref, mk_ref, mv_ref, wmo_ref, g_ref, b_ref, x2_ref):
    q = qm_ref[...]
    nt = (((1,), (1,)), ((), ()))
    outs = []
    for h in range(MEM_HEADS):
        c = slice(h * MEM_HD, (h + 1) * MEM_HD)
        mk, mv = mk_ref[:, c].astype(jnp.bfloat16), mv_ref[:, c].astype(jnp.bfloat16)
        s = lax.dot_general(q[:, c], mk, nt, preferred_element_type=jnp.float32) * (MEM_HD ** -0.5)
        m = jnp.max(s, axis=-1, keepdims=True)
        p = jnp.exp(s - m)
        p = p / jnp.sum(p, axis=-1, keepdims=True)
        outs.append(jnp.dot(p.astype(jnp.bfloat16), mv, preferred_element_type=jnp.float32))
    o = jnp.concatenate(outs, axis=1).astype(jnp.bfloat16)
    att = jnp.dot(o, wmo_ref[...], preferred_element_type=jnp.float32)
    x2_ref[...] = _layer_norm_rows(ALPHA * x1_ref[...] + att, g_ref[...], b_ref[...])


def _mem_attn_sample_kernel(x1_ref, qm_ref, mk_ref, mv_ref, wmo_ref, g_ref, b_ref, x2_ref, o_scr):
    s_id = pl.program_id(0)
    tq = qm_ref.shape[0]
    bf, f32 = jnp.bfloat16, jnp.float32
    q = qm_ref[...]
    qs = jnp.concatenate([q[:, h * MEM_HD:(h + 1) * MEM_HD] for h in range(MEM_HEADS)], axis=0)
    kall = mk_ref[...].reshape(N_MEM * MEM_HEADS, MEM_HD).astype(bf)
    vall = mv_ref[...].reshape(N_MEM * MEM_HEADS, MEM_HD).astype(bf)
    s = lax.dot_general(qs, kall, (((1,), (1,)), ((), ())), preferred_element_type=f32) * (MEM_HD ** -0.5)
    row_h = lax.broadcasted_iota(jnp.int32, s.shape, 0) // tq
    col_h = lax.broadcasted_iota(jnp.int32, s.shape, 1) % MEM_HEADS
    s = jnp.where(row_h == col_h, s, NEG_INF)
    p = jnp.exp(s - jnp.max(s, axis=-1, keepdims=True))
    p = p / jnp.sum(p, axis=-1, keepdims=True)
    o = jnp.dot(p.astype(bf), vall, preferred_element_type=f32)
    o_scr[s_id] = jnp.concatenate([o[h * tq:(h + 1) * tq] for h in range(MEM_HEADS)], axis=1)

    @pl.when(s_id == pl.num_programs(0) - 1)
    def _():
        o_all = o_scr[...].reshape(x1_ref.shape).astype(bf)
        att = jnp.dot(o_all, wmo_ref[...], preferred_element_type=f32)
        x2_ref[...] = _layer_norm_rows(ALPHA * x1_ref[...] + att, g_ref[...], b_ref[...])


def mem_attend_ln(x1, qm, mem_k, mem_v, w_mo, ln_g, ln_b, n_seq, rows_per_seq):
    bf = jnp.bfloat16
    wmo = w_mo.reshape(D_MODEL, D_MODEL).astype(bf)
    g, b = ln_g.reshape(1, -1), ln_b.reshape(1, -1)
    cp = pltpu.CompilerParams(dimension_semantics=("arbitrary",) * 2, vmem_limit_bytes=VMEM_LIMIT)
    mem = pl.BlockSpec((None, N_MEM, D_MODEL), lambda s, i: (s, 0, 0))
    full = lambda shape: pl.BlockSpec(shape, lambda s, i: (0,) * len(shape))
    if rows_per_seq % POST_TM == 0:
        nb = rows_per_seq // POST_TM
        tok = pl.BlockSpec((POST_TM, D_MODEL), lambda s, i: (s * nb + i, 0))
        return pl.pallas_call(
            _mem_attn_kernel, grid=(n_seq, nb),
            in_specs=[tok, tok, mem, mem, full((D_MODEL, D_MODEL)), full((1, D_MODEL)), full((1, D_MODEL))],
            out_specs=tok, out_shape=jax.ShapeDtypeStruct((n_seq * rows_per_seq, D_MODEL), jnp.float32),
            compiler_params=cp, name="mem_attn_prompt",
        )(x1, qm, mem_k, mem_v, wmo, g, b)
    rows = n_seq * rows_per_seq
    mem = pl.BlockSpec((None, N_MEM, MEM_HEADS, MEM_HD), lambda s: (s, 0, 0, 0))
    res = pl.BlockSpec((rows, D_MODEL), lambda s: (0, 0))
    full = lambda shape: pl.BlockSpec(shape, lambda s: (0,) * len(shape))
    return pl.pallas_call(
        _mem_attn_sample_kernel, grid=(n_seq,),
        in_specs=[res, pl.BlockSpec((None, rows_per_seq, D_MODEL), lambda s: (s, 0, 0)), mem, mem,
                  full((D_MODEL, D_MODEL)), full((1, D_MODEL)), full((1, D_MODEL))],
        out_specs=res, out_shape=jax.ShapeDtypeStruct((rows, D_MODEL), jnp.float32),
        scratch_shapes=[pltpu.VMEM((n_seq, rows_per_seq, D_MODEL), jnp.float32)],
        compiler_params=pltpu.CompilerParams(dimension_semantics=("arbitrary",), vmem_limit_bytes=VMEM_LIMIT),
        name="mem_attn_sample",
    )(x1, qm.reshape(n_seq, rows_per_seq, D_MODEL), mem_k, mem_v, wmo, g, b)


def _mem_kv_kernel(m_ref, w_ref, o_ref):
    o_ref[...] = jnp.dot(m_ref[...].astype(jnp.bfloat16), w_ref[...], preferred_element_type=jnp.float32)


def mem_kv(mem, w_mk, w_mv):
    B = mem.shape[0]
    w = jnp.concatenate([w_mk.reshape(D_MODEL, D_MODEL), w_mv.reshape(D_MODEL, D_MODEL)], axis=1).astype(jnp.bfloat16)
    out = pl.pallas_call(
        _mem_kv_kernel, grid=(B,),
        in_specs=[pl.BlockSpec((N_MEM, D_MODEL), lambda i: (i, 0)), pl.BlockSpec((D_MODEL, 2 * D_MODEL), lambda i: (0, 0))],
        out_specs=pl.BlockSpec((N_MEM, 2 * D_MODEL), lambda i: (i, 0)),
        out_shape=jax.ShapeDtypeStruct((B * N_MEM, 2 * D_MODEL), jnp.float32),
        compiler_params=pltpu.CompilerParams(dimension_semantics=("arbitrary",), vmem_limit_bytes=VMEM_LIMIT),
        name="mem_kv",
    )(mem.reshape(B * N_MEM, D_MODEL), w)
    mk = out[:, :D_MODEL].reshape(B, N_MEM, MEM_HEADS, MEM_HD)
    mv = out[:, D_MODEL:].reshape(B, N_MEM, MEM_HEADS, MEM_HD)
    return mk, mv


PEER_RT = 256
PEER_TB = 512
PEER_EB = 1024
PEER_VMEM_LIMIT = 58 * 1024 * 1024


def _top16_rows(s, row_id, exact_ties):
    big = float(2 ** 20)
    out_id = lax.broadcasted_iota(jnp.int32, (PEER_TOPK, s.shape[1]), 0)
    stacked = jnp.zeros((PEER_TOPK, s.shape[1]), jnp.float32)
    rank = jnp.full(s.shape, float(PEER_TOPK), jnp.float32)
    rows = []
    for k in range(PEER_TOPK):
        m = jnp.max(s, axis=0, keepdims=True)
        hit = s == m
        if exact_ties:
            hit = row_id == jnp.min(jnp.where(hit, row_id, big), axis=0, keepdims=True)
        s = jnp.where(hit, NEG_INF, s)
        rank = jnp.where(hit, float(k), rank)
        rows.append(m)
        stacked = jnp.where(out_id == k, m, stacked)
    return rows, stacked, rank, s


def _peer_route_head(s1, s2, exact_ties):
    tb = s1.shape[1]
    row128 = lax.broadcasted_iota(jnp.int32, (PEER_NKEYS, tb), 0).astype(jnp.float32)
    sub8 = lax.broadcasted_iota(jnp.int32, (8, tb), 0)
    sub8f = sub8.astype(jnp.float32)
    r1, v1, rank1, left1 = _top16_rows(s1, row128, exact_ties)
    r2, v2, rank2, left2 = _top16_rows(s2, row128, exact_ties)
    groups, ids = [], []
    for b in range(8):
        lim = PEER_TOPK // (b + 1)
        for a0 in range(0, lim, 8):
            g = v1[a0:a0 + 8] + r2[b]
            if lim - a0 < 8:
                g = jnp.where(sub8 < lim - a0, g, NEG_INF)
            groups.append(g)
            ids.append((sub8f + float(a0)) * float(PEER_TOPK) + float(b))
    groups.append(r1[0] + v2[8:16])
    ids.append(sub8f + 8.0)
    cand = jnp.concatenate(groups, axis=0)
    vals, _, _, left = _top16_rows(cand, jnp.concatenate(ids, axis=0), exact_ties)
    z = jnp.ones_like(vals[0])
    for k in range(1, PEER_TOPK):
        z = z + jnp.exp(vals[k] - vals[0])
    taken = jnp.where((left == NEG_INF) & (cand > NEG_INF), 1.0, 0.0)
    cnt_lo = jnp.zeros((8, tb), jnp.float32)
    gi = 0
    for b in range(8):
        for a0 in range(0, PEER_TOPK // (b + 1), 8):
            if a0 == 0:
                cnt_lo = cnt_lo + taken[gi * 8:(gi + 1) * 8]
            else:
                cnt_hi = taken[gi * 8:(gi + 1) * 8]
            gi += 1
    tail = jnp.sum(taken[gi * 8:(gi + 1) * 8], axis=0, keepdims=True)
    cnt_lo = cnt_lo + jnp.where(sub8 == 0, tail, 0.0)
    lim_full = jnp.full((PEER_NKEYS, tb), -1.0, jnp.float32)
    for a in range(PEER_TOPK):
        cnt = cnt_lo if a < 8 else cnt_hi
        lim_full = jnp.where(rank1 == float(a), cnt[a % 8:a % 8 + 1] - 1.0, lim_full)
    n_taken = (jnp.sum(jnp.where(left1 == NEG_INF, 1.0, 0.0), axis=0, keepdims=True)
               + jnp.sum(jnp.where(left2 == NEG_INF, 1.0, 0.0), axis=0, keepdims=True)
               + jnp.sum(taken, axis=0, keepdims=True))
    return (rank2.astype(jnp.bfloat16), lim_full, jnp.exp(s1 - r1[0]) / z,
            jnp.exp(s2 - r2[0]).astype(jnp.bfloat16), n_taken)


def _peer_route_kernel(n_first, xa_ref, xb_ref, wq_ref, k1_ref, k2_ref, rk2_ref, lim_ref, p1_ref, p2_ref, qt_scr):
    half = PEER_DKEY // 2
    xb = _pick(n_first, xa_ref, xb_ref).astype(jnp.bfloat16)
    qt_scr[...] = lax.dot_general(wq_ref[...], xb, (((1,), (1,)), ((), ())), preferred_element_type=jnp.float32)

    def head(h, carry):
        r0 = pl.multiple_of(h * PEER_DKEY, PEER_DKEY)
        q1 = qt_scr[pl.ds(r0, half), :].astype(jnp.bfloat16)
        q2 = qt_scr[pl.ds(r0 + half, half), :].astype(jnp.bfloat16)
        s1 = jnp.dot(k1_ref[...], q1, preferred_element_type=jnp.float32)
        s2 = jnp.dot(k2_ref[...], q2, preferred_element_type=jnp.float32)

        def emit(exact_ties):
            rk2, lim, p1, p2, n_taken = _peer_route_head(s1, s2, exact_ties)
            rk2_ref[h] = rk2
            lim_ref[h] = lim
            p1_ref[h] = p1
            p2_ref[h] = p2
            return n_taken

        n_taken = emit(False)
        merged = jnp.max(jnp.abs(n_taken - 3.0 * PEER_TOPK)) > 0.0

        @pl.when(merged)
        def _():
            emit(True)

        return carry

    lax.fori_loop(0, PEER_HEADS, head, 0)


def _row_bf16(row):
    r16 = jnp.broadcast_to(row, (16, row.shape[1])).astype(jnp.bfloat16)
    return jnp.concatenate([r16] * (PEER_NKEYS // 16), axis=0)


def _peer_weights(rk2_ref, lim_ref, p1_ref, p2_ref, row0, r, tb):
    w = jnp.zeros((PEER_NKEYS, tb), jnp.bfloat16)
    for h in range(PEER_HEADS):
        p2h = p2_ref[h]
        lim8 = lim_ref[h, pl.ds(row0, 8), :]
        p18 = p1_ref[h, pl.ds(row0, 8), :]
        sel = rk2_ref[h] <= _row_bf16(lim8[r:r + 1, :])
        w = w + jnp.where(sel, p2h, jnp.zeros_like(p2h)) * _row_bf16(p18[r:r + 1, :])
    return w


def _peer_dense_kernel(n_first, xa_ref, xb_ref, rk2_ref, lim_ref, p1_ref, p2_ref, u0_ref, ua_ref, ub_ref, vt_ref, g_ref, b_ref,
                       oa_ref, ob_ref, xb_scr, yt_scr, ht0_scr, ht1_scr):
    k = pl.program_id(1)
    n_i1 = PEER_EB // PEER_NKEYS
    tb = xa_ref.shape[0]
    nt = (((1,), (1,)), ((), ()))
    f32 = jnp.float32

    @pl.when(k == 0)
    def _():
        xb_scr[...] = _pick(n_first, xa_ref, xb_ref).astype(jnp.bfloat16)
        yt_scr[...] = jnp.zeros_like(yt_scr)
        ht0_scr[...] = lax.dot_general(u0_ref[...], xb_scr[...], nt, preferred_element_type=f32)

    xb = xb_scr[...]
    n_sub = 2
    sub = PEER_EB // n_sub

    def second_half(ht_scr, blk, vt_off, acc):
        for j in range(n_sub):
            pieces = []
            for cc in range(sub // PEER_NKEYS):
                w = _peer_weights(rk2_ref, lim_ref, p1_ref, p2_ref, pl.multiple_of(blk * n_i1, 8),
                                  j * (sub // PEER_NKEYS) + cc, tb)
                r0 = j * sub + cc * PEER_NKEYS
                hc = ht_scr[r0:r0 + PEER_NKEYS, :]
                gelu = 0.5 * hc * (1.0 + lax.erf(hc * (2.0 ** -0.5)))
                pieces.append(w * gelu.astype(jnp.bfloat16))
            at = jnp.concatenate(pieces, axis=0)
            c0 = vt_off + j * sub
            acc = acc + jnp.dot(vt_ref[:, c0:c0 + sub], at, preferred_element_type=f32)
        return acc

    acc = yt_scr[...]
    ht1_scr[...] = lax.dot_general(ua_ref[...], xb, nt, preferred_element_type=f32)
    acc = second_half(ht0_scr, 2 * k, 0, acc)
    ht0_scr[...] = lax.dot_general(ub_ref[...], xb, nt, preferred_element_type=f32)
    acc = second_half(ht1_scr, 2 * k + 1, PEER_EB, acc)
    yt_scr[...] = acc

    @pl.when(k == pl.num_programs(1) - 1)
    def _():
        z = ALPHA * _pick(n_first, xa_ref, xb_ref) + yt_scr[...].T
        res = _layer_norm_rows(z, g_ref[...], b_ref[...])
        first = pl.program_id(0) < n_first

        @pl.when(first)
        def _():
            oa_ref[...] = res

        @pl.when(jnp.logical_not(first))
        def _():
            ob_ref[...] = res


def peer_ln(x_a, x_b, w_pq, sub_k1, sub_k2, peer_u, peer_v, ln_g, ln_b):
    T = x_a.shape[0] + x_b.shape[0]
    assert x_a.shape[0] % PEER_TB == 0 and x_b.shape[0] % PEER_TB == 0 and PEER_TB % PEER_RT == 0
    nt = T // PEER_TB
    half = PEER_DKEY // 2
    wq_t = w_pq.reshape(D_MODEL, PEER_HEADS * PEER_DKEY).T.astype(jnp.bfloat16)
    sshape = jax.ShapeDtypeStruct((PEER_HEADS, PEER_NKEYS, T), jnp.float32)
    sspec = pl.BlockSpec((PEER_HEADS, PEER_NKEYS, PEER_RT), lambda j: (0, 0, j))
    rk2, lim, p1, p2 = pl.pallas_call(
        functools.partial(_peer_route_kernel, x_a.shape[0] // PEER_RT),
        grid=(T // PEER_RT,),
        in_specs=[*_split_specs(PEER_RT, D_MODEL, x_a.shape[0] // PEER_RT),
                  pl.BlockSpec((PEER_HEADS * PEER_DKEY, D_MODEL), lambda j: (0, 0)),
                  pl.BlockSpec((PEER_NKEYS, half), lambda j: (0, 0)),
                  pl.BlockSpec((PEER_NKEYS, half), lambda j: (0, 0))],
        out_specs=[sspec, sspec, sspec, sspec],
        out_shape=[jax.ShapeDtypeStruct(sshape.shape, jnp.bfloat16), sshape, sshape,
                   jax.ShapeDtypeStruct(sshape.shape, jnp.bfloat16)],
        scratch_shapes=[pltpu.VMEM((PEER_HEADS * PEER_DKEY, PEER_RT), jnp.float32)],
        compiler_params=pltpu.CompilerParams(dimension_semantics=("arbitrary",), vmem_limit_bytes=VMEM_LIMIT),
        name="peer_route",
    )(x_a, x_b, wq_t, sub_k1.astype(jnp.bfloat16), sub_k2.astype(jnp.bfloat16))

    u_b = peer_u.astype(jnp.bfloat16)
    vt_b = peer_v.T.astype(jnp.bfloat16)
    ne = PEER_N // PEER_EB
    assert ne % 2 == 0
    sspec2 = pl.BlockSpec((PEER_HEADS, PEER_NKEYS, PEER_TB), lambda j, k: (0, 0, j))
    return pl.pallas_call(
        functools.partial(_peer_dense_kernel, x_a.shape[0] // PEER_TB),
        grid=(nt, ne // 2),
        in_specs=[*_split_specs(PEER_TB, D_MODEL, x_a.shape[0] // PEER_TB),
                  sspec2, sspec2, sspec2, sspec2,
                  pl.BlockSpec((PEER_EB, D_MODEL), lambda j, k: (0, 0)),
                  pl.BlockSpec((PEER_EB, D_MODEL), lambda j, k: (2 * k + 1, 0)),
                  pl.BlockSpec((PEER_EB, D_MODEL), lambda j, k: (jnp.minimum(2 * k + 2, ne - 1), 0)),
                  pl.BlockSpec((D_MODEL, 2 * PEER_EB), lambda j, k: (0, k)),
                  pl.BlockSpec((1, D_MODEL), lambda j, k: (0, 0)),
                  pl.BlockSpec((1, D_MODEL), lambda j, k: (0, 0))],
        out_specs=list(_split_specs(PEER_TB, D_MODEL, x_a.shape[0] // PEER_TB)),
        out_shape=[jax.ShapeDtypeStruct(x_a.shape, jnp.float32), jax.ShapeDtypeStruct(x_b.shape, jnp.float32)],
        scratch_shapes=[pltpu.VMEM((PEER_TB, D_MODEL), jnp.bfloat16),
                        pltpu.VMEM((D_MODEL, PEER_TB), jnp.float32),
                        pltpu.VMEM((PEER_EB, PEER_TB), jnp.float32),
                        pltpu.VMEM((PEER_EB, PEER_TB), jnp.float32)],
        compiler_params=pltpu.CompilerParams(dimension_semantics=("arbitrary", "arbitrary"),
                                             vmem_limit_bytes=PEER_VMEM_LIMIT),
        name="peer_dense",
    )(x_a, x_b, rk2, lim, p1, p2, u_b, u_b, u_b, vt_b, ln_g.reshape(1, -1), ln_b.reshape(1, -1))


def kernel(x_prompt, x_sample, cache_kv_latent, cache_k_rope, state_C, state_n, state_m,
           cache_mem_k, cache_mem_v, page_table, mem_prompt, ln0_g, ln0_b, w_in, b_i, b_f,
           g_q, w_uq, g_kv, w_uk, w_uv, w_out, ln1_g, ln1_b, w_mq, w_mk, w_mv, w_mo,
           ln2_g, ln2_b, w_pq, sub_k1, sub_k2, peer_u, peer_v, ln3_g, ln3_b):
    B, S = x_prompt.shape[:2]
    NB, TQ = x_sample.shape[:2]
    past = page_table.shape[1] * PAGE_SIZE
    n_p = B * S
    l = 0
    pos = jnp.concatenate([jnp.tile(jnp.arange(S), B), jnp.tile(past + jnp.arange(TQ), NB)])
    (xn, q_lat, q_rope, kv, kvk, kr, mq, mk, mv, gates, o_gate) = mix_in(
        x_prompt.reshape(n_p, D_MODEL), x_sample.reshape(NB * TQ, D_MODEL), pos, ln0_g, ln0_b, w_in[l], b_i[l], b_f[l], g_q[l], w_uq[l], g_kv[l], w_uk[l])

    mla_p = mla_attend_prompt(q_lat, q_rope, kvk, w_uv[l], B, S)
    mlh_p, C_p, n_pst, m_p = mlstm_prompt(mq, mk, mv, gates, B, S)

    kv_s = kv[n_p:].reshape(NB, TQ, MLA_KV_LORA)
    kr_s = kr[n_p:, :MLA_ROPE].reshape(NB, TQ, MLA_ROPE)
    mla_s = mla_attend_sample(q_lat[n_p:].reshape(NB, TQ, -1), q_rope[n_p:].reshape(NB, TQ, -1), kv_s, kr_s,
                              cache_kv_latent.reshape(cache_kv_latent.shape[1:]),
                              jnp.swapaxes(cache_k_rope.reshape(cache_k_rope.shape[1:]), 1, 2), page_table, w_uv[l])
    mlh_s, C_s, n_s, m_s = mlstm_step(mq[n_p:], mk[n_p:], mv[n_p:], gates[n_p:], state_C.reshape(state_C.shape[1:]),
                                      state_n.reshape(state_n.shape[1:]), state_m.reshape(state_m.shape[1:]), NB, TQ)

    x1, qm = mix_out(xn, mla_p, mla_s, mlh_p, mlh_s, o_gate, w_out[l], ln1_g[l], ln1_b[l], w_mq[l])
    mk_p, mv_p = mem_kv(mem_prompt, w_mk[l], w_mv[l])
    x2_p = mem_attend_ln(x1, qm, mk_p.reshape(B, N_MEM, D_MODEL), mv_p.reshape(B, N_MEM, D_MODEL),
                         w_mo[l], ln2_g[l], ln2_b[l], B, S)
    x2_s = mem_attend_ln(x1[n_p:], qm[n_p:], cache_mem_k.reshape(cache_mem_k.shape[1:]),
                         cache_mem_v.reshape(cache_mem_v.shape[1:]), w_mo[l], ln2_g[l], ln2_b[l], NB, TQ)
    x3_p, x3_s = peer_ln(x2_p, x2_s, w_pq[l], sub_k1[l], sub_k2[l], peer_u[l], peer_v[l], ln3_g[l], ln3_b[l])
    st = lambda t: t[None]
    return (x3_p.reshape(B, S, D_MODEL), x3_s.reshape(NB, TQ, D_MODEL),
            st(kv[:n_p].reshape(B, S, MLA_KV_LORA)), st(kr[:n_p, :MLA_ROPE].reshape(B, S, MLA_ROPE)),
            st(C_p), st(n_pst), st(m_p), st(mk_p), st(mv_p),
            st(kv_s), st(kr_s), st(C_s), st(n_s), st(m_s))
```

```python
import functools

import jax, jax.numpy as jnp
from jax import lax
import numpy as np
from jax.experimental import pallas as pl
from jax.experimental.pallas import tpu as pltpu

D_MODEL = 1024
PAGE_SIZE = 128

MLA_HEADS = 8
MLA_NOPE = 64
MLA_ROPE = 32
MLA_V = 64
MLA_KV_LORA = 256
MLA_Q_LORA = 384
MLA_SCALE = (MLA_NOPE + MLA_ROPE) ** -0.5
ROPE_BASE = 10000.0
ML_HEADS = 4
ML_DH = 128
ML_CHUNK = 256
MLA_WIDTH = MLA_HEADS * MLA_V
ML_WIDTH = ML_HEADS * ML_DH
D_MIX = MLA_WIDTH + ML_WIDTH
N_MEM = 256
MEM_HEADS = 4
MEM_HD = D_MODEL // MEM_HEADS
PEER_HEADS = 8
PEER_NKEYS = 128
PEER_N = PEER_NKEYS * PEER_NKEYS
PEER_DKEY = 128
PEER_TOPK = 16
LN_EPS = 1e-5
RMS_EPS = 1e-6
DEPTH = 1
ALPHA = (2 * DEPTH) ** 0.25
NEG_INF = float('-inf')

VMEM_LIMIT = 48 * 1024 * 1024

IN_PAD = 2944
OFF_CQ, OFF_CKV, OFF_KR, OFF_MQ, OFF_MK, OFF_MV, OFF_G, OFF_O = 0, 384, 640, 768, 1280, 1792, 2304, 2432
MIX_TM = 256


def _split_specs(tm, width, n_first):
    return (pl.BlockSpec((tm, width), lambda i, *_: (jnp.minimum(i, n_first - 1), 0)),
            pl.BlockSpec((tm, width), lambda i, *_: (jnp.maximum(i - n_first, 0), 0)))


def _pick(n_first, a_ref, b_ref):
    return jnp.where(pl.program_id(0) < n_first, a_ref[...], b_ref[...])


def _layer_norm_rows(z, g, b):
    mu = jnp.mean(z, axis=-1, keepdims=True)
    zc = z - mu
    var = jnp.mean(zc * zc, axis=-1, keepdims=True)
    return zc * lax.rsqrt(var + LN_EPS) * g + b


def _rope_lanes(x, cos, sin_signed):
    n = x.shape[1]
    lane = lax.broadcasted_iota(jnp.int32, x.shape, 1)
    partner = jnp.where((lane & 31) < 16, pltpu.roll(x, n - 16, axis=1), pltpu.roll(x, 16, axis=1))
    return x * cos + partner * sin_signed


def _mix_in_kernel(n_first, xa_ref, xb_ref, g0_ref, b0_ref, win_ref, gq_ref, wuq_ref, wuk_ref, gkv_ref, cos_ref, sin_ref, gb_ref,
                   xn_ref, ql_ref, qr_ref, kv_ref, kvk_ref, kr_ref, mq_ref, mk_ref, mv_ref, gate_ref, og_ref):
    xn = _layer_norm_rows(_pick(n_first, xa_ref, xb_ref), g0_ref[...], b0_ref[...])
    xn_ref[...] = xn
    z = jnp.dot(xn.astype(jnp.bfloat16), win_ref[...], preferred_element_type=jnp.float32)
    cos = cos_ref[...]
    sin = sin_ref[...]
    cq = z[:, OFF_CQ:OFF_CQ + MLA_Q_LORA]
    cq = cq * lax.rsqrt(jnp.mean(cq * cq, axis=-1, keepdims=True) + RMS_EPS) * gq_ref[...]
    q = jnp.dot(cq.astype(jnp.bfloat16), wuq_ref[...], preferred_element_type=jnp.float32)
    n_nope = MLA_HEADS * MLA_NOPE
    qrope = _rope_lanes(q[:, n_nope:], jnp.concatenate([cos, cos], axis=1), jnp.concatenate([sin, sin], axis=1))
    qr_ref[...] = qrope.astype(jnp.bfloat16)
    ql_ref[...] = jnp.dot(q[:, :n_nope].astype(jnp.bfloat16), wuk_ref[...],
                          preferred_element_type=jnp.float32).astype(jnp.bfloat16)
    ckv = z[:, OFF_CKV:OFF_CKV + MLA_KV_LORA]
    kv = ckv * lax.rsqrt(jnp.mean(ckv * ckv, axis=-1, keepdims=True) + RMS_EPS) * gkv_ref[...]
    kv_ref[...] = kv
    kr = _rope_lanes(z[:, OFF_KR:OFF_KR + 128], cos, sin)
    kr_ref[...] = kr
    krt = kr + pltpu.roll(kr, 32, axis=1) + pltpu.roll(kr, 64, axis=1) + pltpu.roll(kr, 96, axis=1)
    kvk_ref[...] = jnp.concatenate([kv, krt, krt], axis=1).astype(jnp.bfloat16)
    mq_ref[...] = z[:, OFF_MQ:OFF_MQ + ML_WIDTH]
    mk_ref[...] = z[:, OFF_MK:OFF_MK + ML_WIDTH] * (ML_DH ** -0.5)
    mv_ref[...] = z[:, OFF_MV:OFF_MV + ML_WIDTH]
    g = z[:, OFF_G:OFF_G + 128] + gb_ref[...]
    lane = lax.broadcasted_iota(jnp.int32, g.shape, 1)
    gate_ref[...] = jnp.where(lane < ML_HEADS, g, jax.nn.log_sigmoid(g))
    og_ref[...] = jax.nn.sigmoid(z[:, OFF_O:OFF_O + ML_WIDTH])


def mix_in(x_a, x_b, pos, ln0_g, ln0_b, w_in, b_i, b_f, g_q, w_uq, g_kv, w_uk):
    T = x_a.shape[0] + x_b.shape[0]
    n_first = x_a.shape[0] // MIX_TM
    assert x_a.shape[0] % MIX_TM == 0 and x_b.shape[0] % MIX_TM == 0
    f32, bf = jnp.float32, jnp.bfloat16
    zc = lambda n: jnp.zeros((D_MODEL, n), f32)
    win_p = jnp.concatenate([w_in[:, :672], zc(96), w_in[:, 672:2208], w_in[:, 2208:2216], zc(120), w_in[:, 2216:]],
                            axis=1).astype(bf)
    assert win_p.shape[1] == IN_PAD
    wuq_p = jnp.concatenate([w_uq[:, :, :MLA_NOPE].reshape(MLA_Q_LORA, -1),
                             w_uq[:, :, MLA_NOPE:].reshape(MLA_Q_LORA, -1)], axis=1).astype(bf)
    hh = jnp.arange(MLA_HEADS)
    wuk_blk = jnp.zeros((MLA_HEADS, MLA_NOPE, MLA_HEADS, MLA_KV_LORA), f32)
    wuk_blk = wuk_blk.at[hh, :, hh, :].set(jnp.transpose(w_uk, (1, 2, 0)))
    wuk_blk = wuk_blk.reshape(MLA_HEADS * MLA_NOPE, MLA_HEADS * MLA_KV_LORA).astype(bf)
    inv = 1.0 / (ROPE_BASE ** (jnp.arange(0, MLA_ROPE, 2, dtype=f32) / MLA_ROPE))
    ang = pos.astype(f32)[:, None] * inv[None, :]
    c, s = jnp.cos(ang), jnp.sin(ang)
    cos128 = jnp.tile(jnp.concatenate([c, c], axis=1), (1, 4))
    sin128 = jnp.tile(jnp.concatenate([-s, s], axis=1), (1, 4))
    gbias = jnp.concatenate([b_i, b_f, jnp.zeros((120,), f32)]).reshape(1, 128)
    row = lambda n: pl.BlockSpec((MIX_TM, n), lambda i: (i, 0))
    full = lambda a: pl.BlockSpec(a.shape, lambda i: (0,) * a.ndim)
    ins = [x_a, x_b, ln0_g.reshape(1, -1), ln0_b.reshape(1, -1), win_p, g_q.reshape(1, -1), wuq_p, wuk_blk,
           g_kv.reshape(1, -1), cos128, sin128, gbias]
    in_specs = list(_split_specs(MIX_TM, D_MODEL, n_first)) + [full(a) for a in ins[2:9]] + [row(128), row(128), full(gbias)]
    outs = [(D_MODEL, f32), (MLA_HEADS * MLA_KV_LORA, bf), (MLA_HEADS * MLA_ROPE, bf), (MLA_KV_LORA, f32),
            (MLA_KV_LORA + MLA_HEADS * MLA_ROPE, bf), (128, f32), (ML_WIDTH, f32), (ML_WIDTH, f32),
            (ML_WIDTH, f32), (128, f32), (ML_WIDTH, f32)]
    return pl.pallas_call(
        functools.partial(_mix_in_kernel, n_first),
        grid=(T // MIX_TM,),
        in_specs=in_specs,
        out_specs=[row(n) for n, _ in outs],
        out_shape=[jax.ShapeDtypeStruct((T, n), dt) for n, dt in outs],
        compiler_params=pltpu.CompilerParams(dimension_semantics=("arbitrary",), vmem_limit_bytes=VMEM_LIMIT),
        name="mix_in",
    )(*ins)


ATT_BQ = 128
ATT_BK = 512


def _mla_prompt_kernel(ql_ref, qr_ref, kvk_ref, wuv_ref, o_ref, acc_scr, m_scr, l_scr):
    qi = pl.program_id(1)
    ql = jnp.concatenate([ql_ref[:, h * MLA_KV_LORA:(h + 1) * MLA_KV_LORA] for h in range(MLA_HEADS)], axis=0)
    qr_all = qr_ref[...]
    lane_head = lax.broadcasted_iota(jnp.int32, qr_all.shape, 1) // MLA_ROPE
    qr = jnp.concatenate([jnp.where(lane_head == h, qr_all, jnp.zeros_like(qr_all)) for h in range(MLA_HEADS)],
                         axis=0)
    q = jnp.concatenate([ql, qr], axis=1)
    acc_scr[...] = jnp.zeros_like(acc_scr)
    m_scr[...] = jnp.full_like(m_scr, NEG_INF)
    l_scr[...] = jnp.zeros_like(l_scr)
    nt = (((1,), (1,)), ((), ()))
    n_last = (qi * ATT_BQ) // ATT_BK
    rows = q.shape[0]
    tok = qi * ATT_BQ + (lax.broadcasted_iota(jnp.int32, (rows, ATT_BK), 0) & (ATT_BQ - 1))
    col = lax.broadcasted_iota(jnp.int32, (rows, ATT_BK), 1)

    def scores(kj):
        k0 = pl.multiple_of(kj * ATT_BK, ATT_BK)
        s = lax.dot_general(q, kvk_ref[pl.ds(k0, ATT_BK), :], nt, preferred_element_type=jnp.float32) * MLA_SCALE
        return jnp.where(col + k0 <= tok, s, NEG_INF)

    def accumulate(kj, s):
        k0 = pl.multiple_of(kj * ATT_BK, ATT_BK)
        kvb = kvk_ref[pl.ds(k0, ATT_BK), :MLA_KV_LORA]
        m_old = m_scr[...]
        m_new = jnp.maximum(m_old, jnp.max(s, axis=1, keepdims=True))
        alpha = jnp.exp(m_old - m_new)
        p = jnp.exp(s - m_new)
        l_scr[...] = alpha * l_scr[...] + jnp.sum(p, axis=1, keepdims=True)
        acc_scr[...] = alpha * acc_scr[...] + jnp.dot(p.astype(jnp.bfloat16), kvb,
                                                      preferred_element_type=jnp.float32)
        m_scr[...] = m_new

    def body(kj, s_cur):
        s_next = scores(kj + 1)
        accumulate(kj, s_cur)
        return s_next

    s_last = lax.fori_loop(0, n_last, body, scores(0))
    accumulate(n_last, s_last)
    o = (acc_scr[...] / l_scr[...]).astype(jnp.bfloat16)
    r = jnp.dot(o, wuv_ref[...], preferred_element_type=jnp.float32)
    col_head = lax.broadcasted_iota(jnp.int32, (ATT_BQ, MLA_WIDTH), 1) // MLA_V
    out = jnp.zeros((ATT_BQ, MLA_WIDTH), jnp.float32)
    for h in range(MLA_HEADS):
        out = out + jnp.where(col_head == h, r[h * ATT_BQ:(h + 1) * ATT_BQ], 0.0)
    o_ref[...] = out


def mla_attend_prompt(q_lat, q_rope, kvk, w_uv, n_seq, seq):
    assert seq % ATT_BK == 0 and ATT_BK % ATT_BQ == 0
    H, C = MLA_HEADS, MLA_KV_LORA
    nq = seq // ATT_BQ
    rows = H * ATT_BQ
    return pl.pallas_call(
        _mla_prompt_kernel,
        grid=(n_seq, nq),
        in_specs=[pl.BlockSpec((ATT_BQ, H * C), lambda b, i: (b * nq + i, 0)),
                  pl.BlockSpec((ATT_BQ, H * MLA_ROPE), lambda b, i: (b * nq + i, 0)),
                  pl.BlockSpec((seq, C + H * MLA_ROPE), lambda b, i: (b, 0)),
                  pl.BlockSpec((C, H * MLA_V), lambda b, i: (0, 0))],
        out_specs=pl.BlockSpec((ATT_BQ, H * MLA_V), lambda b, i: (b * nq + i, 0)),
        out_shape=jax.ShapeDtypeStruct((n_seq * seq, H * MLA_V), jnp.float32),
        scratch_shapes=[pltpu.VMEM((rows, C), jnp.float32),
                        pltpu.VMEM((rows, 1), jnp.float32),
                        pltpu.VMEM((rows, 1), jnp.float32)],
        compiler_params=pltpu.CompilerParams(dimension_semantics=("arbitrary", "arbitrary"),
                                             vmem_limit_bytes=VMEM_LIMIT),
        name="mla_prompt",
    )(q_lat, q_rope, kvk, w_uv.reshape(C, H * MLA_V).astype(jnp.bfloat16))


SMP_KC = 2048


def _mla_sample_kernel(pt_ref, ql_ref, qr_ref, kvn_ref, krn_ref, wuv_ref, lat_hbm, rope_hbm, o_ref,
                       lat_buf, rope_buf, lat_bf, sem_lat, sem_rope):
    b = pl.program_id(0)
    nb = pl.num_programs(0)
    n_pages = pt_ref.shape[1]
    rows = ql_ref.shape[0]
    tq = kvn_ref.shape[0]
    n_keys = n_pages * PAGE_SIZE

    def page_copies(seq, slot, p):
        page = pt_ref[seq, p]
        dst = pl.ds(p * PAGE_SIZE, PAGE_SIZE)
        return (pltpu.make_async_copy(lat_hbm.at[page], lat_buf.at[slot, dst], sem_lat.at[slot]),
                pltpu.make_async_copy(rope_hbm.at[page], rope_buf.at[slot, :, dst], sem_rope.at[slot]))

    def start_fetch(seq, slot):
        for p in range(n_pages):
            for cp in page_copies(seq, slot, p):
                cp.start()

    def wait_fetch(seq, slot):
        for p in range(n_pages):
            for cp in page_copies(seq, slot, p):
                cp.wait()

    slot = b % 2

    @pl.when(b == 0)
    def _():
        start_fetch(0, 0)

    @pl.when(b + 1 < nb)
    def _():
        start_fetch(b + 1, 1 - slot)

    wait_fetch(b, slot)

    bf, f32 = jnp.bfloat16, jnp.float32
    ql = ql_ref[...]
    qr = qr_ref[...]
    nt = (((1,), (1,)), ((), ()))
    parts = []
    for c in range(n_keys // SMP_KC):
        r = pl.ds(c * SMP_KC, SMP_KC)
        lb = lat_buf[slot, r, :].astype(bf)
        rb = rope_buf[slot, :, r].astype(bf)
        lat_bf[r, :] = lb
        parts.append(lax.dot_general(ql, lb, nt, preferred_element_type=f32)
                     + jnp.dot(qr, rb, preferred_element_type=f32))
    s_past = jnp.concatenate(parts, axis=1) * MLA_SCALE
    qlf, qrf = ql.astype(f32), qr.astype(f32)
    kvn = kvn_ref[...].astype(f32)
    krn = krn_ref[...].astype(f32)
    tok = lax.broadcasted_iota(jnp.int32, (rows, 1), 0) % tq
    s_new = []
    for j in range(tq):
        sj = (jnp.sum(qlf * kvn[j:j + 1, :], axis=1, keepdims=True)
              + jnp.sum(qrf * krn[j:j + 1, :], axis=1, keepdims=True)) * MLA_SCALE
        s_new.append(jnp.where(tok >= j, sj, NEG_INF))
    m = jnp.max(s_past, axis=1, keepdims=True)
    for sj in s_new:
        m = jnp.maximum(m, sj)
    p_past = jnp.exp(s_past - m)
    p_new = [jnp.exp(sj - m) for sj in s_new]
    l = jnp.sum(p_past, axis=1, keepdims=True)
    for pj in p_new:
        l = l + pj
    inv = 1.0 / l
    o = jnp.dot((p_past * inv).astype(bf), lat_bf[...], preferred_element_type=f32)
    for j in range(tq):
        o = o + (p_new[j] * inv).astype(bf).astype(f32) * kvn[j:j + 1, :]
    r = jnp.dot(o.astype(bf), wuv_ref[...], preferred_element_type=f32)
    col_head = lax.broadcasted_iota(jnp.int32, (tq, MLA_WIDTH), 1) // MLA_V
    out = jnp.zeros((tq, MLA_WIDTH), f32)
    for h in range(MLA_HEADS):
        out = out + jnp.where(col_head == h, r[h * tq:(h + 1) * tq, :], 0.0)
    o_ref[...] = out


def mla_attend_sample(q_lat, q_rope, kv_new, kr_new, pool_lat, pool_rope_t, page_table, w_uv):
    NB, T = q_lat.shape[:2]
    H, C, R = MLA_HEADS, MLA_KV_LORA, MLA_ROPE
    bf = jnp.bfloat16
    n_pages = page_table.shape[1]
    n_keys = n_pages * PAGE_SIZE
    assert n_keys % SMP_KC == 0
    ql = q_lat.reshape(NB, T, H, C).transpose(0, 2, 1, 3).reshape(NB, H * T, C)
    qr = q_rope.reshape(NB, T, H, R).transpose(0, 2, 1, 3).reshape(NB, H * T, R)
    seq = lambda n, w: pl.BlockSpec((None, n, w), lambda b, pt: (b, 0, 0))
    grid_spec = pltpu.PrefetchScalarGridSpec(
        num_scalar_prefetch=1,
        grid=(NB,),
        in_specs=[seq(H * T, C), seq(H * T, R), seq(T, C), seq(T, R),
                  pl.BlockSpec((C, H * MLA_V), lambda b, pt: (0, 0)),
                  pl.BlockSpec(memory_space=pl.ANY), pl.BlockSpec(memory_space=pl.ANY)],
        out_specs=seq(T, H * MLA_V),
        scratch_shapes=[pltpu.VMEM((2, n_keys, C), jnp.float32), pltpu.VMEM((2, R, n_keys), jnp.float32),
                        pltpu.VMEM((n_keys, C), bf),
                        pltpu.SemaphoreType.DMA((2,)), pltpu.SemaphoreType.DMA((2,))],
    )
    out = pl.pallas_call(
        _mla_sample_kernel,
        grid_spec=grid_spec,
        out_shape=jax.ShapeDtypeStruct((NB, T, H * MLA_V), jnp.float32),
        compiler_params=pltpu.CompilerParams(dimension_semantics=("arbitrary",), vmem_limit_bytes=VMEM_LIMIT),
        name="mla_sample",
    )(page_table, ql, qr, kv_new.astype(bf), kr_new.astype(bf), w_uv.reshape(C, H * MLA_V).astype(bf), pool_lat, pool_rope_t)
    return out.reshape(NB * T, H * MLA_V)


def _split3(x):
    hi = x.astype(jnp.bfloat16)
    r1 = x - hi.astype(jnp.float32)
    mid = r1.astype(jnp.bfloat16)
    lo = (r1 - mid.astype(jnp.float32)).astype(jnp.bfloat16)
    return hi, mid, lo


def _mlstm_chunk(q_all, k_all, v_all, gates, c_refs, n_refs, m_refs):
    L = q_all.shape[0]
    bf, f32 = jnp.bfloat16, jnp.float32
    row_t = lax.broadcasted_iota(jnp.int32, (L, L), 0)
    col_s = lax.broadcasted_iota(jnp.int32, (L, L), 1)
    causal = col_s <= row_t
    tril = jnp.where(causal, 1.0, 0.0).astype(bf)
    cum = sum(jnp.dot(tril, part, preferred_element_type=f32) for part in _split3(gates))
    gates_t = gates.T
    cum_t = cum.T
    nt = (((1,), (1,)), ((), ()))
    outs = []
    for h in range(ML_HEADS):
        c = slice(h * ML_DH, (h + 1) * ML_DH)
        q, k, v = q_all[:, c], k_all[:, c], v_all[:, c]
        C, n, m = c_refs[0](h), n_refs[0](h), m_refs[0](h)
        ig_col = gates[:, h:h + 1]
        b_col = cum[:, ML_HEADS + h:ML_HEADS + h + 1]
        ig_row = gates_t[h:h + 1, :]
        b_row = cum_t[ML_HEADS + h:ML_HEADS + h + 1, :]
        D = jnp.where(causal, b_col - b_row + ig_row, NEG_INF)
        inter = b_col + m
        m_t = jnp.maximum(inter, jnp.max(D, axis=1, keepdims=True))
        qb, kb, vb = q.astype(bf), k.astype(bf), v.astype(bf)
        A = jnp.exp(D - m_t) * lax.dot_general(qb, kb, nt, preferred_element_type=f32)
        w_inter = jnp.exp(inter - m_t)
        num = w_inter * jnp.dot(qb, C.astype(bf), preferred_element_type=f32) \
            + jnp.dot(A.astype(bf), vb, preferred_element_type=f32)
        qn = jnp.sum(qb.astype(f32) * n.astype(bf).astype(f32), axis=1, keepdims=True)
        den = w_inter * qn + jnp.sum(A, axis=1, keepdims=True)
        outs.append(num / jnp.maximum(jnp.abs(den), jnp.exp(-m_t)))
        b_end = b_col[L - 1:L, :]
        m_new = jnp.maximum(b_end + m, jnp.max(b_end - b_row + ig_row, axis=1, keepdims=True))
        a_prev = jnp.exp(b_end + m - m_new)
        kw = k * jnp.exp(b_end - b_col + ig_col - m_new)
        c_refs[1](h, a_prev * C + jnp.dot(kw.T.astype(bf), vb, preferred_element_type=f32))
        n_refs[1](h, a_prev * n + jnp.sum(kw, axis=0, keepdims=True))
        m_refs[1](h, m_new)
    return jnp.concatenate(outs, axis=1)


def _state_access(c_get, n_get, m_get, c_set, n_set, m_set):
    c_refs = (lambda h: c_get[h], lambda h, val: c_set.__setitem__(h, val))
    n_refs = (lambda h: n_get[h:h + 1, :], lambda h, val: n_set.__setitem__((slice(h, h + 1), slice(None)), val))
    m_refs = (lambda h: m_get[h:h + 1, 0:1],
              lambda h, val: m_set.__setitem__((slice(h, h + 1), slice(None)), jnp.broadcast_to(val, (1, 128))))
    return c_refs, n_refs, m_refs


def _mlstm_prompt_kernel(q_ref, k_ref, v_ref, g_ref, h_ref, c_out, n_out, m_out, c_scr, n_scr, m_scr):
    j = pl.program_id(1)

    @pl.when(j == 0)
    def _():
        c_scr[...] = jnp.zeros_like(c_scr)
        n_scr[...] = jnp.zeros_like(n_scr)
        m_scr[...] = jnp.zeros_like(m_scr)

    h_ref[...] = _mlstm_chunk(q_ref[...], k_ref[...], v_ref[...], g_ref[...],
                              *_state_access(c_scr, n_scr, m_scr, c_scr, n_scr, m_scr))

    @pl.when(j == pl.num_programs(1) - 1)
    def _():
        c_out[...] = c_scr[...]
        n_out[...] = n_scr[0:ML_HEADS, :]
        m_out[...] = m_scr[0:ML_HEADS, :]


def mlstm_prompt(mq, mk, mv, gates, n_seq, seq):
    nc = seq // ML_CHUNK
    f32 = jnp.float32
    tok = lambda w: pl.BlockSpec((ML_CHUNK, w), lambda b, j: (b * nc + j, 0))
    h, C, n, m = pl.pallas_call(
        _mlstm_prompt_kernel,
        grid=(n_seq, nc),
        in_specs=[tok(ML_WIDTH), tok(ML_WIDTH), tok(ML_WIDTH), tok(128)],
        out_specs=[tok(ML_WIDTH),
                   pl.BlockSpec((None, ML_HEADS, ML_DH, ML_DH), lambda b, j: (b, 0, 0, 0)),
                   pl.BlockSpec((None, ML_HEADS, ML_DH), lambda b, j: (b, 0, 0)),
                   pl.BlockSpec((None, ML_HEADS, 128), lambda b, j: (b, 0, 0))],
        out_shape=[jax.ShapeDtypeStruct((n_seq * seq, ML_WIDTH), f32),
                   jax.ShapeDtypeStruct((n_seq, ML_HEADS, ML_DH, ML_DH), f32),
                   jax.ShapeDtypeStruct((n_seq, ML_HEADS, ML_DH), f32),
                   jax.ShapeDtypeStruct((n_seq, ML_HEADS, 128), f32)],
        scratch_shapes=[pltpu.VMEM((ML_HEADS, ML_DH, ML_DH), f32), pltpu.VMEM((8, ML_DH), f32),
                        pltpu.VMEM((8, 128), f32)],
        compiler_params=pltpu.CompilerParams(dimension_semantics=("arbitrary", "arbitrary"),
                                             vmem_limit_bytes=VMEM_LIMIT),
        name="mlstm_prompt",
    )(mq, mk, mv, gates)
    return h, C, n, m[:, :, 0]


def _mlstm_step_kernel(q_ref, k_ref, v_ref, g_ref, c_in, n_in, m_in, h_ref, c_out, n_out, m_out):
    h_ref[...] = _mlstm_chunk(q_ref[...], k_ref[...], v_ref[...], g_ref[...],
                              *_state_access(c_in, n_in, m_in, c_out, n_out, m_out))


def mlstm_step(mq, mk, mv, gates, state_C, state_n, state_m, n_seq, rows):
    f32 = jnp.float32
    tok = lambda w: pl.BlockSpec((None, rows, w), lambda b: (b, 0, 0))
    st_c = pl.BlockSpec((None, ML_HEADS, ML_DH, ML_DH), lambda b: (b, 0, 0, 0))
    st_n = pl.BlockSpec((None, ML_HEADS, ML_DH), lambda b: (b, 0, 0))
    st_m = pl.BlockSpec((None, ML_HEADS, 128), lambda b: (b, 0, 0))
    r3 = lambda t: t.reshape(n_seq, rows, t.shape[-1])
    m_in = jnp.broadcast_to(state_m[:, :, None], (n_seq, ML_HEADS, 128))
    h, C, n, m = pl.pallas_call(
        _mlstm_step_kernel,
        grid=(n_seq,),
        in_specs=[tok(ML_WIDTH), tok(ML_WIDTH), tok(ML_WIDTH), tok(128), st_c, st_n, st_m],
        out_specs=[tok(ML_WIDTH), st_c, st_n, st_m],
        out_shape=[jax.ShapeDtypeStruct((n_seq, rows, ML_WIDTH), f32),
                   jax.ShapeDtypeStruct((n_seq, ML_HEADS, ML_DH, ML_DH), f32),
                   jax.ShapeDtypeStruct((n_seq, ML_HEADS, ML_DH), f32),
                   jax.ShapeDtypeStruct((n_seq, ML_HEADS, 128), f32)],
        compiler_params=pltpu.CompilerParams(dimension_semantics=("arbitrary",), vmem_limit_bytes=VMEM_LIMIT),
        name="mlstm_step",
    )(r3(mq), r3(mk), r3(mv), r3(gates), state_C, state_n, m_in)
    return h.reshape(n_seq * rows, ML_WIDTH), C, n, m[:, :, 0]


POST_TM = 256
MEM_SEQS = 4


def _mix_out_kernel(n_first, xn_ref, mla_a, mla_b, mlh_a, mlh_b, og_ref, wout_ref, g_ref, b_ref, wmq_ref, x1_ref, qm_ref):
    mixed = jnp.concatenate([_pick(n_first, mla_a, mla_b), og_ref[...] * _pick(n_first, mlh_a, mlh_b)],
                            axis=1).astype(jnp.bfloat16)
    mix = jnp.dot(mixed, wout_ref[...], preferred_element_type=jnp.float32)
    x1 = _layer_norm_rows(ALPHA * xn_ref[...] + mix, g_ref[...], b_ref[...])
    x1_ref[...] = x1
    qm_ref[...] = jnp.dot(x1.astype(jnp.bfloat16), wmq_ref[...],
                          preferred_element_type=jnp.float32).astype(jnp.bfloat16)


def mix_out(xn, mla_a, mla_b, mlh_a, mlh_b, o_gate, w_out, ln_g, ln_b, w_mq):
    T = xn.shape[0]
    n_first = mla_a.shape[0] // POST_TM
    assert mla_a.shape[0] % POST_TM == 0
    bf = jnp.bfloat16
    row = lambda n: pl.BlockSpec((POST_TM, n), lambda i: (i, 0))
    full = lambda shape: pl.BlockSpec(shape, lambda i: (0,) * len(shape))
    return pl.pallas_call(
        functools.partial(_mix_out_kernel, n_first),
        grid=(T // POST_TM,),
        in_specs=[row(D_MODEL), *_split_specs(POST_TM, MLA_WIDTH, n_first), *_split_specs(POST_TM, ML_WIDTH, n_first),
                  row(ML_WIDTH), full((D_MIX, D_MODEL)),
                  full((1, D_MODEL)), full((1, D_MODEL)), full((D_MODEL, D_MODEL))],
        out_specs=[row(D_MODEL), row(D_MODEL)],
        out_shape=[jax.ShapeDtypeStruct((T, D_MODEL), jnp.float32), jax.ShapeDtypeStruct((T, D_MODEL), bf)],
        compiler_params=pltpu.CompilerParams(dimension_semantics=("arbitrary",), vmem_limit_bytes=VMEM_LIMIT),
        name="mix_out",
    )(xn, mla_a, mla_b, mlh_a, mlh_b, o_gate, w_out.astype(bf), ln_g.reshape(1, -1), ln_b.reshape(1, -1),
      w_mq.reshape(D_MODEL, D_MODEL).astype(bf))


def _mem_attn_kernel(x1_ref, qm_ref, mk_ref, mv_ref, wmo_ref, g_ref, b_ref, x2_ref):
    q = qm_ref[...]
    nt = (((1,), (1,)), ((), ()))
    outs = []
    for h in range(MEM_HEADS):
        c = slice(h * MEM_HD, (h + 1) * MEM_HD)
        mk, mv = mk_ref[:, c].astype(jnp.bfloat16), mv_ref[:, c].astype(jnp.bfloat16)
        s = lax.dot_general(q[:, c], mk, nt, preferred_element_type=jnp.float32) * (MEM_HD ** -0.5)
        m = jnp.max(s, axis=-1, keepdims=True)
        p = jnp.exp(s - m)
        p = p / jnp.sum(p, axis=-1, keepdims=True)
        outs.append(jnp.dot(p.astype(jnp.bfloat16), mv, preferred_element_type=jnp.float32))
    o = jnp.concatenate(outs, axis=1).astype(jnp.bfloat16)
    att = jnp.dot(o, wmo_ref[...], preferred_element_type=jnp.float32)
    x2_ref[...] = _layer_norm_rows(ALPHA * x1_ref[...] + att, g_ref[...], b_ref[...])


def _mem_attn_sample_kernel(x1_ref, qm_ref, mk_ref, mv_ref, wmo_ref, g_ref, b_ref, x2_ref, o_scr):
    s_id = pl.program_id(0)
    n_here, tq = qm_ref.shape[:2]
    bf, f32 = jnp.bfloat16, jnp.float32
    for i in range(n_here):
        q = qm_ref[i]
        qs = jnp.concatenate([q[:, h * MEM_HD:(h + 1) * MEM_HD] for h in range(MEM_HEADS)], axis=0)
        kall = mk_ref[i].reshape(N_MEM * MEM_HEADS, MEM_HD).astype(bf)
        vall = mv_ref[i].reshape(N_MEM * MEM_HEADS, MEM_HD).astype(bf)
        s = lax.dot_general(qs, kall, (((1,), (1,)), ((), ())), preferred_element_type=f32) * (MEM_HD ** -0.5)
        row_h = lax.broadcasted_iota(jnp.int32, s.shape, 0) // tq
        col_h = lax.broadcasted_iota(jnp.int32, s.shape, 1) % MEM_HEADS
        s = jnp.where(row_h == col_h, s, NEG_INF)
        p = jnp.exp(s - jnp.max(s, axis=-1, keepdims=True))
        p = p / jnp.sum(p, axis=-1, keepdims=True)
        o = jnp.dot(p.astype(bf), vall, preferred_element_type=f32)
        o_scr[s_id * n_here + i] = jnp.concatenate([o[h * tq:(h + 1) * tq] for h in range(MEM_HEADS)], axis=1)

    @pl.when(s_id == pl.num_programs(0) - 1)
    def _():
        o_all = o_scr[...].reshape(x1_ref.shape).astype(bf)
        att = jnp.dot(o_all, wmo_ref[...], preferred_element_type=f32)
        x2_ref[...] = _layer_norm_rows(ALPHA * x1_ref[...] + att, g_ref[...], b_ref[...])


def mem_attend_ln(x1, qm, mem_k, mem_v, w_mo, ln_g, ln_b, n_seq, rows_per_seq):
    bf = jnp.bfloat16
    wmo = w_mo.reshape(D_MODEL, D_MODEL).astype(bf)
    g, b = ln_g.reshape(1, -1), ln_b.reshape(1, -1)
    cp = pltpu.CompilerParams(dimension_semantics=("arbitrary",) * 2, vmem_limit_bytes=VMEM_LIMIT)
    mem = pl.BlockSpec((None, N_MEM, D_MODEL), lambda s, i: (s, 0, 0))
    full = lambda shape: pl.BlockSpec(shape, lambda s, i: (0,) * len(shape))
    if rows_per_seq % POST_TM == 0:
        nb = rows_per_seq // POST_TM
        tok = pl.BlockSpec((POST_TM, D_MODEL), lambda s, i: (s * nb + i, 0))
        return pl.pallas_call(
            _mem_attn_kernel, grid=(n_seq, nb),
            in_specs=[tok, tok, mem, mem, full((D_MODEL, D_MODEL)), full((1, D_MODEL)), full((1, D_MODEL))],
            out_specs=tok, out_shape=jax.ShapeDtypeStruct((n_seq * rows_per_seq, D_MODEL), jnp.float32),
            compiler_params=cp, name="mem_attn_prompt",
        )(x1, qm, mem_k, mem_v, wmo, g, b)
    rows = n_seq * rows_per_seq
    assert n_seq % MEM_SEQS == 0
    mem = pl.BlockSpec((MEM_SEQS, N_MEM, MEM_HEADS, MEM_HD), lambda s: (s, 0, 0, 0))
    res = pl.BlockSpec((rows, D_MODEL), lambda s: (0, 0))
    full = lambda shape: pl.BlockSpec(shape, lambda s: (0,) * len(shape))
    return pl.pallas_call(
        _mem_attn_sample_kernel, grid=(n_seq // MEM_SEQS,),
        in_specs=[res, pl.BlockSpec((MEM_SEQS, rows_per_seq, D_MODEL), lambda s: (s, 0, 0)), mem, mem,
                  full((D_MODEL, D_MODEL)), full((1, D_MODEL)), full((1, D_MODEL))],
        out_specs=res, out_shape=jax.ShapeDtypeStruct((rows, D_MODEL), jnp.float32),
        scratch_shapes=[pltpu.VMEM((n_seq, rows_per_seq, D_MODEL), jnp.float32)],
        compiler_params=pltpu.CompilerParams(dimension_semantics=("arbitrary",), vmem_limit_bytes=VMEM_LIMIT),
        name="mem_attn_sample",
    )(x1, qm.reshape(n_seq, rows_per_seq, D_MODEL), mem_k, mem_v, wmo, g, b)


def _mem_kv_kernel(m_ref, w_ref, o_ref):
    o_ref[...] = jnp.dot(m_ref[...].astype(jnp.bfloat16), w_ref[...], preferred_element_type=jnp.float32)


def mem_kv(mem, w_mk, w_mv):
    B = mem.shape[0]
    w = jnp.concatenate([w_mk.reshape(D_MODEL, D_MODEL), w_mv.reshape(D_MODEL, D_MODEL)], axis=1).astype(jnp.bfloat16)
    out = pl.pallas_call(
        _mem_kv_kernel, grid=(B,),
        in_specs=[pl.BlockSpec((N_MEM, D_MODEL), lambda i: (i, 0)), pl.BlockSpec((D_MODEL, 2 * D_MODEL), lambda i: (0, 0))],
        out_specs=pl.BlockSpec((N_MEM, 2 * D_MODEL), lambda i: (i, 0)),
        out_shape=jax.ShapeDtypeStruct((B * N_MEM, 2 * D_MODEL), jnp.float32),
        compiler_params=pltpu.CompilerParams(dimension_semantics=("arbitrary",), vmem_limit_bytes=VMEM_LIMIT),
        name="mem_kv",
    )(mem.reshape(B * N_MEM, D_MODEL), w)
    mk = out[:, :D_MODEL].reshape(B, N_MEM, MEM_HEADS, MEM_HD)
    mv = out[:, D_MODEL:].reshape(B, N_MEM, MEM_HEADS, MEM_HD)
    return mk, mv


PEER_RT = 256
PEER_TB = 512
PEER_EB = 1024
PEER_VMEM_LIMIT = 58 * 1024 * 1024


def _top16_rows(s, row_id, exact_ties):
    big = float(2 ** 20)
    out_id = lax.broadcasted_iota(jnp.int32, (PEER_TOPK, s.shape[1]), 0)
    stacked = jnp.zeros((PEER_TOPK, s.shape[1]), jnp.float32)
    rank = jnp.full(s.shape, float(PEER_TOPK), jnp.float32)
    rows = []
    for k in range(PEER_TOPK):
        m = jnp.max(s, axis=0, keepdims=True)
        hit = s == m
        if exact_ties:
            hit = row_id == jnp.min(jnp.where(hit, row_id, big), axis=0, keepdims=True)
        s = jnp.where(hit, NEG_INF, s)
        rank = jnp.where(hit, float(k), rank)
        rows.append(m)
        stacked = jnp.where(out_id == k, m, stacked)
    return rows, stacked, rank, s


def _peer_route_head(s1, s2, exact_ties):
    tb = s1.shape[1]
    row128 = lax.broadcasted_iota(jnp.int32, (PEER_NKEYS, tb), 0).astype(jnp.float32)
    sub8 = lax.broadcasted_iota(jnp.int32, (8, tb), 0)
    sub8f = sub8.astype(jnp.float32)
    r1, v1, rank1, left1 = _top16_rows(s1, row128, exact_ties)
    r2, v2, rank2, left2 = _top16_rows(s2, row128, exact_ties)
    groups, ids = [], []
    for b in range(8):
        lim = PEER_TOPK // (b + 1)
        for a0 in range(0, lim, 8):
            g = v1[a0:a0 + 8] + r2[b]
            if lim - a0 < 8:
                g = jnp.where(sub8 < lim - a0, g, NEG_INF)
            groups.append(g)
            ids.append((sub8f + float(a0)) * float(PEER_TOPK) + float(b))
    groups.append(r1[0] + v2[8:16])
    ids.append(sub8f + 8.0)
    cand = jnp.concatenate(groups, axis=0)
    vals, _, _, left = _top16_rows(cand, jnp.concatenate(ids, axis=0), exact_ties)
    z = jnp.ones_like(vals[0])
    for k in range(1, PEER_TOPK):
        z = z + jnp.exp(vals[k] - vals[0])
    taken = jnp.where((left == NEG_INF) & (cand > NEG_INF), 1.0, 0.0)
    cnt_lo = jnp.zeros((8, tb), jnp.float32)
    gi = 0
    for b in range(8):
        for a0 in range(0, PEER_TOPK // (b + 1), 8):
            if a0 == 0:
                cnt_lo = cnt_lo + taken[gi * 8:(gi + 1) * 8]
            else:
                cnt_hi = taken[gi * 8:(gi + 1) * 8]
            gi += 1
    tail = jnp.sum(taken[gi * 8:(gi + 1) * 8], axis=0, keepdims=True)
    cnt_lo = cnt_lo + jnp.where(sub8 == 0, tail, 0.0)
    lim_full = jnp.full((PEER_NKEYS, tb), -1.0, jnp.float32)
    for a in range(PEER_TOPK):
        cnt = cnt_lo if a < 8 else cnt_hi
        lim_full = jnp.where(rank1 == float(a), cnt[a % 8:a % 8 + 1] - 1.0, lim_full)
    n_taken = (jnp.sum(jnp.where(left1 == NEG_INF, 1.0, 0.0), axis=0, keepdims=True)
               + jnp.sum(jnp.where(left2 == NEG_INF, 1.0, 0.0), axis=0, keepdims=True)
               + jnp.sum(taken, axis=0, keepdims=True))
    return (rank2.astype(jnp.bfloat16), lim_full, jnp.exp(s1 - r1[0]) / z,
            jnp.exp(s2 - r2[0]).astype(jnp.bfloat16), n_taken)


def _peer_route_kernel(n_first, xa_ref, xb_ref, wq_ref, k1_ref, k2_ref, rk2_ref, lim_ref, p1_ref, p2_ref, qt_scr):
    half = PEER_DKEY // 2
    xb = _pick(n_first, xa_ref, xb_ref).astype(jnp.bfloat16)
    qt_scr[...] = lax.dot_general(wq_ref[...], xb, (((1,), (1,)), ((), ())), preferred_element_type=jnp.float32)

    def head(h, carry):
        r0 = pl.multiple_of(h * PEER_DKEY, PEER_DKEY)
        q1 = qt_scr[pl.ds(r0, half), :].astype(jnp.bfloat16)
        q2 = qt_scr[pl.ds(r0 + half, half), :].astype(jnp.bfloat16)
        s1 = jnp.dot(k1_ref[...], q1, preferred_element_type=jnp.float32)
        s2 = jnp.dot(k2_ref[...], q2, preferred_element_type=jnp.float32)

        def emit(exact_ties):
            rk2, lim, p1, p2, n_taken = _peer_route_head(s1, s2, exact_ties)
            rk2_ref[h] = rk2
            lim_ref[h] = lim
            p1_ref[h] = p1
            p2_ref[h] = p2
            return n_taken

        n_taken = emit(False)
        merged = jnp.max(jnp.abs(n_taken - 3.0 * PEER_TOPK)) > 0.0

        @pl.when(merged)
        def _():
            emit(True)

        return carry

    lax.fori_loop(0, PEER_HEADS, head, 0)


def _row_bf16(row):
    r16 = jnp.broadcast_to(row, (16, row.shape[1])).astype(jnp.bfloat16)
    return jnp.concatenate([r16] * (PEER_NKEYS // 16), axis=0)


def _peer_weights(rk2_ref, lim_ref, p1_ref, p2_ref, row0, r, tb):
    w = jnp.zeros((PEER_NKEYS, tb), jnp.bfloat16)
    for h in range(PEER_HEADS):
        p2h = p2_ref[h]
        lim8 = lim_ref[h, pl.ds(row0, 8), :]
        p18 = p1_ref[h, pl.ds(row0, 8), :]
        sel = rk2_ref[h] <= _row_bf16(lim8[r:r + 1, :])
        w = w + jnp.where(sel, p2h, jnp.zeros_like(p2h)) * _row_bf16(p18[r:r + 1, :])
    return w


def _peer_dense_kernel(n_first, xa_ref, xb_ref, rk2_ref, lim_ref, p1_ref, p2_ref, u0_ref, ua_ref, ub_ref, vt_ref, g_ref, b_ref,
                       oa_ref, ob_ref, xb_scr, yt_scr, ht0_scr, ht1_scr):
    k = pl.program_id(1)
    n_i1 = PEER_EB // PEER_NKEYS
    tb = xa_ref.shape[0]
    nt = (((1,), (1,)), ((), ()))
    f32 = jnp.float32

    @pl.when(k == 0)
    def _():
        xb_scr[...] = _pick(n_first, xa_ref, xb_ref).astype(jnp.bfloat16)
        yt_scr[...] = jnp.zeros_like(yt_scr)
        ht0_scr[...] = lax.dot_general(u0_ref[...], xb_scr[...], nt, preferred_element_type=f32)

    xb = xb_scr[...]
    n_sub = 2
    sub = PEER_EB // n_sub

    def second_half(ht_scr, blk, vt_off, acc):
        for j in range(n_sub):
            pieces = []
            for cc in range(sub // PEER_NKEYS):
                w = _peer_weights(rk2_ref, lim_ref, p1_ref, p2_ref, pl.multiple_of(blk * n_i1, 8),
                                  j * (sub // PEER_NKEYS) + cc, tb)
                r0 = j * sub + cc * PEER_NKEYS
                hc = ht_scr[r0:r0 + PEER_NKEYS, :]
                gelu = 0.5 * hc * (1.0 + lax.erf(hc * (2.0 ** -0.5)))
                pieces.append(w * gelu.astype(jnp.bfloat16))
            at = jnp.concatenate(pieces, axis=0)
            c0 = vt_off + j * sub
            acc = acc + jnp.dot(vt_ref[:, c0:c0 + sub], at, preferred_element_type=f32)
        return acc

    acc = yt_scr[...]
    ht1_scr[...] = lax.dot_general(ua_ref[...], xb, nt, preferred_element_type=f32)
    acc = second_half(ht0_scr, 2 * k, 0, acc)
    ht0_scr[...] = lax.dot_general(ub_ref[...], xb, nt, preferred_element_type=f32)
    acc = second_half(ht1_scr, 2 * k + 1, PEER_EB, acc)
    yt_scr[...] = acc

    @pl.when(k == pl.num_programs(1) - 1)
    def _():
        z = ALPHA * _pick(n_first, xa_ref, xb_ref) + yt_scr[...].T
        res = _layer_norm_rows(z, g_ref[...], b_ref[...])
        first = pl.program_id(0) < n_first

        @pl.when(first)
        def _():
            oa_ref[...] = res

        @pl.when(jnp.logical_not(first))
        def _():
            ob_ref[...] = res


def peer_ln(x_a, x_b, w_pq, sub_k1, sub_k2, peer_u, peer_v, ln_g, ln_b):
    T = x_a.shape[0] + x_b.shape[0]
    assert x_a.shape[0] % PEER_TB == 0 and x_b.shape[0] % PEER_TB == 0 and PEER_TB % PEER_RT == 0
    nt = T // PEER_TB
    half = PEER_DKEY // 2
    wq_t = w_pq.reshape(D_MODEL, PEER_HEADS * PEER_DKEY).T.astype(jnp.bfloat16)
    sshape = jax.ShapeDtypeStruct((PEER_HEADS, PEER_NKEYS, T), jnp.float32)
    sspec = pl.BlockSpec((PEER_HEADS, PEER_NKEYS, PEER_RT), lambda j: (0, 0, j))
    rk2, lim, p1, p2 = pl.pallas_call(
        functools.partial(_peer_route_kernel, x_a.shape[0] // PEER_RT),
        grid=(T // PEER_RT,),
        in_specs=[*_split_specs(PEER_RT, D_MODEL, x_a.shape[0] // PEER_RT),
                  pl.BlockSpec((PEER_HEADS * PEER_DKEY, D_MODEL), lambda j: (0, 0)),
                  pl.BlockSpec((PEER_NKEYS, half), lambda j: (0, 0)),
                  pl.BlockSpec((PEER_NKEYS, half), lambda j: (0, 0))],
        out_specs=[sspec, sspec, sspec, sspec],
        out_shape=[jax.ShapeDtypeStruct(sshape.shape, jnp.bfloat16), sshape, sshape,
                   jax.ShapeDtypeStruct(sshape.shape, jnp.bfloat16)],
        scratch_shapes=[pltpu.VMEM((PEER_HEADS * PEER_DKEY, PEER_RT), jnp.float32)],
        compiler_params=pltpu.CompilerParams(dimension_semantics=("arbitrary",), vmem_limit_bytes=VMEM_LIMIT),
        name="peer_route",
    )(x_a, x_b, wq_t, sub_k1.astype(jnp.bfloat16), sub_k2.astype(jnp.bfloat16))

    u_b = peer_u.astype(jnp.bfloat16)
    vt_b = peer_v.T.astype(jnp.bfloat16)
    ne = PEER_N // PEER_EB
    assert ne % 2 == 0
    sspec2 = pl.BlockSpec((PEER_HEADS, PEER_NKEYS, PEER_TB), lambda j, k: (0, 0, j))
    return pl.pallas_call(
        functools.partial(_peer_dense_kernel, x_a.shape[0] // PEER_TB),
        grid=(nt, ne // 2),
        in_specs=[*_split_specs(PEER_TB, D_MODEL, x_a.shape[0] // PEER_TB),
                  sspec2, sspec2, sspec2, sspec2,
                  pl.BlockSpec((PEER_EB, D_MODEL), lambda j, k: (0, 0)),
                  pl.BlockSpec((PEER_EB, D_MODEL), lambda j, k: (2 * k + 1, 0)),
                  pl.BlockSpec((PEER_EB, D_MODEL), lambda j, k: (jnp.minimum(2 * k + 2, ne - 1), 0)),
                  pl.BlockSpec((D_MODEL, 2 * PEER_EB), lambda j, k: (0, k)),
                  pl.BlockSpec((1, D_MODEL), lambda j, k: (0, 0)),
                  pl.BlockSpec((1, D_MODEL), lambda j, k: (0, 0))],
        out_specs=list(_split_specs(PEER_TB, D_MODEL, x_a.shape[0] // PEER_TB)),
        out_shape=[jax.ShapeDtypeStruct(x_a.shape, jnp.float32), jax.ShapeDtypeStruct(x_b.shape, jnp.float32)],
        scratch_shapes=[pltpu.VMEM((PEER_TB, D_MODEL), jnp.bfloat16),
                        pltpu.VMEM((D_MODEL, PEER_TB), jnp.float32),
                        pltpu.VMEM((PEER_EB, PEER_TB), jnp.float32),
                        pltpu.VMEM((PEER_EB, PEER_TB), jnp.float32)],
        compiler_params=pltpu.CompilerParams(dimension_semantics=("arbitrary", "arbitrary"),
                                             vmem_limit_bytes=PEER_VMEM_LIMIT),
        name="peer_dense",
    )(x_a, x_b, rk2, lim, p1, p2, u_b, u_b, u_b, vt_b, ln_g.reshape(1, -1), ln_b.reshape(1, -1))


def kernel(x_prompt, x_sample, cache_kv_latent, cache_k_rope, state_C, state_n, state_m,
           cache_mem_k, cache_mem_v, page_table, mem_prompt, ln0_g, ln0_b, w_in, b_i, b_f,
           g_q, w_uq, g_kv, w_uk, w_uv, w_out, ln1_g, ln1_b, w_mq, w_mk, w_mv, w_mo,
           ln2_g, ln2_b, w_pq, sub_k1, sub_k2, peer_u, peer_v, ln3_g, ln3_b):
    B, S = x_prompt.shape[:2]
    NB, TQ = x_sample.shape[:2]
    past = page_table.shape[1] * PAGE_SIZE
    n_p = B * S
    l = 0
    pos = jnp.concatenate([jnp.tile(jnp.arange(S), B), jnp.tile(past + jnp.arange(TQ), NB)])
    (xn, q_lat, q_rope, kv, kvk, kr, mq, mk, mv, gates, o_gate) = mix_in(
        x_prompt.reshape(n_p, D_MODEL), x_sample.reshape(NB * TQ, D_MODEL), pos, ln0_g, ln0_b, w_in[l], b_i[l], b_f[l], g_q[l], w_uq[l], g_kv[l], w_uk[l])

    mla_p = mla_attend_prompt(q_lat, q_rope, kvk, w_uv[l], B, S)
    mlh_p, C_p, n_pst, m_p = mlstm_prompt(mq, mk, mv, gates, B, S)

    kv_s = kv[n_p:].reshape(NB, TQ, MLA_KV_LORA)
    kr_s = kr[n_p:, :MLA_ROPE].reshape(NB, TQ, MLA_ROPE)
    mla_s = mla_attend_sample(q_lat[n_p:].reshape(NB, TQ, -1), q_rope[n_p:].reshape(NB, TQ, -1), kv_s, kr_s,
                              cache_kv_latent.reshape(cache_kv_latent.shape[1:]),
                              jnp.swapaxes(cache_k_rope.reshape(cache_k_rope.shape[1:]), 1, 2), page_table, w_uv[l])
    mlh_s, C_s, n_s, m_s = mlstm_step(mq[n_p:], mk[n_p:], mv[n_p:], gates[n_p:], state_C.reshape(state_C.shape[1:]),
                                      state_n.reshape(state_n.shape[1:]), state_m.reshape(state_m.shape[1:]), NB, TQ)

    x1, qm = mix_out(xn, mla_p, mla_s, mlh_p, mlh_s, o_gate, w_out[l], ln1_g[l], ln1_b[l], w_mq[l])
    mk_p, mv_p = mem_kv(mem_prompt, w_mk[l], w_mv[l])
    x2_p = mem_attend_ln(x1, qm, mk_p.reshape(B, N_MEM, D_MODEL), mv_p.reshape(B, N_MEM, D_MODEL),
                         w_mo[l], ln2_g[l], ln2_b[l], B, S)
    x2_s = mem_attend_ln(x1[n_p:], qm[n_p:], cache_mem_k.reshape(cache_mem_k.shape[1:]),
                         cache_mem_v.reshape(cache_mem_v.shape[1:]), w_mo[l], ln2_g[l], ln2_b[l], NB, TQ)
    x3_p, x3_s = peer_ln(x2_p, x2_s, w_pq[l], sub_k1[l], sub_k2[l], peer_u[l], peer_v[l], ln3_g[l], ln3_b[l])
    st = lambda t: t[None]
    return (x3_p.reshape(B, S, D_MODEL), x3_s.reshape(NB, TQ, D_MODEL),
            st(kv[:n_p].reshape(B, S, MLA_KV_LORA)), st(kr[:n_p, :MLA_ROPE].reshape(B, S, MLA_ROPE)),
            st(C_p), st(n_pst), st(m_p), st(mk_p), st(mv_p),
            st(kv_s), st(kr_s), st(C_s), st(n_s), st(m_s))
```

```python
import functools

import jax, jax.numpy as jnp
from jax import lax
import numpy as np
from jax.experimental import pallas as pl
from jax.experimental.pallas import tpu as pltpu

D_MODEL = 1024
PAGE_SIZE = 128

MLA_HEADS = 8
MLA_NOPE = 64
MLA_ROPE = 32
MLA_V = 64
MLA_KV_LORA = 256
MLA_Q_LORA = 384
MLA_SCALE = (MLA_NOPE + MLA_ROPE) ** -0.5
ROPE_BASE = 10000.0
ML_HEADS = 4
ML_DH = 128
ML_CHUNK = 256
MLA_WIDTH = MLA_HEADS * MLA_V
ML_WIDTH = ML_HEADS * ML_DH
D_MIX = MLA_WIDTH + ML_WIDTH
N_MEM = 256
MEM_HEADS = 4
MEM_HD = D_MODEL // MEM_HEADS
PEER_HEADS = 8
PEER_NKEYS = 128
PEER_N = PEER_NKEYS * PEER_NKEYS
PEER_DKEY = 128
PEER_TOPK = 16
LN_EPS = 1e-5
RMS_EPS = 1e-6
DEPTH = 1
ALPHA = (2 * DEPTH) ** 0.25
NEG_INF = float('-inf')

VMEM_LIMIT = 48 * 1024 * 1024

IN_PAD = 2944
OFF_CQ, OFF_CKV, OFF_KR, OFF_MQ, OFF_MK, OFF_MV, OFF_G, OFF_O = 0, 384, 640, 768, 1280, 1792, 2304, 2432
MIX_TM = 256


def _split_specs(tm, width, n_first):
    return (pl.BlockSpec((tm, width), lambda i, *_: (jnp.minimum(i, n_first - 1), 0)),
            pl.BlockSpec((tm, width), lambda i, *_: (jnp.maximum(i - n_first, 0), 0)))


def _pick(n_first, a_ref, b_ref):
    return jnp.where(pl.program_id(0) < n_first, a_ref[...], b_ref[...])


def _layer_norm_rows(z, g, b):
    mu = jnp.mean(z, axis=-1, keepdims=True)
    zc = z - mu
    var = jnp.mean(zc * zc, axis=-1, keepdims=True)
    return zc * lax.rsqrt(var + LN_EPS) * g + b


def _rope_lanes(x, cos, sin_signed):
    n = x.shape[1]
    lane = lax.broadcasted_iota(jnp.int32, x.shape, 1)
    partner = jnp.where((lane & 31) < 16, pltpu.roll(x, n - 16, axis=1), pltpu.roll(x, 16, axis=1))
    return x * cos + partner * sin_signed


def _mix_in_kernel(n_first, xa_ref, xb_ref, g0_ref, b0_ref, win_ref, gq_ref, wuq_ref, wuk_ref, gkv_ref, cos_ref, sin_ref, gb_ref,
                   xn_ref, ql_ref, qr_ref, kv_ref, kvk_ref, kr_ref, mq_ref, mk_ref, mv_ref, gate_ref, og_ref):
    xn = _layer_norm_rows(_pick(n_first, xa_ref, xb_ref), g0_ref[...], b0_ref[...])
    xn_ref[...] = xn
    z = jnp.dot(xn.astype(jnp.bfloat16), win_ref[...], preferred_element_type=jnp.float32)
    cos = cos_ref[...]
    sin = sin_ref[...]
    cq = z[:, OFF_CQ:OFF_CQ + MLA_Q_LORA]
    cq = cq * lax.rsqrt(jnp.mean(cq * cq, axis=-1, keepdims=True) + RMS_EPS) * gq_ref[...]
    q = jnp.dot(cq.astype(jnp.bfloat16), wuq_ref[...], preferred_element_type=jnp.float32)
    n_nope = MLA_HEADS * MLA_NOPE
    qrope = _rope_lanes(q[:, n_nope:], jnp.concatenate([cos, cos], axis=1), jnp.concatenate([sin, sin], axis=1))
    qr_ref[...] = qrope.astype(jnp.bfloat16)
    ql_ref[...] = jnp.dot(q[:, :n_nope].astype(jnp.bfloat16), wuk_ref[...],
                          preferred_element_type=jnp.float32).astype(jnp.bfloat16)
    ckv = z[:, OFF_CKV:OFF_CKV + MLA_KV_LORA]
    kv = ckv * lax.rsqrt(jnp.mean(ckv * ckv, axis=-1, keepdims=True) + RMS_EPS) * gkv_ref[...]
    kv_ref[...] = kv
    kr = _rope_lanes(z[:, OFF_KR:OFF_KR + 128], cos, sin)
    kr_ref[...] = kr
    krt = kr + pltpu.roll(kr, 32, axis=1) + pltpu.roll(kr, 64, axis=1) + pltpu.roll(kr, 96, axis=1)
    kvk_ref[...] = jnp.concatenate([kv, krt, krt], axis=1).astype(jnp.bfloat16)
    mq_ref[...] = z[:, OFF_MQ:OFF_MQ + ML_WIDTH]
    mk_ref[...] = z[:, OFF_MK:OFF_MK + ML_WIDTH] * (ML_DH ** -0.5)
    mv_ref[...] = z[:, OFF_MV:OFF_MV + ML_WIDTH]
    g = z[:, OFF_G:OFF_G + 128] + gb_ref[...]
    lane = lax.broadcasted_iota(jnp.int32, g.shape, 1)
    gate_ref[...] = jnp.where(lane < ML_HEADS, g, jax.nn.log_sigmoid(g))
    og_ref[...] = jax.nn.sigmoid(z[:, OFF_O:OFF_O + ML_WIDTH])


def mix_in(x_a, x_b, pos, ln0_g, ln0_b, w_in, b_i, b_f, g_q, w_uq, g_kv, w_uk):
    T = x_a.shape[0] + x_b.shape[0]
    n_first = x_a.shape[0] // MIX_TM
    assert x_a.shape[0] % MIX_TM == 0 and x_b.shape[0] % MIX_TM == 0
    f32, bf = jnp.float32, jnp.bfloat16
    zc = lambda n: jnp.zeros((D_MODEL, n), f32)
    win_p = jnp.concatenate([w_in[:, :672], zc(96), w_in[:, 672:2208], w_in[:, 2208:2216], zc(120), w_in[:, 2216:]],
                            axis=1).astype(bf)
    assert win_p.shape[1] == IN_PAD
    wuq_p = jnp.concatenate([w_uq[:, :, :MLA_NOPE].reshape(MLA_Q_LORA, -1),
                             w_uq[:, :, MLA_NOPE:].reshape(MLA_Q_LORA, -1)], axis=1).astype(bf)
    hh = jnp.arange(MLA_HEADS)
    wuk_blk = jnp.zeros((MLA_HEADS, MLA_NOPE, MLA_HEADS, MLA_KV_LORA), f32)
    wuk_blk = wuk_blk.at[hh, :, hh, :].set(jnp.transpose(w_uk, (1, 2, 0)))
    wuk_blk = wuk_blk.reshape(MLA_HEADS * MLA_NOPE, MLA_HEADS * MLA_KV_LORA).astype(bf)
    inv = 1.0 / (ROPE_BASE ** (jnp.arange(0, MLA_ROPE, 2, dtype=f32) / MLA_ROPE))
    ang = pos.astype(f32)[:, None] * inv[None, :]
    c, s = jnp.cos(ang), jnp.sin(ang)
    cos128 = jnp.tile(jnp.concatenate([c, c], axis=1), (1, 4))
    sin128 = jnp.tile(jnp.concatenate([-s, s], axis=1), (1, 4))
    gbias = jnp.concatenate([b_i, b_f, jnp.zeros((120,), f32)]).reshape(1, 128)
    row = lambda n: pl.BlockSpec((MIX_TM, n), lambda i: (i, 0))
    full = lambda a: pl.BlockSpec(a.shape, lambda i: (0,) * a.ndim)
    ins = [x_a, x_b, ln0_g.reshape(1, -1), ln0_b.reshape(1, -1), win_p, g_q.reshape(1, -1), wuq_p, wuk_blk,
           g_kv.reshape(1, -1), cos128, sin128, gbias]
    in_specs = list(_split_specs(MIX_TM, D_MODEL, n_first)) + [full(a) for a in ins[2:9]] + [row(128), row(128), full(gbias)]
    outs = [(D_MODEL, f32), (MLA_HEADS * MLA_KV_LORA, bf), (MLA_HEADS * MLA_ROPE, bf), (MLA_KV_LORA, f32),
            (MLA_KV_LORA + MLA_HEADS * MLA_ROPE, bf), (128, f32), (ML_WIDTH, f32), (ML_WIDTH, f32),
            (ML_WIDTH, f32), (128, f32), (ML_WIDTH, f32)]
    return pl.pallas_call(
        functools.partial(_mix_in_kernel, n_first),
        grid=(T // MIX_TM,),
        in_specs=in_specs,
        out_specs=[row(n) for n, _ in outs],
        out_shape=[jax.ShapeDtypeStruct((T, n), dt) for n, dt in outs],
        compiler_params=pltpu.CompilerParams(dimension_semantics=("arbitrary",), vmem_limit_bytes=VMEM_LIMIT),
        name="mix_in",
    )(*ins)


ATT_BQ = 128
ATT_BK = 512


def _mla_prompt_kernel(ql_ref, qr_ref, kvk_ref, wuv_ref, o_ref, acc_scr, m_scr, l_scr):
    qi = pl.program_id(1)
    ql = jnp.concatenate([ql_ref[:, h * MLA_KV_LORA:(h + 1) * MLA_KV_LORA] for h in range(MLA_HEADS)], axis=0)
    qr_all = qr_ref[...]
    lane_head = lax.broadcasted_iota(jnp.int32, qr_all.shape, 1) // MLA_ROPE
    qr = jnp.concatenate([jnp.where(lane_head == h, qr_all, jnp.zeros_like(qr_all)) for h in range(MLA_HEADS)],
                         axis=0)
    q = jnp.concatenate([ql, qr], axis=1)
    acc_scr[...] = jnp.zeros_like(acc_scr)
    m_scr[...] = jnp.full_like(m_scr, NEG_INF)
    l_scr[...] = jnp.zeros_like(l_scr)
    nt = (((1,), (1,)), ((), ()))
    n_last = (qi * ATT_BQ) // ATT_BK
    rows = q.shape[0]
    tok = qi * ATT_BQ + (lax.broadcasted_iota(jnp.int32, (rows, ATT_BK), 0) & (ATT_BQ - 1))
    col = lax.broadcasted_iota(jnp.int32, (rows, ATT_BK), 1)

    def scores(kj):
        k0 = pl.multiple_of(kj * ATT_BK, ATT_BK)
        s = lax.dot_general(q, kvk_ref[pl.ds(k0, ATT_BK), :], nt, preferred_element_type=jnp.float32) * MLA_SCALE
        return jnp.where(col + k0 <= tok, s, NEG_INF)

    def accumulate(kj, s):
        k0 = pl.multiple_of(kj * ATT_BK, ATT_BK)
        kvb = kvk_ref[pl.ds(k0, ATT_BK), :MLA_KV_LORA]
        m_old = m_scr[...]
        m_new = jnp.maximum(m_old, jnp.max(s, axis=1, keepdims=True))
        alpha = jnp.exp(m_old - m_new)
        p = jnp.exp(s - m_new)
        l_scr[...] = alpha * l_scr[...] + jnp.sum(p, axis=1, keepdims=True)
        acc_scr[...] = alpha * acc_scr[...] + jnp.dot(p.astype(jnp.bfloat16), kvb,
                                                      preferred_element_type=jnp.float32)
        m_scr[...] = m_new

    def body(kj, s_cur):
        s_next = scores(kj + 1)
        accumulate(kj, s_cur)
        return s_next

    s_last = lax.fori_loop(0, n_last, body, scores(0))
    accumulate(n_last, s_last)
    o = (acc_scr[...] / l_scr[...]).astype(jnp.bfloat16)
    r = jnp.dot(o, wuv_ref[...], preferred_element_type=jnp.float32)
    col_head = lax.broadcasted_iota(jnp.int32, (ATT_BQ, MLA_WIDTH), 1) // MLA_V
    out = jnp.zeros((ATT_BQ, MLA_WIDTH), jnp.float32)
    for h in range(MLA_HEADS):
        out = out + jnp.where(col_head == h, r[h * ATT_BQ:(h + 1) * ATT_BQ], 0.0)
    o_ref[...] = out


def mla_attend_prompt(q_lat, q_rope, kvk, w_uv, n_seq, seq):
    assert seq % ATT_BK == 0 and ATT_BK % ATT_BQ == 0
    H, C = MLA_HEADS, MLA_KV_LORA
    nq = seq // ATT_BQ
    rows = H * ATT_BQ
    return pl.pallas_call(
        _mla_prompt_kernel,
        grid=(n_seq, nq),
        in_specs=[pl.BlockSpec((ATT_BQ, H * C), lambda b, i: (b * nq + i, 0)),
                  pl.BlockSpec((ATT_BQ, H * MLA_ROPE), lambda b, i: (b * nq + i, 0)),
                  pl.BlockSpec((seq, C + H * MLA_ROPE), lambda b, i: (b, 0)),
                  pl.BlockSpec((C, H * MLA_V), lambda b, i: (0, 0))],
        out_specs=pl.BlockSpec((ATT_BQ, H * MLA_V), lambda b, i: (b * nq + i, 0)),
        out_shape=jax.ShapeDtypeStruct((n_seq * seq, H * MLA_V), jnp.float32),
        scratch_shapes=[pltpu.VMEM((rows, C), jnp.float32),
                        pltpu.VMEM((rows, 1), jnp.float32),
                        pltpu.VMEM((rows, 1), jnp.float32)],
        compiler_params=pltpu.CompilerParams(dimension_semantics=("arbitrary", "arbitrary"),
                                             vmem_limit_bytes=VMEM_LIMIT),
        name="mla_prompt",
    )(q_lat, q_rope, kvk, w_uv.reshape(C, H * MLA_V).astype(jnp.bfloat16))


SMP_KC = 2048


def _mla_sample_kernel(pt_ref, ql_ref, qr_ref, kvn_ref, krn_ref, wuv_ref, lat_hbm, rope_hbm, o_ref,
                       lat_buf, rope_buf, lat_bf, sem_lat, sem_rope):
    b = pl.program_id(0)
    nb = pl.num_programs(0)
    n_pages = pt_ref.shape[1]
    rows = ql_ref.shape[0]
    tq = kvn_ref.shape[0]
    n_keys = n_pages * PAGE_SIZE

    def page_copies(seq, slot, p):
        page = pt_ref[seq, p]
        dst = pl.ds(p * PAGE_SIZE, PAGE_SIZE)
        return (pltpu.make_async_copy(lat_hbm.at[page], lat_buf.at[slot, dst], sem_lat.at[slot]),
                pltpu.make_async_copy(rope_hbm.at[page], rope_buf.at[slot, :, dst], sem_rope.at[slot]))

    def start_fetch(seq, slot):
        for p in range(n_pages):
            for cp in page_copies(seq, slot, p):
                cp.start(priority=p % 2)

    def wait_fetch(seq, slot):
        for p in range(n_pages):
            for cp in page_copies(seq, slot, p):
                cp.wait()

    slot = b % 2

    @pl.when(b == 0)
    def _():
        start_fetch(0, 0)

    @pl.when(b + 1 < nb)
    def _():
        start_fetch(b + 1, 1 - slot)

    wait_fetch(b, slot)

    bf, f32 = jnp.bfloat16, jnp.float32
    ql = ql_ref[...]
    qr = qr_ref[...]
    nt = (((1,), (1,)), ((), ()))
    parts = []
    for c in range(n_keys // SMP_KC):
        r = pl.ds(c * SMP_KC, SMP_KC)
        lb = lat_buf[slot, r, :].astype(bf)
        rb = rope_buf[slot, :, r].astype(bf)
        lat_bf[r, :] = lb
        parts.append(lax.dot_general(ql, lb, nt, preferred_element_type=f32)
                     + jnp.dot(qr, rb, preferred_element_type=f32))
    s_past = jnp.concatenate(parts, axis=1) * MLA_SCALE
    qlf, qrf = ql.astype(f32), qr.astype(f32)
    kvn = kvn_ref[...].astype(f32)
    krn = krn_ref[...].astype(f32)
    tok = lax.broadcasted_iota(jnp.int32, (rows, 1), 0) % tq
    s_new = []
    for j in range(tq):
        sj = (jnp.sum(qlf * kvn[j:j + 1, :], axis=1, keepdims=True)
              + jnp.sum(qrf * krn[j:j + 1, :], axis=1, keepdims=True)) * MLA_SCALE
        s_new.append(jnp.where(tok >= j, sj, NEG_INF))
    m = jnp.max(s_past, axis=1, keepdims=True)
    for sj in s_new:
        m = jnp.maximum(m, sj)
    p_past = jnp.exp(s_past - m)
    p_new = [jnp.exp(sj - m) for sj in s_new]
    l = jnp.sum(p_past, axis=1, keepdims=True)
    for pj in p_new:
        l = l + pj
    inv = 1.0 / l
    o = jnp.dot((p_past * inv).astype(bf), lat_bf[...], preferred_element_type=f32)
    for j in range(tq):
        o = o + (p_new[j] * inv).astype(bf).astype(f32) * kvn[j:j + 1, :]
    r = jnp.dot(o.astype(bf), wuv_ref[...], preferred_element_type=f32)
    col_head = lax.broadcasted_iota(jnp.int32, (tq, MLA_WIDTH), 1) // MLA_V
    out = jnp.zeros((tq, MLA_WIDTH), f32)
    for h in range(MLA_HEADS):
        out = out + jnp.where(col_head == h, r[h * tq:(h + 1) * tq, :], 0.0)
    o_ref[...] = out


def mla_attend_sample(q_lat, q_rope, kv_new, kr_new, pool_lat, pool_rope_t, page_table, w_uv):
    NB, T = q_lat.shape[:2]
    H, C, R = MLA_HEADS, MLA_KV_LORA, MLA_ROPE
    bf = jnp.bfloat16
    n_pages = page_table.shape[1]
    n_keys = n_pages * PAGE_SIZE
    assert n_keys % SMP_KC == 0
    ql = q_lat.reshape(NB, T, H, C).transpose(0, 2, 1, 3).reshape(NB, H * T, C)
    qr = q_rope.reshape(NB, T, H, R).transpose(0, 2, 1, 3).reshape(NB, H * T, R)
    seq = lambda n, w: pl.BlockSpec((None, n, w), lambda b, pt: (b, 0, 0))
    grid_spec = pltpu.PrefetchScalarGridSpec(
        num_scalar_prefetch=1,
        grid=(NB,),
        in_specs=[seq(H * T, C), seq(H * T, R), seq(T, C), seq(T, R),
                  pl.BlockSpec((C, H * MLA_V), lambda b, pt: (0, 0)),
                  pl.BlockSpec(memory_space=pl.ANY), pl.BlockSpec(memory_space=pl.ANY)],
        out_specs=seq(T, H * MLA_V),
        scratch_shapes=[pltpu.VMEM((2, n_keys, C), jnp.float32), pltpu.VMEM((2, R, n_keys), jnp.float32),
                        pltpu.VMEM((n_keys, C), bf),
                        pltpu.SemaphoreType.DMA((2,)), pltpu.SemaphoreType.DMA((2,))],
    )
    out = pl.pallas_call(
        _mla_sample_kernel,
        grid_spec=grid_spec,
        out_shape=jax.ShapeDtypeStruct((NB, T, H * MLA_V), jnp.float32),
        compiler_params=pltpu.CompilerParams(dimension_semantics=("arbitrary",), vmem_limit_bytes=VMEM_LIMIT),
        name="mla_sample",
    )(page_table, ql, qr, kv_new.astype(bf), kr_new.astype(bf), w_uv.reshape(C, H * MLA_V).astype(bf), pool_lat, pool_rope_t)
    return out.reshape(NB * T, H * MLA_V)


def _split3(x):
    hi = x.astype(jnp.bfloat16)
    r1 = x - hi.astype(jnp.float32)
    mid = r1.astype(jnp.bfloat16)
    lo = (r1 - mid.astype(jnp.float32)).astype(jnp.bfloat16)
    return hi, mid, lo


def _mlstm_chunk(q_all, k_all, v_all, gates, c_refs, n_refs, m_refs):
    L = q_all.shape[0]
    bf, f32 = jnp.bfloat16, jnp.float32
    row_t = lax.broadcasted_iota(jnp.int32, (L, L), 0)
    col_s = lax.broadcasted_iota(jnp.int32, (L, L), 1)
    causal = col_s <= row_t
    tril = jnp.where(causal, 1.0, 0.0).astype(bf)
    cum = sum(jnp.dot(tril, part, preferred_element_type=f32) for part in _split3(gates))
    gates_t = gates.T
    cum_t = cum.T
    nt = (((1,), (1,)), ((), ()))
    outs = []
    for h in range(ML_HEADS):
        c = slice(h * ML_DH, (h + 1) * ML_DH)
        q, k, v = q_all[:, c], k_all[:, c], v_all[:, c]
        C, n, m = c_refs[0](h), n_refs[0](h), m_refs[0](h)
        ig_col = gates[:, h:h + 1]
        b_col = cum[:, ML_HEADS + h:ML_HEADS + h + 1]
        ig_row = gates_t[h:h + 1, :]
        b_row = cum_t[ML_HEADS + h:ML_HEADS + h + 1, :]
        D = jnp.where(causal, b_col - b_row + ig_row, NEG_INF)
        inter = b_col + m
        m_t = jnp.maximum(inter, jnp.max(D, axis=1, keepdims=True))
        qb, kb, vb = q.astype(bf), k.astype(bf), v.astype(bf)
        A = jnp.exp(D - m_t) * lax.dot_general(qb, kb, nt, preferred_element_type=f32)
        w_inter = jnp.exp(inter - m_t)
        num = w_inter * jnp.dot(qb, C.astype(bf), preferred_element_type=f32) \
            + jnp.dot(A.astype(bf), vb, preferred_element_type=f32)
        qn = jnp.sum(qb.astype(f32) * n.astype(bf).astype(f32), axis=1, keepdims=True)
        den = w_inter * qn + jnp.sum(A, axis=1, keepdims=True)
        outs.append(num / jnp.maximum(jnp.abs(den), jnp.exp(-m_t)))
        b_end = b_col[L - 1:L, :]
        m_new = jnp.maximum(b_end + m, jnp.max(b_end - b_row + ig_row, axis=1, keepdims=True))
        a_prev = jnp.exp(b_end + m - m_new)
        kw = k * jnp.exp(b_end - b_col + ig_col - m_new)
        c_refs[1](h, a_prev * C + jnp.dot(kw.T.astype(bf), vb, preferred_element_type=f32))
        n_refs[1](h, a_prev * n + jnp.sum(kw, axis=0, keepdims=True))
        m_refs[1](h, m_new)
    return jnp.concatenate(outs, axis=1)


def _state_access(c_get, n_get, m_get, c_set, n_set, m_set):
    c_refs = (lambda h: c_get[h], lambda h, val: c_set.__setitem__(h, val))
    n_refs = (lambda h: n_get[h:h + 1, :], lambda h, val: n_set.__setitem__((slice(h, h + 1), slice(None)), val))
    m_refs = (lambda h: m_get[h:h + 1, 0:1],
              lambda h, val: m_set.__setitem__((slice(h, h + 1), slice(None)), jnp.broadcast_to(val, (1, 128))))
    return c_refs, n_refs, m_refs


def _mlstm_prompt_kernel(q_ref, k_ref, v_ref, g_ref, h_ref, c_out, n_out, m_out, c_scr, n_scr, m_scr):
    j = pl.program_id(1)

    @pl.when(j == 0)
    def _():
        c_scr[...] = jnp.zeros_like(c_scr)
        n_scr[...] = jnp.zeros_like(n_scr)
        m_scr[...] = jnp.zeros_like(m_scr)

    h_ref[...] = _mlstm_chunk(q_ref[...], k_ref[...], v_ref[...], g_ref[...],
                              *_state_access(c_scr, n_scr, m_scr, c_scr, n_scr, m_scr))

    @pl.when(j == pl.num_programs(1) - 1)
    def _():
        c_out[...] = c_scr[...]
        n_out[...] = n_scr[0:ML_HEADS, :]
        m_out[...] = m_scr[0:ML_HEADS, :]


def mlstm_prompt(mq, mk, mv, gates, n_seq, seq):
    nc = seq // ML_CHUNK
    f32 = jnp.float32
    tok = lambda w: pl.BlockSpec((ML_CHUNK, w), lambda b, j: (b * nc + j, 0))
    h, C, n, m = pl.pallas_call(
        _mlstm_prompt_kernel,
        grid=(n_seq, nc),
        in_specs=[tok(ML_WIDTH), tok(ML_WIDTH), tok(ML_WIDTH), tok(128)],
        out_specs=[tok(ML_WIDTH),
                   pl.BlockSpec((None, ML_HEADS, ML_DH, ML_DH), lambda b, j: (b, 0, 0, 0)),
                   pl.BlockSpec((None, ML_HEADS, ML_DH), lambda b, j: (b, 0, 0)),
                   pl.BlockSpec((None, ML_HEADS, 128), lambda b, j: (b, 0, 0))],
        out_shape=[jax.ShapeDtypeStruct((n_seq * seq, ML_WIDTH), f32),
                   jax.ShapeDtypeStruct((n_seq, ML_HEADS, ML_DH, ML_DH), f32),
                   jax.ShapeDtypeStruct((n_seq, ML_HEADS, ML_DH), f32),
                   jax.ShapeDtypeStruct((n_seq, ML_HEADS, 128), f32)],
        scratch_shapes=[pltpu.VMEM((ML_HEADS, ML_DH, ML_DH), f32), pltpu.VMEM((8, ML_DH), f32),
                        pltpu.VMEM((8, 128), f32)],
        compiler_params=pltpu.CompilerParams(dimension_semantics=("arbitrary", "arbitrary"),
                                             vmem_limit_bytes=VMEM_LIMIT),
        name="mlstm_prompt",
    )(mq, mk, mv, gates)
    return h, C, n, m[:, :, 0]


def _mlstm_step_kernel(q_ref, k_ref, v_ref, g_ref, c_in, n_in, m_in, h_ref, c_out, n_out, m_out):
    h_ref[...] = _mlstm_chunk(q_ref[...], k_ref[...], v_ref[...], g_ref[...],
                              *_state_access(c_in, n_in, m_in, c_out, n_out, m_out))


def mlstm_step(mq, mk, mv, gates, state_C, state_n, state_m, n_seq, rows):
    f32 = jnp.float32
    tok = lambda w: pl.BlockSpec((None, rows, w), lambda b: (b, 0, 0))
    st_c = pl.BlockSpec((None, ML_HEADS, ML_DH, ML_DH), lambda b: (b, 0, 0, 0))
    st_n = pl.BlockSpec((None, ML_HEADS, ML_DH), lambda b: (b, 0, 0))
    st_m = pl.BlockSpec((None, ML_HEADS, 128), lambda b: (b, 0, 0))
    r3 = lambda t: t.reshape(n_seq, rows, t.shape[-1])
    m_in = jnp.broadcast_to(state_m[:, :, None], (n_seq, ML_HEADS, 128))
    h, C, n, m = pl.pallas_call(
        _mlstm_step_kernel,
        grid=(n_seq,),
        in_specs=[tok(ML_WIDTH), tok(ML_WIDTH), tok(ML_WIDTH), tok(128), st_c, st_n, st_m],
        out_specs=[tok(ML_WIDTH), st_c, st_n, st_m],
        out_shape=[jax.ShapeDtypeStruct((n_seq, rows, ML_WIDTH), f32),
                   jax.ShapeDtypeStruct((n_seq, ML_HEADS, ML_DH, ML_DH), f32),
                   jax.ShapeDtypeStruct((n_seq, ML_HEADS, ML_DH), f32),
                   jax.ShapeDtypeStruct((n_seq, ML_HEADS, 128), f32)],
        compiler_params=pltpu.CompilerParams(dimension_semantics=("arbitrary",), vmem_limit_bytes=VMEM_LIMIT),
        name="mlstm_step",
    )(r3(mq), r3(mk), r3(mv), r3(gates), state_C, state_n, m_in)
    return h.reshape(n_seq * rows, ML_WIDTH), C, n, m[:, :, 0]


POST_TM = 256
MEM_SEQS = 4


def _mix_out_kernel(n_first, xn_ref, mla_a, mla_b, mlh_a, mlh_b, og_ref, wout_ref, g_ref, b_ref, wmq_ref, x1_ref, qm_ref):
    mixed = jnp.concatenate([_pick(n_first, mla_a, mla_b), og_ref[...] * _pick(n_first, mlh_a, mlh_b)],
                            axis=1).astype(jnp.bfloat16)
    mix = jnp.dot(mixed, wout_ref[...], preferred_element_type=jnp.float32)
    x1 = _layer_norm_rows(ALPHA * xn_ref[...] + mix, g_ref[...], b_ref[...])
    x1_ref[...] = x1
    qm_ref[...] = jnp.dot(x1.astype(jnp.bfloat16), wmq_ref[...],
                          preferred_element_type=jnp.float32).astype(jnp.bfloat16)


def mix_out(xn, mla_a, mla_b, mlh_a, mlh_b, o_gate, w_out, ln_g, ln_b, w_mq):
    T = xn.shape[0]
    n_first = mla_a.shape[0] // POST_TM
    assert mla_a.shape[0] % POST_TM == 0
    bf = jnp.bfloat16
    row = lambda n: pl.BlockSpec((POST_TM, n), lambda i: (i, 0))
    full = lambda shape: pl.BlockSpec(shape, lambda i: (0,) * len(shape))
    return pl.pallas_call(
        functools.partial(_mix_out_kernel, n_first),
        grid=(T // POST_TM,),
        in_specs=[row(D_MODEL), *_split_specs(POST_TM, MLA_WIDTH, n_first), *_split_specs(POST_TM, ML_WIDTH, n_first),
                  row(ML_WIDTH), full((D_MIX, D_MODEL)),
                  full((1, D_MODEL)), full((1, D_MODEL)), full((D_MODEL, D_MODEL))],
        out_specs=[row(D_MODEL), row(D_MODEL)],
        out_shape=[jax.ShapeDtypeStruct((T, D_MODEL), jnp.float32), jax.ShapeDtypeStruct((T, D_MODEL), bf)],
        compiler_params=pltpu.CompilerParams(dimension_semantics=("arbitrary",), vmem_limit_bytes=VMEM_LIMIT),
        name="mix_out",
    )(xn, mla_a, mla_b, mlh_a, mlh_b, o_gate, w_out.astype(bf), ln_g.reshape(1, -1), ln_b.reshape(1, -1),
      w_mq.reshape(D_MODEL, D_MODEL).astype(bf))


def _mem_attn_kernel(x1_ref, qm_ref, mk_ref, mv_ref, wmo_ref, g_ref, b_ref, x2_ref):
    q = qm_ref[...]
    nt = (((1,), (1,)), ((), ()))
    outs = []
    for h in range(MEM_HEADS):
        c = slice(h * MEM_HD, (h + 1) * MEM_HD)
        mk, mv = mk_ref[:, c].astype(jnp.bfloat16), mv_ref[:, c].astype(jnp.bfloat16)
        s = lax.dot_general(q[:, c], mk, nt, preferred_element_type=jnp.float32) * (MEM_HD ** -0.5)
        m = jnp.max(s, axis=-1, keepdims=True)
        p = jnp.exp(s - m)
        p = p / jnp.sum(p, axis=-1, keepdims=True)
        outs.append(jnp.dot(p.astype(jnp.bfloat16), mv, preferred_element_type=jnp.float32))
    o = jnp.concatenate(outs, axis=1).astype(jnp.bfloat16)
    att = jnp.dot(o, wmo_ref[...], preferred_element_type=jnp.float32)
    x2_ref[...] = _layer_norm_rows(ALPHA * x1_ref[...] + att, g_ref[...], b_ref[...])


def _mem_attn_sample_kernel(x1_ref, qm_ref, mk_ref, mv_ref, wmo_ref, g_ref, b_ref, x2_ref, o_scr):
    s_id = pl.program_id(0)
    n_here, tq = qm_ref.shape[:2]
    bf, f32 = jnp.bfloat16, jnp.float32
    for i in range(n_here):
        q = qm_ref[i]
        qs = jnp.concatenate([q[:, h * MEM_HD:(h + 1) * MEM_HD] for h in range(MEM_HEADS)], axis=0)
        kall = mk_ref[i].reshape(N_MEM * MEM_HEADS, MEM_HD).astype(bf)
        vall = mv_ref[i].reshape(N_MEM * MEM_HEADS, MEM_HD).astype(bf)
        s = lax.dot_general(qs, kall, (((1,), (1,)), ((), ())), preferred_element_type=f32) * (MEM_HD ** -0.5)
        row_h = lax.broadcasted_iota(jnp.int32, s.shape, 0) // tq
        col_h = lax.broadcasted_iota(jnp.int32, s.shape, 1) % MEM_HEADS
        s = jnp.where(row_h == col_h, s, NEG_INF)
        p = jnp.exp(s - jnp.max(s, axis=-1, keepdims=True))
        p = p / jnp.sum(p, axis=-1, keepdims=True)
        o = jnp.dot(p.astype(bf), vall, preferred_element_type=f32)
        o_scr[s_id * n_here + i] = jnp.concatenate([o[h * tq:(h + 1) * tq] for h in range(MEM_HEADS)], axis=1)

    @pl.when(s_id == pl.num_programs(0) - 1)
    def _():
        o_all = o_scr[...].reshape(x1_ref.shape).astype(bf)
        att = jnp.dot(o_all, wmo_ref[...], preferred_element_type=f32)
        x2_ref[...] = _layer_norm_rows(ALPHA * x1_ref[...] + att, g_ref[...], b_ref[...])


def mem_attend_ln(x1, qm, mem_k, mem_v, w_mo, ln_g, ln_b, n_seq, rows_per_seq):
    bf = jnp.bfloat16
    wmo = w_mo.reshape(D_MODEL, D_MODEL).astype(bf)
    g, b = ln_g.reshape(1, -1), ln_b.reshape(1, -1)
    cp = pltpu.CompilerParams(dimension_semantics=("arbitrary",) * 2, vmem_limit_bytes=VMEM_LIMIT)
    mem = pl.BlockSpec((None, N_MEM, D_MODEL), lambda s, i: (s, 0, 0))
    full = lambda shape: pl.BlockSpec(shape, lambda s, i: (0,) * len(shape))
    if rows_per_seq % POST_TM == 0:
        nb = rows_per_seq // POST_TM
        tok = pl.BlockSpec((POST_TM, D_MODEL), lambda s, i: (s * nb + i, 0))
        return pl.pallas_call(
            _mem_attn_kernel, grid=(n_seq, nb),
            in_specs=[tok, tok, mem, mem, full((D_MODEL, D_MODEL)), full((1, D_MODEL)), full((1, D_MODEL))],
            out_specs=tok, out_shape=jax.ShapeDtypeStruct((n_seq * rows_per_seq, D_MODEL), jnp.float32),
            compiler_params=cp, name="mem_attn_prompt",
        )(x1, qm, mem_k, mem_v, wmo, g, b)
    rows = n_seq * rows_per_seq
    assert n_seq % MEM_SEQS == 0
    mem = pl.BlockSpec((MEM_SEQS, N_MEM, MEM_HEADS, MEM_HD), lambda s: (s, 0, 0, 0))
    res = pl.BlockSpec((rows, D_MODEL), lambda s: (0, 0))
    full = lambda shape: pl.BlockSpec(shape, lambda s: (0,) * len(shape))
    return pl.pallas_call(
        _mem_attn_sample_kernel, grid=(n_seq // MEM_SEQS,),
        in_specs=[res, pl.BlockSpec((MEM_SEQS, rows_per_seq, D_MODEL), lambda s: (s, 0, 0)), mem, mem,
                  full((D_MODEL, D_MODEL)), full((1, D_MODEL)), full((1, D_MODEL))],
        out_specs=res, out_shape=jax.ShapeDtypeStruct((rows, D_MODEL), jnp.float32),
        scratch_shapes=[pltpu.VMEM((n_seq, rows_per_seq, D_MODEL), jnp.float32)],
        compiler_params=pltpu.CompilerParams(dimension_semantics=("arbitrary",), vmem_limit_bytes=VMEM_LIMIT),
        name="mem_attn_sample",
    )(x1, qm.reshape(n_seq, rows_per_seq, D_MODEL), mem_k, mem_v, wmo, g, b)


def _mem_kv_kernel(m_ref, w_ref, o_ref):
    o_ref[...] = jnp.dot(m_ref[...].astype(jnp.bfloat16), w_ref[...], preferred_element_type=jnp.float32)


def mem_kv(mem, w_mk, w_mv):
    B = mem.shape[0]
    w = jnp.concatenate([w_mk.reshape(D_MODEL, D_MODEL), w_mv.reshape(D_MODEL, D_MODEL)], axis=1).astype(jnp.bfloat16)
    out = pl.pallas_call(
        _mem_kv_kernel, grid=(B,),
        in_specs=[pl.BlockSpec((N_MEM, D_MODEL), lambda i: (i, 0)), pl.BlockSpec((D_MODEL, 2 * D_MODEL), lambda i: (0, 0))],
        out_specs=pl.BlockSpec((N_MEM, 2 * D_MODEL), lambda i: (i, 0)),
        out_shape=jax.ShapeDtypeStruct((B * N_MEM, 2 * D_MODEL), jnp.float32),
        compiler_params=pltpu.CompilerParams(dimension_semantics=("arbitrary",), vmem_limit_bytes=VMEM_LIMIT),
        name="mem_kv",
    )(mem.reshape(B * N_MEM, D_MODEL), w)
    mk = out[:, :D_MODEL].reshape(B, N_MEM, MEM_HEADS, MEM_HD)
    mv = out[:, D_MODEL:].reshape(B, N_MEM, MEM_HEADS, MEM_HD)
    return mk, mv


PEER_RT = 256
PEER_TB = 512
PEER_EB = 1024
PEER_VMEM_LIMIT = 58 * 1024 * 1024


def _top16_rows(s, row_id, exact_ties):
    big = float(2 ** 20)
    out_id = lax.broadcasted_iota(jnp.int32, (PEER_TOPK, s.shape[1]), 0)
    stacked = jnp.zeros((PEER_TOPK, s.shape[1]), jnp.float32)
    rank = jnp.full(s.shape, float(PEER_TOPK), jnp.float32)
    rows = []
    for k in range(PEER_TOPK):
        m = jnp.max(s, axis=0, keepdims=True)
        hit = s == m
        if exact_ties:
            hit = row_id == jnp.min(jnp.where(hit, row_id, big), axis=0, keepdims=True)
        s = jnp.where(hit, NEG_INF, s)
        rank = jnp.where(hit, float(k), rank)
        rows.append(m)
        stacked = jnp.where(out_id == k, m, stacked)
    return rows, stacked, rank, s


def _peer_route_head(s1, s2, exact_ties):
    tb = s1.shape[1]
    row128 = lax.broadcasted_iota(jnp.int32, (PEER_NKEYS, tb), 0).astype(jnp.float32)
    sub8 = lax.broadcasted_iota(jnp.int32, (8, tb), 0)
    sub8f = sub8.astype(jnp.float32)
    r1, v1, rank1, left1 = _top16_rows(s1, row128, exact_ties)
    r2, v2, rank2, left2 = _top16_rows(s2, row128, exact_ties)
    groups, ids = [], []
    for b in range(8):
        lim = PEER_TOPK // (b + 1)
        for a0 in range(0, lim, 8):
            g = v1[a0:a0 + 8] + r2[b]
            if lim - a0 < 8:
                g = jnp.where(sub8 < lim - a0, g, NEG_INF)
            groups.append(g)
            ids.append((sub8f + float(a0)) * float(PEER_TOPK) + float(b))
    groups.append(r1[0] + v2[8:16])
    ids.append(sub8f + 8.0)
    cand = jnp.concatenate(groups, axis=0)
    vals, _, _, left = _top16_rows(cand, jnp.concatenate(ids, axis=0), exact_ties)
    z = jnp.ones_like(vals[0])
    for k in range(1, PEER_TOPK):
        z = z + jnp.exp(vals[k] - vals[0])
    taken = jnp.where((left == NEG_INF) & (cand > NEG_INF), 1.0, 0.0)
    cnt_lo = jnp.zeros((8, tb), jnp.float32)
    gi = 0
    for b in range(8):
        for a0 in range(0, PEER_TOPK // (b + 1), 8):
            if a0 == 0:
                cnt_lo = cnt_lo + taken[gi * 8:(gi + 1) * 8]
            else:
                cnt_hi = taken[gi * 8:(gi + 1) * 8]
            gi += 1
    tail = jnp.sum(taken[gi * 8:(gi + 1) * 8], axis=0, keepdims=True)
    cnt_lo = cnt_lo + jnp.where(sub8 == 0, tail, 0.0)
    lim_full = jnp.full((PEER_NKEYS, tb), -1.0, jnp.float32)
    for a in range(PEER_TOPK):
        cnt = cnt_lo if a < 8 else cnt_hi
        lim_full = jnp.where(rank1 == float(a), cnt[a % 8:a % 8 + 1] - 1.0, lim_full)
    n_taken = (jnp.sum(jnp.where(left1 == NEG_INF, 1.0, 0.0), axis=0, keepdims=True)
               + jnp.sum(jnp.where(left2 == NEG_INF, 1.0, 0.0), axis=0, keepdims=True)
               + jnp.sum(taken, axis=0, keepdims=True))
    return (rank2.astype(jnp.bfloat16), lim_full, jnp.exp(s1 - r1[0]) / z,
            jnp.exp(s2 - r2[0]).astype(jnp.bfloat16), n_taken)


def _peer_route_kernel(n_first, xa_ref, xb_ref, wq_ref, k1_ref, k2_ref, rk2_ref, lim_ref, p1_ref, p2_ref, qt_scr):
    half = PEER_DKEY // 2
    xb = _pick(n_first, xa_ref, xb_ref).astype(jnp.bfloat16)
    qt_scr[...] = lax.dot_general(wq_ref[...], xb, (((1,), (1,)), ((), ())), preferred_element_type=jnp.float32)

    def head(h, carry):
        r0 = pl.multiple_of(h * PEER_DKEY, PEER_DKEY)
        q1 = qt_scr[pl.ds(r0, half), :].astype(jnp.bfloat16)
        q2 = qt_scr[pl.ds(r0 + half, half), :].astype(jnp.bfloat16)
        s1 = jnp.dot(k1_ref[...], q1, preferred_element_type=jnp.float32)
        s2 = jnp.dot(k2_ref[...], q2, preferred_element_type=jnp.float32)

        def emit(exact_ties):
            rk2, lim, p1, p2, n_taken = _peer_route_head(s1, s2, exact_ties)
            rk2_ref[h] = rk2
            lim_ref[h] = lim
            p1_ref[h] = p1
            p2_ref[h] = p2
            return n_taken

        n_taken = emit(False)
        merged = jnp.max(jnp.abs(n_taken - 3.0 * PEER_TOPK)) > 0.0

        @pl.when(merged)
        def _():
            emit(True)

        return carry

    lax.fori_loop(0, PEER_HEADS, head, 0)


def _row_bf16(row):
    r16 = jnp.broadcast_to(row, (16, row.shape[1])).astype(jnp.bfloat16)
    return jnp.concatenate([r16] * (PEER_NKEYS // 16), axis=0)


def _peer_weights(rk2_ref, lim_ref, p1_ref, p2_ref, row0, r, tb):
    w = jnp.zeros((PEER_NKEYS, tb), jnp.bfloat16)
    for h in range(PEER_HEADS):
        p2h = p2_ref[h]
        lim8 = lim_ref[h, pl.ds(row0, 8), :]
        p18 = p1_ref[h, pl.ds(row0, 8), :]
        sel = rk2_ref[h] <= _row_bf16(lim8[r:r + 1, :])
        w = w + jnp.where(sel, p2h, jnp.zeros_like(p2h)) * _row_bf16(p18[r:r + 1, :])
    return w


def _peer_dense_kernel(n_first, xa_ref, xb_ref, rk2_ref, lim_ref, p1_ref, p2_ref, u0_ref, ua_ref, ub_ref, vt_ref, g_ref, b_ref,
                       oa_ref, ob_ref, xb_scr, yt_scr, ht0_scr, ht1_scr):
    k = pl.program_id(1)
    n_i1 = PEER_EB // PEER_NKEYS
    tb = xa_ref.shape[0]
    nt = (((1,), (1,)), ((), ()))
    f32 = jnp.float32

    @pl.when(k == 0)
    def _():
        xb_scr[...] = _pick(n_first, xa_ref, xb_ref).astype(jnp.bfloat16)
        yt_scr[...] = jnp.zeros_like(yt_scr)
        ht0_scr[...] = lax.dot_general(u0_ref[...], xb_scr[...], nt, preferred_element_type=f32)

    xb = xb_scr[...]
    n_sub = 2
    sub = PEER_EB // n_sub

    def second_half(ht_scr, blk, vt_off, acc):
        for j in range(n_sub):
            pieces = []
            for cc in range(sub // PEER_NKEYS):
                w = _peer_weights(rk2_ref, lim_ref, p1_ref, p2_ref, pl.multiple_of(blk * n_i1, 8),
                                  j * (sub // PEER_NKEYS) + cc, tb)
                r0 = j * sub + cc * PEER_NKEYS
                hc = ht_scr[r0:r0 + PEER_NKEYS, :]
                gelu = 0.5 * hc * (1.0 + lax.erf(hc * (2.0 ** -0.5)))
                pieces.append(w * gelu.astype(jnp.bfloat16))
            at = jnp.concatenate(pieces, axis=0)
            c0 = vt_off + j * sub
            acc = acc + jnp.dot(vt_ref[:, c0:c0 + sub], at, preferred_element_type=f32)
        return acc

    acc = yt_scr[...]
    ht1_scr[...] = lax.dot_general(ua_ref[...], xb, nt, preferred_element_type=f32)
    acc = second_half(ht0_scr, 2 * k, 0, acc)
    ht0_scr[...] = lax.dot_general(ub_ref[...], xb, nt, preferred_element_type=f32)
    acc = second_half(ht1_scr, 2 * k + 1, PEER_EB, acc)
    yt_scr[...] = acc

    @pl.when(k == pl.num_programs(1) - 1)
    def _():
        z = ALPHA * _pick(n_first, xa_ref, xb_ref) + yt_scr[...].T
        res = _layer_norm_rows(z, g_ref[...], b_ref[...])
        first = pl.program_id(0) < n_first

        @pl.when(first)
        def _():
            oa_ref[...] = res

        @pl.when(jnp.logical_not(first))
        def _():
            ob_ref[...] = res


def peer_ln(x_a, x_b, w_pq, sub_k1, sub_k2, peer_u, peer_v, ln_g, ln_b):
    T = x_a.shape[0] + x_b.shape[0]
    assert x_a.shape[0] % PEER_TB == 0 and x_b.shape[0] % PEER_TB == 0 and PEER_TB % PEER_RT == 0
    nt = T // PEER_TB
    half = PEER_DKEY // 2
    wq_t = w_pq.reshape(D_MODEL, PEER_HEADS * PEER_DKEY).T.astype(jnp.bfloat16)
    sshape = jax.ShapeDtypeStruct((PEER_HEADS, PEER_NKEYS, T), jnp.float32)
    sspec = pl.BlockSpec((PEER_HEADS, PEER_NKEYS, PEER_RT), lambda j: (0, 0, j))
    rk2, lim, p1, p2 = pl.pallas_call(
        functools.partial(_peer_route_kernel, x_a.shape[0] // PEER_RT),
        grid=(T // PEER_RT,),
        in_specs=[*_split_specs(PEER_RT, D_MODEL, x_a.shape[0] // PEER_RT),
                  pl.BlockSpec((PEER_HEADS * PEER_DKEY, D_MODEL), lambda j: (0, 0)),
                  pl.BlockSpec((PEER_NKEYS, half), lambda j: (0, 0)),
                  pl.BlockSpec((PEER_NKEYS, half), lambda j: (0, 0))],
        out_specs=[sspec, sspec, sspec, sspec],
        out_shape=[jax.ShapeDtypeStruct(sshape.shape, jnp.bfloat16), sshape, sshape,
                   jax.ShapeDtypeStruct(sshape.shape, jnp.bfloat16)],
        scratch_shapes=[pltpu.VMEM((PEER_HEADS * PEER_DKEY, PEER_RT), jnp.float32)],
        compiler_params=pltpu.CompilerParams(dimension_semantics=("arbitrary",), vmem_limit_bytes=VMEM_LIMIT),
        name="peer_route",
    )(x_a, x_b, wq_t, sub_k1.astype(jnp.bfloat16), sub_k2.astype(jnp.bfloat16))

    u_b = peer_u.astype(jnp.bfloat16)
    vt_b = peer_v.T.astype(jnp.bfloat16)
    ne = PEER_N // PEER_EB
    assert ne % 2 == 0
    sspec2 = pl.BlockSpec((PEER_HEADS, PEER_NKEYS, PEER_TB), lambda j, k: (0, 0, j))
    return pl.pallas_call(
        functools.partial(_peer_dense_kernel, x_a.shape[0] // PEER_TB),
        grid=(nt, ne // 2),
        in_specs=[*_split_specs(PEER_TB, D_MODEL, x_a.shape[0] // PEER_TB),
                  sspec2, sspec2, sspec2, sspec2,
                  pl.BlockSpec((PEER_EB, D_MODEL), lambda j, k: (0, 0)),
                  pl.BlockSpec((PEER_EB, D_MODEL), lambda j, k: (2 * k + 1, 0)),
                  pl.BlockSpec((PEER_EB, D_MODEL), lambda j, k: (jnp.minimum(2 * k + 2, ne - 1), 0)),
                  pl.BlockSpec((D_MODEL, 2 * PEER_EB), lambda j, k: (0, k)),
                  pl.BlockSpec((1, D_MODEL), lambda j, k: (0, 0)),
                  pl.BlockSpec((1, D_MODEL), lambda j, k: (0, 0))],
        out_specs=list(_split_specs(PEER_TB, D_MODEL, x_a.shape[0] // PEER_TB)),
        out_shape=[jax.ShapeDtypeStruct(x_a.shape, jnp.float32), jax.ShapeDtypeStruct(x_b.shape, jnp.float32)],
        scratch_shapes=[pltpu.VMEM((PEER_TB, D_MODEL), jnp.bfloat16),
                        pltpu.VMEM((D_MODEL, PEER_TB), jnp.float32),
                        pltpu.VMEM((PEER_EB, PEER_TB), jnp.float32),
                        pltpu.VMEM((PEER_EB, PEER_TB), jnp.float32)],
        compiler_params=pltpu.CompilerParams(dimension_semantics=("arbitrary", "arbitrary"),
                                             vmem_limit_bytes=PEER_VMEM_LIMIT),
        name="peer_dense",
    )(x_a, x_b, rk2, lim, p1, p2, u_b, u_b, u_b, vt_b, ln_g.reshape(1, -1), ln_b.reshape(1, -1))


def kernel(x_prompt, x_sample, cache_kv_latent, cache_k_rope, state_C, state_n, state_m,
           cache_mem_k, cache_mem_v, page_table, mem_prompt, ln0_g, ln0_b, w_in, b_i, b_f,
           g_q, w_uq, g_kv, w_uk, w_uv, w_out, ln1_g, ln1_b, w_mq, w_mk, w_mv, w_mo,
           ln2_g, ln2_b, w_pq, sub_k1, sub_k2, peer_u, peer_v, ln3_g, ln3_b):
    B, S = x_prompt.shape[:2]
    NB, TQ = x_sample.shape[:2]
    past = page_table.shape[1] * PAGE_SIZE
    n_p = B * S
    l = 0
    pos = jnp.concatenate([jnp.tile(jnp.arange(S), B), jnp.tile(past + jnp.arange(TQ), NB)])
    (xn, q_lat, q_rope, kv, kvk, kr, mq, mk, mv, gates, o_gate) = mix_in(
        x_prompt.reshape(n_p, D_MODEL), x_sample.reshape(NB * TQ, D_MODEL), pos, ln0_g, ln0_b, w_in[l], b_i[l], b_f[l], g_q[l], w_uq[l], g_kv[l], w_uk[l])

    mla_p = mla_attend_prompt(q_lat, q_rope, kvk, w_uv[l], B, S)
    mlh_p, C_p, n_pst, m_p = mlstm_prompt(mq, mk, mv, gates, B, S)

    kv_s = kv[n_p:].reshape(NB, TQ, MLA_KV_LORA)
    kr_s = kr[n_p:, :MLA_ROPE].reshape(NB, TQ, MLA_ROPE)
    mla_s = mla_attend_sample(q_lat[n_p:].reshape(NB, TQ, -1), q_rope[n_p:].reshape(NB, TQ, -1), kv_s, kr_s,
                              cache_kv_latent.reshape(cache_kv_latent.shape[1:]),
                              jnp.swapaxes(cache_k_rope.reshape(cache_k_rope.shape[1:]), 1, 2), page_table, w_uv[l])
    mlh_s, C_s, n_s, m_s = mlstm_step(mq[n_p:], mk[n_p:], mv[n_p:], gates[n_p:], state_C.reshape(state_C.shape[1:]),
                                      state_n.reshape(state_n.shape[1:]), state_m.reshape(state_m.shape[1:]), NB, TQ)

    x1, qm = mix_out(xn, mla_p, mla_s, mlh_p, mlh_s, o_gate, w_out[l], ln1_g[l], ln1_b[l], w_mq[l])
    mk_p, mv_p = mem_kv(mem_prompt, w_mk[l], w_mv[l])
    x2_p = mem_attend_ln(x1, qm, mk_p.reshape(B, N_MEM, D_MODEL), mv_p.reshape(B, N_MEM, D_MODEL),
                         w_mo[l], ln2_g[l], ln2_b[l], B, S)
    x2_s = mem_attend_ln(x1[n_p:], qm[n_p:], cache_mem_k.reshape(cache_mem_k.shape[1:]),
                         cache_mem_v.reshape(cache_mem_v.shape[1:]), w_mo[l], ln2_g[l], ln2_b[l], NB, TQ)
    x3_p, x3_s = peer_ln(x2_p, x2_s, w_pq[l], sub_k1[l], sub_k2[l], peer_u[l], peer_v[l], ln3_g[l], ln3_b[l])
    st = lambda t: t[None]
    return (x3_p.reshape(B, S, D_MODEL), x3_s.reshape(NB, TQ, D_MODEL),
            st(kv[:n_p].reshape(B, S, MLA_KV_LORA)), st(kr[:n_p, :MLA_ROPE].reshape(B, S, MLA_ROPE)),
            st(C_p), st(n_pst), st(m_p), st(mk_p), st(mv_p),
            st(kv_s), st(kr_s), st(C_s), st(n_s), st(m_s))
```
